```python
import math
import jax, jax.numpy as jnp
from jax import lax
import numpy as np

D_MODEL = 1024
BATCH = 8
SEQ = 2048
DEPTH = 2
DEC_BATCH = 128
DEC_SEQ = 1
PAST_LEN = 16384
PAGE_SIZE = 128

N_META = 16
D_A = D_MODEL // 2
S5_GROUP = 16
S5_GROUPS = D_A // S5_GROUP
S5_STATE = 64
DT_MIN = 1e-3
DT_MAX = 1e-1
D_B = D_MODEL // 2
RW_HEAD = 64
RW_HEADS = D_B // RW_HEAD
LORA_W = 64
LORA_A = 64
LORA_G = 128
GN_EPS = 64e-5
P_RW = 3 * D_B + LORA_W + LORA_A + LORA_G
P_IN = D_A + P_RW + 2 * D_MODEL
D_FF = 2816
N_EXPERTS = 8
TOP_K = 2
D_EXPERT = 3584
N_DENSE = (DEPTH + 1) // 2
N_MOE = DEPTH // 2
RMS_EPS = 1e-6

kernel_name = 'hybrid_gated_s5_rwkv7_moe_step'


def rmsnorm(x, g):
    xf = x.astype(jnp.float32)
    y = xf * lax.rsqrt(jnp.mean(xf * xf, axis=-1, keepdims=True) + RMS_EPS)
    return (y * g.astype(jnp.float32)).astype(x.dtype)


def swiglu(h, w1, w3, w2):
    return (jax.nn.silu(h @ w1) * (h @ w3)) @ w2


def moe_ffn(h, w_router, w1, w3, w2):
    logits = (h @ w_router).astype(jnp.float32)
    top_v, top_i = lax.top_k(logits, TOP_K)
    gates = jax.nn.softmax(top_v, axis=-1)
    combine = jnp.sum(jax.nn.one_hot(top_i, N_EXPERTS, dtype=jnp.float32) * gates[..., None], axis=-2)
    combine = combine.astype(h.dtype)
    out = jnp.zeros_like(h)
    for e in range(N_EXPERTS):
        out = out + combine[..., e:e + 1] * swiglu(h, w1[e], w3[e], w2[e])
    return out


def _cplx_affine_combine(e1, e2):
    a1r, a1i, b1r, b1i = e1
    a2r, a2i, b2r, b2i = e2
    return (a1r * a2r - a1i * a2i,
            a1r * a2i + a1i * a2r,
            a2r * b1r - a2i * b1i + b2r,
            a2r * b1i + a2i * b1r + b2i)


def s5_branch(u, x0_re, x0_im, lam_re, lam_im, log_dt, b_re, b_im, c_re, c_im, d_skip, w_glu1, w_glu2):
    f32 = jnp.float32
    bsz, t_len, _ = u.shape
    uf = u.astype(f32).reshape(bsz, t_len, S5_GROUPS, S5_GROUP)
    lr = lam_re.astype(f32)
    li = lam_im.astype(f32)
    dt = jnp.exp(log_dt.astype(f32))[:, None]
    mag = jnp.exp(lr * dt)
    ab_re = mag * jnp.cos(li * dt)
    ab_im = mag * jnp.sin(li * dt)
    den = lr * lr + li * li
    q_re = ((ab_re - 1.0) * lr + ab_im * li) / den
    q_im = (ab_im * lr - (ab_re - 1.0) * li) / den
    bu_re = jnp.einsum('btgc,gpc->btgp', uf, b_re.astype(f32))
    bu_im = jnp.einsum('btgc,gpc->btgp', uf, b_im.astype(f32))
    bb_re = q_re * bu_re - q_im * bu_im
    bb_im = q_re * bu_im + q_im * bu_re
    x0r = x0_re.astype(f32)
    x0i = x0_im.astype(f32)
    bb_re = bb_re.at[:, 0].add(ab_re * x0r - ab_im * x0i)
    bb_im = bb_im.at[:, 0].add(ab_re * x0i + ab_im * x0r)
    a_re = jnp.broadcast_to(ab_re, (1, t_len) + ab_re.shape)
    a_im = jnp.broadcast_to(ab_im, (1, t_len) + ab_im.shape)
    _, _, xs_re, xs_im = lax.associative_scan(_cplx_affine_combine, (a_re, a_im, bb_re, bb_im), axis=1)
    y = (jnp.einsum('btgp,gcp->btgc', xs_re, c_re.astype(f32))
         - jnp.einsum('btgp,gcp->btgc', xs_im, c_im.astype(f32))
         + d_skip.astype(f32).reshape(S5_GROUPS, S5_GROUP) * uf)
    y = jax.nn.gelu(y.reshape(bsz, t_len, D_A)).astype(u.dtype)
    out = (y @ w_glu1) * jax.nn.sigmoid(y @ w_glu2)
    return out, xs_re[:, -1], xs_im[:, -1]


def rwkv7_branch(p, shift0, s0, mu, w0, w_up, a0, a_up, g_up, k_k, k_a, r_k, gn_g, gn_b, w_bo):
    f32 = jnp.float32
    bsz, t_len, _ = p.shape
    prev = jnp.concatenate([shift0[:, None].astype(p.dtype), p[:, :-1]], axis=1)
    z = (p + (prev - p) * mu).astype(f32)
    o1, o2, o3 = D_B, 2 * D_B, 3 * D_B
    o4 = o3 + LORA_W
    o5 = o4 + LORA_A
    r = z[..., :o1]
    k = z[..., o1:o2]
    v = z[..., o2:o3]
    zw = z[..., o3:o4]
    za = z[..., o4:o5]
    zg = z[..., o5:]
    w_log = -jax.nn.softplus(-(w0 + jnp.tanh(zw) @ w_up)) - 0.5
    decay = jnp.exp(-jnp.exp(w_log))
    a = jax.nn.sigmoid(a0 + za @ a_up)
    g = jax.nn.sigmoid(zg) @ g_up

    def heads(t):
        return t.reshape(bsz, t_len, RW_HEADS, RW_HEAD)

    kk = heads(k * k_k)
    kk = kk / jnp.maximum(jnp.linalg.norm(kk, axis=-1, keepdims=True), 1e-12)
    k = k * (1.0 + (a - 1.0) * k_a)
    rh, kh, vh, wh, ah = heads(r), heads(k), heads(v), heads(decay), heads(a)

    def tm(t):
        return jnp.swapaxes(t, 0, 1)

    def step(s, inp):
        r_t, w_t, k_t, v_t, kk_t, a_t = inp
        sa = jnp.einsum('bhvk,bhk->bhv', s, -kk_t)
        s = (s * w_t[:, :, None, :]
             + sa[..., None] * (kk_t * a_t)[:, :, None, :]
             + v_t[..., None] * k_t[:, :, None, :])
        return s, jnp.einsum('bhvk,bhk->bhv', s, r_t)

    s_fin, o = lax.scan(step, s0.astype(f32), (tm(rh), tm(wh), tm(kh), tm(vh), tm(kk), tm(ah)))
    o = tm(o)
    mean = jnp.mean(o, axis=-1, keepdims=True)
    var = jnp.mean(jnp.square(o - mean), axis=-1, keepdims=True)
    o = ((o - mean) * lax.rsqrt(var + GN_EPS)).reshape(bsz, t_len, D_B) * gn_g + gn_b
    bonus = (jnp.sum(rh * kh * r_k, axis=-1, keepdims=True) * vh).reshape(bsz, t_len, D_B)
    out = ((o + bonus) * g).astype(p.dtype) @ w_bo
    return out, s_fin, p[:, -1].astype(f32)


def trunk(x, s5_re0, s5_im0, rw_s0, rw_sh0, W):
    new_re, new_im, new_s, new_sh = [], [], [], []
    o_a = D_A
    o_b = D_A + P_RW
    o_c = o_b + D_MODEL
    for l in range(DEPTH):
        h = rmsnorm(x, W['norm_mix'][l])
        proj = h @ W['w_in'][l]
        ya, xr, xi = s5_branch(proj[..., :o_a], s5_re0[l], s5_im0[l],
                               W['s5_lam_re'][l], W['s5_lam_im'][l], W['s5_log_dt'][l],
                               W['s5_b_re'][l], W['s5_b_im'][l], W['s5_c_re'][l], W['s5_c_im'][l],
                               W['s5_d'][l], W['s5_glu1'][l], W['s5_glu2'][l])
        yb, s_fin, sh = rwkv7_branch(proj[..., o_a:o_b], rw_sh0[l], rw_s0[l],
                                     W['rw_mu'][l], W['rw_w0'][l], W['rw_w_up'][l], W['rw_a0'][l],
                                     W['rw_a_up'][l], W['rw_g_up'][l], W['rw_k_k'][l], W['rw_k_a'][l],
                                     W['rw_r_k'][l], W['rw_gn_g'][l], W['rw_gn_b'][l], W['rw_w_bo'][l])
        m = (jax.nn.sigmoid(proj[..., o_b:o_c]) * ya.astype(x.dtype)
             + jax.nn.sigmoid(proj[..., o_c:]) * yb.astype(x.dtype))
        x = x + m @ W['w_out'][l]
        h2 = rmsnorm(x, W['norm_ffn'][l])
        j = l // 2
        if l % 2 == 0:
            x = x + swiglu(h2, W['ffn_w1'][j], W['ffn_w3'][j], W['ffn_w2'][j])
        else:
            x = x + moe_ffn(h2, W['moe_router'][j], W['moe_w1'][j], W['moe_w3'][j], W['moe_w2'][j])
        new_re.append(xr)
        new_im.append(xi)
        new_s.append(s_fin)
        new_sh.append(sh)
    y = rmsnorm(x, W['norm_final'])
    return (y, jnp.stack(new_re), jnp.stack(new_im), jnp.stack(new_s), jnp.stack(new_sh))


def setup_inputs(seed: int = 0) -> dict:
    key = jax.random.key(seed)
    ks = list(jax.random.split(key, 48))
    it = iter(ks)
    f32 = jnp.float32

    def nrm(shape, scale):
        return scale * jax.random.normal(next(it), shape, f32)

    def unif(shape, lo, hi):
        return jax.random.uniform(next(it), shape, f32, lo, hi)

    G, P, C, H, N = S5_GROUPS, S5_STATE, S5_GROUP, RW_HEADS, RW_HEAD
    lam_im_base = jnp.broadcast_to(math.pi * jnp.arange(P, dtype=f32), (DEPTH, G, P))
    return {
        'x_prompt': nrm((BATCH, SEQ, D_MODEL), 1.0),
        'x_sample': nrm((DEC_BATCH, DEC_SEQ, D_MODEL), 1.0),
        'state_s5_re': nrm((DEPTH, DEC_BATCH, G, P), 0.5),
        'state_s5_im': nrm((DEPTH, DEC_BATCH, G, P), 0.5),
        'state_rwkv': nrm((DEPTH, DEC_BATCH, H, N, N), 0.2),
        'state_shift': nrm((DEPTH, DEC_BATCH, P_RW), 1.0),
        'meta_tokens': nrm((N_META, D_MODEL), 1.0),
        'norm_mix': 1.0 + nrm((DEPTH, D_MODEL), 0.05),
        'w_in': nrm((DEPTH, D_MODEL, P_IN), D_MODEL ** -0.5),
        's5_lam_re': -0.5 + nrm((DEPTH, G, P), 0.01),
        's5_lam_im': lam_im_base + nrm((DEPTH, G, P), 0.01),
        's5_log_dt': unif((DEPTH, G), math.log(DT_MIN), math.log(DT_MAX)),
        's5_b_re': nrm((DEPTH, G, P, C), (2 * C) ** -0.5),
        's5_b_im': nrm((DEPTH, G, P, C), (2 * C) ** -0.5),
        's5_c_re': nrm((DEPTH, G, C, P), (2 * P) ** -0.5),
        's5_c_im': nrm((DEPTH, G, C, P), (2 * P) ** -0.5),
        's5_d': nrm((DEPTH, D_A), 1.0),
        's5_glu1': nrm((DEPTH, D_A, D_MODEL), D_A ** -0.5),
        's5_glu2': nrm((DEPTH, D_A, D_MODEL), D_A ** -0.5),
        'rw_mu': unif((DEPTH, P_RW), 0.0, 1.0),
        'rw_w0': -2.0 + nrm((DEPTH, D_B), 0.5),
        'rw_w_up': nrm((DEPTH, LORA_W, D_B), 0.1),
        'rw_a0': nrm((DEPTH, D_B), 0.1),
        'rw_a_up': nrm((DEPTH, LORA_A, D_B), 0.5 * LORA_A ** -0.5),
        'rw_g_up': nrm((DEPTH, LORA_G, D_B), LORA_G ** -0.5),
        'rw_k_k': 0.85 + nrm((DEPTH, D_B), 0.05),
        'rw_k_a': 1.0 + nrm((DEPTH, D_B), 0.05),
        'rw_r_k': nrm((DEPTH, H, N), 0.1),
        'rw_gn_g': 1.0 + nrm((DEPTH, D_B), 0.05),
        'rw_gn_b': nrm((DEPTH, D_B), 0.02),
        'rw_w_bo': nrm((DEPTH, D_B, D_MODEL), D_B ** -0.5),
        'w_out': nrm((DEPTH, D_MODEL, D_MODEL), D_MODEL ** -0.5),
        'norm_ffn': 1.0 + nrm((DEPTH, D_MODEL), 0.05),
        'ffn_w1': nrm((N_DENSE, D_MODEL, D_FF), D_MODEL ** -0.5),
        'ffn_w3': nrm((N_DENSE, D_MODEL, D_FF), D_MODEL ** -0.5),
        'ffn_w2': nrm((N_DENSE, D_FF, D_MODEL), D_FF ** -0.5),
        'moe_router': nrm((N_MOE, D_MODEL, N_EXPERTS), D_MODEL ** -0.5),
        'moe_w1': nrm((N_MOE, N_EXPERTS, D_MODEL, D_EXPERT), D_MODEL ** -0.5),
        'moe_w3': nrm((N_MOE, N_EXPERTS, D_MODEL, D_EXPERT), D_MODEL ** -0.5),
        'moe_w2': nrm((N_MOE, N_EXPERTS, D_EXPERT, D_MODEL), D_EXPERT ** -0.5),
        'norm_final': 1.0 + nrm((D_MODEL,), 0.05),
    }


def reference(x_prompt, x_sample, state_s5_re, state_s5_im, state_rwkv, state_shift,
              meta_tokens, norm_mix, w_in, s5_lam_re, s5_lam_im, s5_log_dt, s5_b_re, s5_b_im,
              s5_c_re, s5_c_im, s5_d, s5_glu1, s5_glu2, rw_mu, rw_w0, rw_w_up, rw_a0, rw_a_up,
              rw_g_up, rw_k_k, rw_k_a, rw_r_k, rw_gn_g, rw_gn_b, rw_w_bo, w_out, norm_ffn,
              ffn_w1, ffn_w3, ffn_w2, moe_router, moe_w1, moe_w3, moe_w2, norm_final):
    W = dict(norm_mix=norm_mix, w_in=w_in, s5_lam_re=s5_lam_re, s5_lam_im=s5_lam_im,
             s5_log_dt=s5_log_dt, s5_b_re=s5_b_re, s5_b_im=s5_b_im, s5_c_re=s5_c_re,
             s5_c_im=s5_c_im, s5_d=s5_d, s5_glu1=s5_glu1, s5_glu2=s5_glu2, rw_mu=rw_mu,
             rw_w0=rw_w0, rw_w_up=rw_w_up, rw_a0=rw_a0, rw_a_up=rw_a_up, rw_g_up=rw_g_up,
             rw_k_k=rw_k_k, rw_k_a=rw_k_a, rw_r_k=rw_r_k, rw_gn_g=rw_gn_g, rw_gn_b=rw_gn_b,
             rw_w_bo=rw_w_bo, w_out=w_out, norm_ffn=norm_ffn, ffn_w1=ffn_w1, ffn_w3=ffn_w3,
             ffn_w2=ffn_w2, moe_router=moe_router, moe_w1=moe_w1, moe_w3=moe_w3,
             moe_w2=moe_w2, norm_final=norm_final)
    f32 = jnp.float32
    bsz = x_prompt.shape[0]
    meta = jnp.broadcast_to(meta_tokens.astype(x_prompt.dtype)[None], (bsz, N_META, D_MODEL))
    xp = jnp.concatenate([meta, x_prompt], axis=1)
    z_re = jnp.zeros((DEPTH, bsz, S5_GROUPS, S5_STATE), f32)
    z_s = jnp.zeros((DEPTH, bsz, RW_HEADS, RW_HEAD, RW_HEAD), f32)
    z_sh = jnp.zeros((DEPTH, bsz, P_RW), f32)
    yp, p_re, p_im, p_rw, p_sh = trunk(xp, z_re, z_re, z_s, z_sh, W)
    ys, s_re, s_im, s_rw, s_sh = trunk(x_sample, state_s5_re, state_s5_im, state_rwkv, state_shift, W)
    return (yp[:, N_META:], ys, p_re, p_im, p_rw, p_sh, s_re, s_im, s_rw, s_sh)
```

```python
import functools

import jax
import jax.numpy as jnp
from jax import lax
from jax.experimental import pallas as pl
from jax.experimental.pallas import tpu as pltpu

F32 = jnp.float32
BF16 = jnp.bfloat16
HIGHEST = lax.Precision.HIGHEST

D_MODEL = 1024
DEPTH = 2
N_META = 16
D_A = 512
S5_GROUP = 16
S5_GROUPS = 32
S5_STATE = 64
S5_TILES = 4
S5_HALF = S5_GROUPS * S5_STATE
D_B = 512
RW_HEAD = 64
RW_HEADS = 8
LORA_W = 64
LORA_A = 64
LORA_G = 128
GN_EPS = 64e-5
P_RW = 3 * D_B + LORA_W + LORA_A + LORA_G
D_FF = 2816
N_EXPERTS = 8
D_EXPERT = 3584
RMS_EPS = 1e-6
LANES = 128
SUBLANES = 8
VMEM_LIMIT_MB = 56


def _pick_tile(n, pref, mult=SUBLANES):
    best = None
    for d in range(mult, min(n, pref) + 1, mult):
        if n % d == 0:
            best = d
    return best if best is not None else n


def _cparams(sem):
    return pltpu.CompilerParams(dimension_semantics=sem,
                                vmem_limit_bytes=VMEM_LIMIT_MB * 1024 * 1024)


def _const_spec(shape):
    nd = len(shape)
    return pl.BlockSpec(shape, lambda *_: (0,) * nd, pipeline_mode=pl.Buffered(1))


def _mm(a, b, hp):
    if hp:
        return jnp.dot(a.astype(F32), b, precision=HIGHEST, preferred_element_type=F32)
    return jnp.dot(a.astype(BF16), b, preferred_element_type=F32)


def _segsum(x, ones, hp):
    hi = x.astype(BF16)
    r1 = x - hi.astype(F32)
    mid = r1.astype(BF16)
    out = (jnp.dot(hi, ones, preferred_element_type=F32)
           + jnp.dot(mid, ones, preferred_element_type=F32))
    if hp:
        lo = (r1 - mid.astype(F32)).astype(BF16)
        out = out + jnp.dot(lo, ones, preferred_element_type=F32)
    return out


def _sigmoid(x):
    return 1.0 / (1.0 + jnp.exp(-x))


def _softplus(x):
    return jnp.maximum(x, 0.0) + jnp.log1p(jnp.exp(-jnp.abs(x)))


def _rms(x, g):
    return x * lax.rsqrt(jnp.mean(x * x, axis=-1, keepdims=True) + RMS_EPS) * g


def _s5_disc_kernel(lr_ref, li_ref, ldt_ref, btr_ref, bti_ref, abr_o, abi_o, bbr_o, bbi_o):
    lr = lr_ref[...]
    li = li_ref[...]
    dt = jnp.exp(ldt_ref[...])
    mag = jnp.exp(lr * dt)
    ab_re = mag * jnp.cos(li * dt)
    ab_im = mag * jnp.sin(li * dt)
    den = lr * lr + li * li
    q_re = ((ab_re - 1.0) * lr + ab_im * li) / den
    q_im = (ab_im * lr - (ab_re - 1.0) * li) / den
    abr_o[...] = ab_re
    abi_o[...] = ab_im
    for c in range(S5_GROUP):
        b_re = btr_ref[c]
        b_im = bti_ref[c]
        bbr_o[c] = q_re * b_re - q_im * b_im
        bbi_o[c] = q_re * b_im + q_im * b_re


def _s5_discretize(lam_re, lam_im, log_dt, b_re, b_im):
    gp = jax.ShapeDtypeStruct((S5_GROUPS, S5_STATE), F32)
    cgp = jax.ShapeDtypeStruct((S5_GROUP, S5_GROUPS, S5_STATE), F32)
    return pl.pallas_call(_s5_disc_kernel, out_shape=(gp, gp, cgp, cgp), name="s5_disc")(
        lam_re, lam_im, log_dt[:, None],
        jnp.transpose(b_re, (2, 0, 1)), jnp.transpose(b_im, (2, 0, 1)))


def _s5_kernel(x_ref, gm_ref, wu_ref, bbar_ref, ar_ref, ai_ref, x0r_ref, x0i_ref,
               cre_ref, cim_ref, d_ref, g1_ref, g2_ref,
               ya_ref, xr_out, xi_out, br_scr, bi_scr, sr_scr, si_scr, *, batch, steps, hp):
    @pl.when(pl.program_id(0) == 0)
    def _():
        sr_scr[...] = x0r_ref[...]
        si_scr[...] = x0i_ref[...]

    h = _rms(x_ref[...], gm_ref[...])
    u = _mm(h, wu_ref[...], hp)
    half = S5_HALF // S5_TILES
    for j in range(S5_TILES):
        bbj = _mm(u[:, j * LANES:(j + 1) * LANES], bbar_ref[j], hp)
        br_scr[:, j * half:(j + 1) * half] = bbj[:, :half]
        bi_scr[:, j * half:(j + 1) * half] = bbj[:, half:]

    def step(t, carry):
        xr, xi = carry
        rows = pl.ds(pl.multiple_of(t * batch, batch), batch)
        ar = ar_ref[...]
        ai = ai_ref[...]
        nxr = ar * xr - ai * xi + br_scr[rows, :]
        nxi = ar * xi + ai * xr + bi_scr[rows, :]
        br_scr[rows, :] = nxr
        bi_scr[rows, :] = nxi
        return nxr, nxi

    carry = (sr_scr[...], si_scr[...])
    if steps == 1:
        carry = step(0, carry)
    else:
        carry = lax.fori_loop(0, steps, step, carry)
    sr_scr[...] = carry[0]
    si_scr[...] = carry[1]
    xr_out[...] = carry[0]
    xi_out[...] = carry[1]

    ys = []
    for j in range(S5_TILES):
        ys.append(_mm(br_scr[:, j * half:(j + 1) * half], cre_ref[j], hp)
                  + _mm(bi_scr[:, j * half:(j + 1) * half], cim_ref[j], hp))
    y = jnp.concatenate(ys, axis=1) + d_ref[...] * u
    y = jax.nn.gelu(y)
    ya_ref[...] = _mm(y, g1_ref[...], hp) * _sigmoid(_mm(y, g2_ref[...], hp))


def _s5_branch(x, batch, tchunk, P, x0r, x0i, hp):
    rows = x.shape[0]
    rc = batch * tchunk
    nchunks = rows // rc
    assert nchunks * rc == rows
    row_spec = lambda w: pl.BlockSpec((rc, w), lambda c: (c, 0))
    state_spec = pl.BlockSpec((batch, S5_HALF), lambda c: (0, 0))
    consts = [P["g_mix"], P["w_u"], P["s5_bbar"], P["s5_ar"][batch], P["s5_ai"][batch], x0r, x0i,
              P["s5_cre"], P["s5_cim"], P["s5_d"], P["s5_glu1"], P["s5_glu2"]]
    return pl.pallas_call(
        functools.partial(_s5_kernel, batch=batch, steps=tchunk, hp=hp),
        grid=(nchunks,),
        in_specs=[row_spec(D_MODEL)] + [_const_spec(a.shape) for a in consts],
        out_specs=(row_spec(D_MODEL), state_spec, state_spec),
        out_shape=(jax.ShapeDtypeStruct((rows, D_MODEL), F32),
                   jax.ShapeDtypeStruct((batch, S5_HALF), F32),
                   jax.ShapeDtypeStruct((batch, S5_HALF), F32)),
        scratch_shapes=[pltpu.VMEM((rc, S5_HALF), F32), pltpu.VMEM((rc, S5_HALF), F32),
                        pltpu.VMEM((batch, S5_HALF), F32), pltpu.VMEM((batch, S5_HALF), F32)],
        compiler_params=_cparams(("arbitrary",)),
        name="s5_branch_hp" if hp else "s5_branch",
    )(x, *consts)


def _rw_prep_kernel(x_ref, gm_ref, wrw_ref, sh0_ref, mu_ref, wl_ref, w0_ref, a0_ref, gup_ref,
                    kk_ref, ka_ref, rk_ref, ones_ref,
                    r_o, w_o, k_o, v_o, kk_o, nb_o, g_o, bon_o, sh_o, carry_scr, *, batch, hp):
    @pl.when(pl.program_id(0) == 0)
    def _():
        carry_scr[...] = sh0_ref[...]

    h = _rms(x_ref[...], gm_ref[...])
    p = _mm(h, wrw_ref[...], hp)
    rows = p.shape[0]
    if rows > batch:
        prev = jnp.concatenate([carry_scr[...], p[:rows - batch]], axis=0)
    else:
        prev = carry_scr[...]
    last = p[rows - batch:]
    carry_scr[...] = last
    sh_o[...] = last
    z = p + (prev - p) * mu_ref[...]
    r = z[:, :D_B]
    k = z[:, D_B:2 * D_B]
    v = z[:, 2 * D_B:3 * D_B]
    zwa = z[:, 3 * D_B:3 * D_B + LORA_W + LORA_A]
    zg = z[:, 3 * D_B + LORA_W + LORA_A:]
    lane = lax.broadcasted_iota(jnp.int32, zwa.shape, 1)
    tw = jnp.where(lane < LORA_W, jnp.tanh(zwa), zwa)
    lw = _mm(tw, wl_ref[...], hp)
    w_log = -_softplus(-(w0_ref[...] + lw[:, :D_B])) - 0.5
    decay = jnp.exp(-jnp.exp(w_log))
    a = _sigmoid(a0_ref[...] + lw[:, D_B:])
    g = _mm(_sigmoid(zg), gup_ref[...], hp)
    kk = k * kk_ref[...]
    n2 = _segsum(kk * kk, ones_ref[...], hp)
    kkn = kk * lax.rsqrt(jnp.maximum(n2, 1e-24))
    k2 = k * (1.0 + (a - 1.0) * ka_ref[...])
    rk = _segsum(r * k2 * rk_ref[...], ones_ref[...], hp)
    r_o[...] = r
    w_o[...] = decay
    k_o[...] = k2
    v_o[...] = v
    kk_o[...] = kkn
    nb_o[...] = -(kkn * a)
    g_o[...] = g
    bon_o[...] = rk * v


def _rw_prep(x, batch, tchunk, P, sh0, hp):
    rows = x.shape[0]
    rc = batch * tchunk
    nchunks = rows // rc
    assert nchunks * rc == rows
    row_spec = lambda w: pl.BlockSpec((rc, w), lambda c: (c, 0))
    consts = [P["g_mix"], P["w_rw"], sh0, P["rw_mu"], P["rw_wl"], P["rw_w0"], P["rw_a0"], P["rw_g_up"],
              P["rw_k_k"], P["rw_k_a"], P["rw_r_k"], P["ones_head"]]
    vec = jax.ShapeDtypeStruct((rows, D_B), F32)
    return pl.pallas_call(
        functools.partial(_rw_prep_kernel, batch=batch, hp=hp),
        grid=(nchunks,),
        in_specs=[row_spec(D_MODEL)] + [_const_spec(a.shape) for a in consts],
        out_specs=tuple([row_spec(D_B)] * 8) + (pl.BlockSpec((batch, P_RW), lambda c: (0, 0)),),
        out_shape=tuple([vec] * 8) + (jax.ShapeDtypeStruct((batch, P_RW), F32),),
        scratch_shapes=[pltpu.VMEM((batch, P_RW), F32)],
        compiler_params=_cparams(("arbitrary",)),
        name="rw_prep_hp" if hp else "rw_prep",
    )(x, *consts)


RW_KH = RW_HEAD // 2
RW_VH = RW_HEAD // SUBLANES
N_KVEC = 5


def _rw_scan_kernel(kin_ref, vin_ref, s0_ref, o_ref, sfin_ref, s_scr, *, steps):
    @pl.when(pl.program_id(1) == 0)
    def _():
        s_scr[...] = s0_ref[...]

    def krow(t, vec, kh):
        return jnp.broadcast_to(kin_ref[t, pl.ds(vec * RW_KH + kh, 1), :], (SUBLANES, LANES))

    def step(t, carry):
        acc = [None] * RW_VH
        for kh in range(RW_KH):
            kkb = krow(t, 0, kh)
            for vh in range(RW_VH):
                prod = s_scr[vh, kh] * kkb
                acc[vh] = prod if kh == 0 else acc[vh] + prod
        sa = [a + pltpu.roll(a, LANES // 2, 1) for a in acc]
        vv = [vin_ref[t, vh * SUBLANES:(vh + 1) * SUBLANES, :] for vh in range(RW_VH)]
        oacc = [None] * RW_VH
        for kh in range(RW_KH):
            wb = krow(t, 1, kh)
            nbb = krow(t, 2, kh)
            kb = krow(t, 3, kh)
            rb = krow(t, 4, kh)
            for vh in range(RW_VH):
                s = s_scr[vh, kh] * wb + sa[vh] * nbb + vv[vh] * kb
                s_scr[vh, kh] = s
                q = s * rb
                oacc[vh] = q if kh == 0 else oacc[vh] + q
        for vh in range(RW_VH):
            o_ref[t, vh * SUBLANES:(vh + 1) * SUBLANES, :] = oacc[vh] + pltpu.roll(oacc[vh], LANES // 2, 1)
        return carry

    if steps == 1:
        step(0, 0)
    else:
        lax.fori_loop(0, steps, step, 0)
    sfin_ref[...] = s_scr[...]


def _rw_scan(kin, vin, s0, tchunk):
    nlt, T = kin.shape[0], kin.shape[1]
    nchunks = T // tchunk
    assert nchunks * tchunk == T
    sshape = (RW_VH, RW_KH, SUBLANES, LANES)
    state_spec = pl.BlockSpec((None,) + sshape, lambda j, c: (j, 0, 0, 0, 0))
    return pl.pallas_call(
        functools.partial(_rw_scan_kernel, steps=tchunk),
        grid=(nlt, nchunks),
        in_specs=[pl.BlockSpec((None, tchunk, N_KVEC * RW_KH, LANES), lambda j, c: (j, c, 0, 0)),
                  pl.BlockSpec((None, tchunk, RW_HEAD, LANES), lambda j, c: (j, c, 0, 0)),
                  state_spec],
        out_specs=(pl.BlockSpec((None, tchunk, RW_HEAD, LANES), lambda j, c: (j, c, 0, 0)), state_spec),
        out_shape=(jax.ShapeDtypeStruct((nlt, T, RW_HEAD, LANES), F32),
                   jax.ShapeDtypeStruct((nlt,) + sshape, F32)),
        scratch_shapes=[pltpu.VMEM(sshape, F32)],
        compiler_params=_cparams(("arbitrary", "arbitrary")),
        name="rw_scan",
    )(kin, vin, s0)


def _rw_recurrence_long(vecs, batch, T, s0, tchunk):
    assert batch == SUBLANES
    r, w, k, v, kk, nb = vecs

    def klay(a):
        a = a.reshape(T, batch, RW_HEADS, RW_KH, 2)
        return jnp.transpose(a, (0, 3, 4, 1, 2)).reshape(T, RW_KH, LANES)

    kin = jnp.concatenate([klay(kk), klay(w), klay(nb), klay(k), klay(r)], axis=1)[None]
    vt = jnp.transpose(v.reshape(T, batch, RW_HEADS, RW_HEAD), (0, 3, 1, 2)).reshape(T, RW_HEAD, LANES // 2)
    vin = jnp.concatenate([vt, vt], axis=-1)[None]
    s = s0.reshape(batch, RW_HEADS, RW_VH, SUBLANES, RW_KH, 2)
    s = jnp.transpose(s, (2, 4, 3, 5, 0, 1)).reshape(1, RW_VH, RW_KH, SUBLANES, LANES)
    o, sfin = _rw_scan(kin, vin, s, tchunk)
    o = o[0, :, :, :LANES // 2].reshape(T, RW_HEAD, batch, RW_HEADS)
    o = jnp.transpose(o, (0, 2, 3, 1)).reshape(T * batch, D_B)
    sfin = sfin.reshape(RW_VH, RW_KH, SUBLANES, 2, batch, RW_HEADS)
    sfin = jnp.transpose(sfin, (4, 5, 0, 2, 1, 3)).reshape(batch, RW_HEADS, RW_HEAD, RW_HEAD)
    return o, sfin


def _rw_step_kernel(s_ref, kk_ref, w_ref, nb_ref, k_ref, r_ref, v_ref, ones_ref, s_o, o_o):
    s = s_ref[...]
    ones = ones_ref[...]
    sa = _segsum(s * kk_ref[...], ones, True)
    s = s * w_ref[...] + sa * nb_ref[...] + v_ref[...] * k_ref[...]
    s_o[...] = s
    o_o[...] = _segsum(s * r_ref[...], ones, True)


def _rw_recurrence_step(vecs, batch, s0, ones_half):
    r, w, k, v, kk, nb = vecs
    pairs = batch * RW_HEADS
    vrows = RW_HEAD // 2
    rows = pairs * vrows

    def kexp(a):
        a = a.reshape(pairs, 1, 1, RW_HEAD)
        return jnp.broadcast_to(a, (pairs, vrows, 2, RW_HEAD)).reshape(rows, LANES)

    vexp = jnp.broadcast_to(v.reshape(rows, 2, 1), (rows, 2, RW_HEAD)).reshape(rows, LANES)
    tr = _pick_tile(rows, 2048)
    spec = pl.BlockSpec((tr, LANES), lambda i: (i, 0))
    big = jax.ShapeDtypeStruct((rows, LANES), F32)
    s_new, o = pl.pallas_call(
        _rw_step_kernel,
        grid=(rows // tr,),
        in_specs=[spec] * 7 + [_const_spec(ones_half.shape)],
        out_specs=(spec, spec),
        out_shape=(big, big),
        compiler_params=_cparams(("arbitrary",)),
        name="rw_step",
    )(s0.reshape(rows, LANES), kexp(kk), kexp(w), kexp(nb), kexp(k), kexp(r), vexp, ones_half)
    o = o.reshape(rows, 2, RW_HEAD)[:, :, 0].reshape(batch, D_B)
    return o, s_new.reshape(batch, RW_HEADS, RW_HEAD, RW_HEAD)


def _mix_kernel(*refs, hp, moe):
    (x_ref, ya_ref, o_ref, bon_ref, g_ref, gm_ref, wg_ref, gng_ref, gnb_ref, ones_ref,
     wbo_ref, wout_ref, gf_ref) = refs[:13]
    if moe:
        wr_ref, x1_o, h2_o, route_o = refs[13:]
    else:
        x1_o, h2_o = refs[13:]
    x = x_ref[...]
    h = _rms(x, gm_ref[...])
    gates = _mm(h, wg_ref[...], hp)
    o = o_ref[...]
    ones = ones_ref[...]
    inv_n = 1.0 / RW_HEAD
    mean = _segsum(o, ones, hp) * inv_n
    dlt = o - mean
    var = _segsum(dlt * dlt, ones, hp) * inv_n
    on = dlt * lax.rsqrt(var + GN_EPS) * gng_ref[...] + gnb_ref[...]
    yb = _mm((on + bon_ref[...]) * g_ref[...], wbo_ref[...], hp)
    m = _sigmoid(gates[:, :D_MODEL]) * ya_ref[...] + _sigmoid(gates[:, D_MODEL:]) * yb
    x1 = x + _mm(m, wout_ref[...], hp)
    h2 = _rms(x1, gf_ref[...])
    x1_o[...] = x1
    h2_o[...] = h2.astype(h2_o.dtype)
    if moe:
        logits = _mm(h2, wr_ref[...], hp)
        lane = lax.broadcasted_iota(jnp.int32, logits.shape, 1).astype(F32)
        neg = jnp.float32(-jnp.inf)
        lg = jnp.where(lane < N_EXPERTS, logits, neg)
        m1 = jnp.max(lg, axis=1, keepdims=True)
        i1 = jnp.min(jnp.where(lg == m1, lane, float(LANES)), axis=1, keepdims=True)
        lg2 = jnp.where(lane == i1, neg, lg)
        m2 = jnp.max(lg2, axis=1, keepdims=True)
        i2 = jnp.min(jnp.where(lg2 == m2, lane, float(LANES)), axis=1, keepdims=True)
        e = jnp.exp(m2 - m1)
        g1 = 1.0 / (1.0 + e)
        g2 = e / (1.0 + e)
        route_o[...] = jnp.where(lane == 0.0, i1, jnp.where(lane == 1.0, i2,
                                 jnp.where(lane == 2.0, g1, jnp.where(lane == 3.0, g2, 0.0))))


def _mix(x, ya, o, bon, g, P, tm, hp, moe):
    rows = x.shape[0]
    assert rows % tm == 0
    row_spec = lambda w: pl.BlockSpec((tm, w), lambda i: (i, 0))
    consts = [P["g_mix"], P["w_gate"], P["rw_gn_g"], P["rw_gn_b"], P["ones_head"],
              P["rw_w_bo"], P["w_out"], P["g_ffn"]]
    if moe:
        consts.append(P["w_router"])
    h2_dtype = F32 if hp else BF16
    out_specs = [row_spec(D_MODEL), row_spec(D_MODEL)]
    out_shape = [jax.ShapeDtypeStruct((rows, D_MODEL), F32), jax.ShapeDtypeStruct((rows, D_MODEL), h2_dtype)]
    if moe:
        out_specs.append(row_spec(LANES))
        out_shape.append(jax.ShapeDtypeStruct((rows, LANES), F32))
    return pl.pallas_call(
        functools.partial(_mix_kernel, hp=hp, moe=moe),
        grid=(rows // tm,),
        in_specs=[row_spec(D_MODEL), row_spec(D_MODEL), row_spec(D_B), row_spec(D_B), row_spec(D_B)]
                 + [_const_spec(a.shape) for a in consts],
        out_specs=tuple(out_specs),
        out_shape=tuple(out_shape),
        compiler_params=_cparams(("arbitrary",)),
        name=("mix_moe" if moe else "mix") + ("_hp" if hp else ""),
    )(x, ya, o, bon, g, *consts)


def _ffn_kernel(h_ref, x1_ref, w1_ref, w3_ref, w2_ref, o_ref, acc, *, hp):
    f = pl.program_id(1)

    @pl.when(f == 0)
    def _():
        acc[...] = jnp.zeros_like(acc)

    h = h_ref[...]
    a = _mm(h, w1_ref[...], hp)
    b = _mm(h, w3_ref[...], hp)
    acc[...] += _mm(a * _sigmoid(a) * b, w2_ref[...], hp)

    @pl.when(f == pl.num_programs(1) - 1)
    def _():
        o_ref[...] = x1_ref[...] + acc[...]


def _ffn(h2, x1, w1, w3, w2, tm, tf, hp):
    rows = h2.shape[0]
    assert rows % tm == 0 and D_FF % tf == 0
    return pl.pallas_call(
        functools.partial(_ffn_kernel, hp=hp),
        grid=(rows // tm, D_FF // tf),
        in_specs=[pl.BlockSpec((tm, D_MODEL), lambda i, f: (i, 0)),
                  pl.BlockSpec((tm, D_MODEL), lambda i, f: (i, 0)),
                  pl.BlockSpec((D_MODEL, tf), lambda i, f: (0, f)),
                  pl.BlockSpec((D_MODEL, tf), lambda i, f: (0, f)),
                  pl.BlockSpec((tf, D_MODEL), lambda i, f: (f, 0))],
        out_specs=pl.BlockSpec((tm, D_MODEL), lambda i, f: (i, 0)),
        out_shape=jax.ShapeDtypeStruct((rows, D_MODEL), F32),
        scratch_shapes=[pltpu.VMEM((tm, D_MODEL), F32)],
        compiler_params=_cparams(("arbitrary", "arbitrary")),
        name="ffn_hp" if hp else "ffn",
    )(h2, x1, w1, w3, w2)


MOE_TM = 512
MOE_TF = 896


def _moe_kernel(te_ref, nv_ref, x_ref, w1_ref, w3_ref, w2_ref, o_ref, acc):
    i = pl.program_id(0)
    f = pl.program_id(1)

    @pl.when(f == 0)
    def _():
        acc[...] = jnp.zeros_like(acc)

    @pl.when(i < nv_ref[0])
    def _():
        x = x_ref[...]
        a = jnp.dot(x, w1_ref[...], preferred_element_type=F32)
        b = jnp.dot(x, w3_ref[...], preferred_element_type=F32)
        acc[...] += jnp.dot((a * _sigmoid(a) * b).astype(BF16), w2_ref[...], preferred_element_type=F32)

    @pl.when(f == pl.num_programs(1) - 1)
    def _():
        o_ref[...] = acc[...]


def _moe_experts(xs, tile_expert, n_valid, w1, w3, w2):
    rows = xs.shape[0]
    ntiles = rows // MOE_TM
    nf = D_EXPERT // MOE_TF
    assert ntiles * MOE_TM == rows and nf * MOE_TF == D_EXPERT

    def fblk(i, f, nv):
        return jnp.where(i < nv[0], f, nf - 1)

    grid_spec = pltpu.PrefetchScalarGridSpec(
        num_scalar_prefetch=2,
        grid=(ntiles, nf),
        in_specs=[pl.BlockSpec((MOE_TM, D_MODEL), lambda i, f, te, nv: (i, 0)),
                  pl.BlockSpec((None, D_MODEL, MOE_TF), lambda i, f, te, nv: (te[i], 0, fblk(i, f, nv))),
                  pl.BlockSpec((None, D_MODEL, MOE_TF), lambda i, f, te, nv: (te[i], 0, fblk(i, f, nv))),
                  pl.BlockSpec((None, MOE_TF, D_MODEL), lambda i, f, te, nv: (te[i], fblk(i, f, nv), 0))],
        out_specs=pl.BlockSpec((MOE_TM, D_MODEL), lambda i, f, te, nv: (i, 0)),
        scratch_shapes=[pltpu.VMEM((MOE_TM, D_MODEL), F32)])
    return pl.pallas_call(
        _moe_kernel,
        grid_spec=grid_spec,
        out_shape=jax.ShapeDtypeStruct((rows, D_MODEL), F32),
        compiler_params=_cparams(("arbitrary", "arbitrary")),
        name="moe_experts",
    )(tile_expert, n_valid, xs, w1, w3, w2)


def _moe_plan(expert_idx):
    n_pairs = expert_idx.shape[0] * 2
    flat_e = expert_idx.reshape(n_pairs)
    onehot = (flat_e[:, None] == jnp.arange(N_EXPERTS, dtype=jnp.int32)[None, :]).astype(jnp.int32)
    csum = jnp.cumsum(onehot, axis=0)
    rank = jnp.sum(csum * onehot, axis=1) - 1
    counts = csum[-1]
    padded = ((counts + MOE_TM - 1) // MOE_TM) * MOE_TM
    pend = jnp.cumsum(padded)
    pstart = pend - padded
    dest = pstart[flat_e] + rank
    ntiles = (n_pairs + N_EXPERTS * (MOE_TM - 1)) // MOE_TM + 1
    rows = ntiles * MOE_TM
    src_tok = jnp.zeros((rows,), jnp.int32).at[dest].set(jnp.arange(n_pairs, dtype=jnp.int32) // 2)
    tile_start = jnp.arange(ntiles, dtype=jnp.int32) * MOE_TM
    n_valid = (pend[-1] // MOE_TM).astype(jnp.int32)
    te = jnp.sum((pend[None, :] <= tile_start[:, None]).astype(jnp.int32), axis=1)
    te = jnp.minimum(te, N_EXPERTS - 1)
    last_e = te[jnp.maximum(n_valid - 1, 0)]
    te = jnp.where(jnp.arange(ntiles) < n_valid, te, last_e).astype(jnp.int32)
    return src_tok, dest.reshape(-1, 2), te, n_valid.reshape(1)


def _combine_kernel(x1_ref, y1_ref, y2_ref, route_ref, gf_ref, o_ref):
    route = route_ref[...]
    x2 = x1_ref[...] + route[:, 2:3] * y1_ref[...] + route[:, 3:4] * y2_ref[...]
    o_ref[...] = _rms(x2, gf_ref[...])


def _combine_norm(x1, y1, y2, route, g_final, tm):
    rows = x1.shape[0]
    assert rows % tm == 0
    row_spec = lambda w: pl.BlockSpec((tm, w), lambda i: (i, 0))
    return pl.pallas_call(
        _combine_kernel,
        grid=(rows // tm,),
        in_specs=[row_spec(D_MODEL), row_spec(D_MODEL), row_spec(D_MODEL), row_spec(LANES),
                  _const_spec(g_final.shape)],
        out_specs=row_spec(D_MODEL),
        out_shape=jax.ShapeDtypeStruct((rows, D_MODEL), F32),
        compiler_params=_cparams(("arbitrary",)),
        name="combine_norm",
    )(x1, y1, y2, route, g_final)


def _layer_params(l, W, hp, batches):
    wdt = F32 if hp else BF16
    row = lambda a: a.reshape(1, -1).astype(F32)
    w_in = W["w_in"][l]
    P = {
        "g_mix": row(W["norm_mix"][l]),
        "g_ffn": row(W["norm_ffn"][l]),
        "w_u": w_in[:, :D_A].astype(wdt),
        "w_rw": w_in[:, D_A:D_A + P_RW].astype(wdt),
        "w_gate": w_in[:, D_A + P_RW:].astype(wdt),
        "s5_d": row(W["s5_d"][l]),
        "s5_glu1": W["s5_glu1"][l].astype(wdt),
        "s5_glu2": W["s5_glu2"][l].astype(wdt),
        "rw_mu": row(W["rw_mu"][l]),
        "rw_w0": row(W["rw_w0"][l]),
        "rw_a0": row(W["rw_a0"][l]),
        "rw_g_up": W["rw_g_up"][l].astype(wdt),
        "rw_k_k": row(W["rw_k_k"][l]),
        "rw_k_a": row(W["rw_k_a"][l]),
        "rw_r_k": row(W["rw_r_k"][l]),
        "rw_gn_g": row(W["rw_gn_g"][l]),
        "rw_gn_b": row(W["rw_gn_b"][l]),
        "rw_w_bo": W["rw_w_bo"][l].astype(wdt),
        "w_out": W["w_out"][l].astype(wdt),
    }
    wl = jnp.zeros((LORA_W + LORA_A, 2 * D_B), F32)
    wl = wl.at[:LORA_W, :D_B].set(W["rw_w_up"][l]).at[LORA_W:, D_B:].set(W["rw_a_up"][l])
    P["rw_wl"] = wl.astype(wdt)
    P["ones_head"] = jnp.kron(jnp.eye(RW_HEADS, dtype=F32), jnp.ones((RW_HEAD, RW_HEAD), F32)).astype(BF16)

    ab_re, ab_im, bb_re, bb_im = _s5_discretize(W["s5_lam_re"][l], W["s5_lam_im"][l], W["s5_log_dt"][l],
                                                W["s5_b_re"][l], W["s5_b_im"][l])
    P["s5_ar"] = {b: jnp.broadcast_to(ab_re.reshape(1, S5_HALF), (b, S5_HALF)) for b in batches}
    P["s5_ai"] = {b: jnp.broadcast_to(ab_im.reshape(1, S5_HALF), (b, S5_HALF)) for b in batches}
    gpt = S5_GROUPS // S5_TILES
    eye = jnp.eye(gpt, dtype=F32)
    bb = jnp.stack([bb_re, bb_im]).reshape(2, S5_GROUP, S5_TILES, gpt, S5_STATE)
    bb = jnp.transpose(bb, (2, 1, 0, 3, 4))
    bbar = bb[:, None] * eye[None, :, None, None, :, None]
    P["s5_bbar"] = bbar.reshape(S5_TILES, LANES, 2 * gpt * S5_STATE).astype(wdt)
    cs = jnp.stack([W["s5_c_re"][l], -W["s5_c_im"][l]]).reshape(2, S5_TILES, gpt, S5_GROUP, S5_STATE)
    cs = jnp.transpose(cs, (1, 0, 2, 4, 3))
    cm = cs[:, :, :, :, None, :] * eye[None, None, :, None, :, None]
    cm = cm.reshape(S5_TILES, 2, gpt * S5_STATE, LANES).astype(wdt)
    P["s5_cre"] = cm[:, 0]
    P["s5_cim"] = cm[:, 1]
    return P


def _mixer(l, x, batch, T, s5r0, s5i0, rw_s0, rw_sh0, P, hp, moe, ones_half):
    tchunk = _pick_tile(T, 48, 1)
    ya, xr, xi = _s5_branch(x, batch, tchunk, P, s5r0, s5i0, hp)
    r, w, k, v, kk, nb, g, bon, sh = _rw_prep(x, batch, tchunk, P, rw_sh0, hp)
    if T == 1:
        o, s_fin = _rw_recurrence_step((r, w, k, v, kk, nb), batch, rw_s0, ones_half)
    else:
        o, s_fin = _rw_recurrence_long((r, w, k, v, kk, nb), batch, T, rw_s0, tchunk)
    rows = x.shape[0]
    tm = _pick_tile(rows, 344)
    outs = _mix(x, ya, o, bon, g, P, tm, hp, moe)
    return outs, (xr, xi, s_fin, sh)


def kernel(x_prompt, x_sample, state_s5_re, state_s5_im, state_rwkv, state_shift, meta_tokens, norm_mix, w_in, s5_lam_re, s5_lam_im, s5_log_dt, s5_b_re, s5_b_im, s5_c_re, s5_c_im, s5_d, s5_glu1, s5_glu2, rw_mu, rw_w0, rw_w_up, rw_a0, rw_a_up, rw_g_up, rw_k_k, rw_k_a, rw_r_k, rw_gn_g, rw_gn_b, rw_w_bo, w_out, norm_ffn, ffn_w1, ffn_w3, ffn_w2, moe_router, moe_w1, moe_w3, moe_w2, norm_final):
    W = dict(norm_mix=norm_mix, w_in=w_in, s5_lam_re=s5_lam_re, s5_lam_im=s5_lam_im,
             s5_log_dt=s5_log_dt, s5_b_re=s5_b_re, s5_b_im=s5_b_im, s5_c_re=s5_c_re,
             s5_c_im=s5_c_im, s5_d=s5_d, s5_glu1=s5_glu1, s5_glu2=s5_glu2, rw_mu=rw_mu,
             rw_w0=rw_w0, rw_w_up=rw_w_up, rw_a0=rw_a0, rw_a_up=rw_a_up, rw_g_up=rw_g_up,
             rw_k_k=rw_k_k, rw_k_a=rw_k_a, rw_r_k=rw_r_k, rw_gn_g=rw_gn_g, rw_gn_b=rw_gn_b,
             rw_w_bo=rw_w_bo, w_out=w_out, norm_ffn=norm_ffn)
    bp, seq, _ = x_prompt.shape
    bs = x_sample.shape[0]
    tp = N_META + seq
    assert x_sample.shape[1] == 1

    meta = jnp.broadcast_to(meta_tokens.astype(F32)[None], (bp, N_META, D_MODEL))
    xp = jnp.transpose(jnp.concatenate([meta, x_prompt], axis=1), (1, 0, 2)).reshape(tp * bp, D_MODEL)
    xs = x_sample.reshape(bs, D_MODEL)
    ones_half = jnp.kron(jnp.eye(2, dtype=F32), jnp.ones((RW_HEAD, RW_HEAD), F32)).astype(BF16)
    g_final = norm_final.reshape(1, D_MODEL).astype(F32)

    zero_s5 = jnp.zeros((bp, S5_HALF), F32)
    zero_rw = jnp.zeros((bp, RW_HEADS, RW_HEAD, RW_HEAD), F32)
    zero_sh = jnp.zeros((bp, P_RW), F32)

    p_states, s_states = [], []
    for l in range(DEPTH):
        moe = (l % 2 == 1)
        j = l // 2
        Pl = _layer_params(l, W, False, (bp,))
        Ph = _layer_params(l, W, True, (bs,))
        if moe:
            wr = jnp.zeros((D_MODEL, LANES), F32).at[:, :N_EXPERTS].set(moe_router[j])
            Pl["w_router"] = wr.astype(BF16)
            Ph["w_router"] = wr
        outs_p, st_p = _mixer(l, xp, bp, tp, zero_s5, zero_s5, zero_rw, zero_sh, Pl, False, moe, ones_half)
        outs_s, st_s = _mixer(l, xs, bs, 1, state_s5_re[l].reshape(bs, S5_HALF),
                              state_s5_im[l].reshape(bs, S5_HALF), state_rwkv[l], state_shift[l],
                              Ph, True, moe, ones_half)
        p_states.append(st_p)
        s_states.append(st_s)
        if not moe:
            xp = _ffn(outs_p[1], outs_p[0], ffn_w1[j].astype(BF16), ffn_w3[j].astype(BF16),
                      ffn_w2[j].astype(BF16), _pick_tile(tp * bp, 688), 1408, False)
            xs = _ffn(outs_s[1], outs_s[0], ffn_w1[j], ffn_w3[j], ffn_w2[j], bs, 256, True)
        else:
            x1 = jnp.concatenate([outs_p[0], outs_s[0]], axis=0)
            h2 = jnp.concatenate([outs_p[1], outs_s[1].astype(BF16)], axis=0)
            route = jnp.concatenate([outs_p[2], outs_s[2]], axis=0)
            src_tok, dest, te, n_valid = _moe_plan(route[:, :2].astype(jnp.int32))
            y_sorted = _moe_experts(jnp.take(h2, src_tok, axis=0), te, n_valid,
                                    moe_w1[j].astype(BF16), moe_w3[j].astype(BF16), moe_w2[j].astype(BF16))
            y1 = jnp.take(y_sorted, dest[:, 0], axis=0)
            y2 = jnp.take(y_sorted, dest[:, 1], axis=0)
            y = _combine_norm(x1, y1, y2, route, g_final, _pick_tile(x1.shape[0], 640))
            xp, xs = y[:tp * bp], y[tp * bp:]

    assert DEPTH % 2 == 0, "final norm is fused into the mixture-of-experts combine of the last layer"
    y_prompt = jnp.transpose(xp.reshape(tp, bp, D_MODEL)[N_META:], (1, 0, 2))
    y_sample = xs.reshape(bs, 1, D_MODEL)

    def stack(states, b):
        re = jnp.stack([s[0].reshape(b, S5_GROUPS, S5_STATE) for s in states])
        im = jnp.stack([s[1].reshape(b, S5_GROUPS, S5_STATE) for s in states])
        rw = jnp.stack([s[2] for s in states])
        sh = jnp.stack([s[3] for s in states])
        return re, im, rw, sh

    p_re, p_im, p_rw, p_sh = stack(p_states, bp)
    s_re, s_im, s_rw, s_sh = stack(s_states, bs)
    return (y_prompt, y_sample, p_re, p_im, p_rw, p_sh, s_re, s_im, s_rw, s_sh)
```

```python
import functools

import jax
import jax.numpy as jnp
from jax import lax
from jax.experimental import pallas as pl
from jax.experimental.pallas import tpu as pltpu

F32 = jnp.float32
BF16 = jnp.bfloat16
HIGHEST = lax.Precision.HIGHEST

D_MODEL = 1024
DEPTH = 2
N_META = 16
D_A = 512
S5_GROUP = 16
S5_GROUPS = 32
S5_STATE = 64
S5_TILES = 4
S5_HALF = S5_GROUPS * S5_STATE
D_B = 512
RW_HEAD = 64
RW_HEADS = 8
LORA_W = 64
LORA_A = 64
LORA_G = 128
GN_EPS = 64e-5
P_RW = 3 * D_B + LORA_W + LORA_A + LORA_G
D_FF = 2816
N_EXPERTS = 8
D_EXPERT = 3584
RMS_EPS = 1e-6
LANES = 128
SUBLANES = 8
VMEM_LIMIT_MB = 56
RW_TC = LANES
DB_TILES = D_B // LANES


def _pick_tile(n, pref, mult=SUBLANES):
    best = None
    for d in range(mult, min(n, pref) + 1, mult):
        if n % d == 0:
            best = d
    return best if best is not None else n


def _pick_s5_chunk(t_pad, pref=136):
    cands = [d for d in range(SUBLANES, min(t_pad, pref) + 1, SUBLANES) if t_pad % d == 0]
    odd = [d for d in cands if (d // SUBLANES) % 2 == 1]
    return max(odd) if odd else max(cands)


def _cparams(sem):
    return pltpu.CompilerParams(dimension_semantics=sem,
                                vmem_limit_bytes=VMEM_LIMIT_MB * 1024 * 1024)


def _const_spec(shape):
    nd = len(shape)
    return pl.BlockSpec(shape, lambda *_: (0,) * nd, pipeline_mode=pl.Buffered(1))


def _mm(a, b, hp):
    if hp:
        return jnp.dot(a.astype(F32), b, precision=HIGHEST, preferred_element_type=F32)
    return jnp.dot(a.astype(BF16), b, preferred_element_type=F32)


def _segsum(x, ones, hp):
    hi = x.astype(BF16)
    r1 = x - hi.astype(F32)
    mid = r1.astype(BF16)
    out = (jnp.dot(hi, ones, preferred_element_type=F32)
           + jnp.dot(mid, ones, preferred_element_type=F32))
    if hp:
        lo = (r1 - mid.astype(F32)).astype(BF16)
        out = out + jnp.dot(lo, ones, preferred_element_type=F32)
    return out


def _sigmoid(x):
    return 1.0 / (1.0 + jnp.exp(-x))


def _softplus(x):
    return jnp.maximum(x, 0.0) + jnp.log1p(jnp.exp(-jnp.abs(x)))


def _rms(x, g):
    return x * lax.rsqrt(jnp.mean(x * x, axis=-1, keepdims=True) + RMS_EPS) * g


def _s5_disc_kernel(lr_ref, li_ref, ldt_ref, btr_ref, bti_ref, abr_o, abi_o, bbr_o, bbi_o):
    lr = lr_ref[...]
    li = li_ref[...]
    dt = jnp.exp(ldt_ref[...])
    mag = jnp.exp(lr * dt)
    ab_re = mag * jnp.cos(li * dt)
    ab_im = mag * jnp.sin(li * dt)
    den = lr * lr + li * li
    q_re = ((ab_re - 1.0) * lr + ab_im * li) / den
    q_im = (ab_im * lr - (ab_re - 1.0) * li) / den
    abr_o[...] = ab_re
    abi_o[...] = ab_im
    for c in range(S5_GROUP):
        b_re = btr_ref[c]
        b_im = bti_ref[c]
        bbr_o[c] = q_re * b_re - q_im * b_im
        bbi_o[c] = q_re * b_im + q_im * b_re


def _s5_discretize(lam_re, lam_im, log_dt, b_re, b_im):
    gp = jax.ShapeDtypeStruct((S5_GROUPS, S5_STATE), F32)
    cgp = jax.ShapeDtypeStruct((S5_GROUP, S5_GROUPS, S5_STATE), F32)
    return pl.pallas_call(_s5_disc_kernel, out_shape=(gp, gp, cgp, cgp), name="s5_disc")(
        lam_re, lam_im, log_dt[:, None],
        jnp.transpose(b_re, (2, 0, 1)), jnp.transpose(b_im, (2, 0, 1)))


def _s5_kernel(x_ref, gm_ref, wu_ref, bbar_ref, ar_ref, ai_ref, x0r_ref, x0i_ref,
               cre_ref, cim_ref, d_ref, g1_ref, g2_ref,
               ya_ref, xr_out, xi_out, br_scr, bi_scr, sr_scr, si_scr, *rest,
               batch, steps, t_start, hp, reorder):
    c = pl.program_id(0)

    @pl.when(c == 0)
    def _():
        sr_scr[...] = x0r_ref[...]
        si_scr[...] = x0i_ref[...]

    rows = batch * steps
    x = x_ref[...]
    if reorder:
        bm_scr, tm_scr = rest
        x = x.reshape(rows, D_MODEL)
    h = _rms(x, gm_ref[...])
    u = _mm(h, wu_ref[...], hp)

    def to_time_major(t, carry):
        for s in range(S5_TILES):
            tm_scr[s, pl.ds(pl.multiple_of(t * batch, batch), batch), :] = bm_scr[s, pl.ds(t, batch, stride=steps), :]
        return carry

    def to_batch_major(t, carry):
        for s in range(S5_TILES):
            bm_scr[s, pl.ds(t, batch, stride=steps), :] = tm_scr[s, pl.ds(pl.multiple_of(t * batch, batch), batch), :]
        return carry

    if reorder:
        for s in range(S5_TILES):
            bm_scr[s] = u[:, s * LANES:(s + 1) * LANES]
        lax.fori_loop(0, steps, to_time_major, 0)
        u_tm = jnp.concatenate([tm_scr[s] for s in range(S5_TILES)], axis=1)
    else:
        u_tm = u
    half = S5_HALF // S5_TILES
    for j in range(S5_TILES):
        bbj = _mm(u_tm[:, j * LANES:(j + 1) * LANES], bbar_ref[j], hp)
        br_scr[:, j * half:(j + 1) * half] = bbj[:, :half]
        bi_scr[:, j * half:(j + 1) * half] = bbj[:, half:]

    def step(t, carry):
        xr, xi = carry
        rws = pl.ds(pl.multiple_of(t * batch, batch), batch)
        ar = ar_ref[...]
        ai = ai_ref[...]
        nxr = ar * xr - ai * xi + br_scr[rws, :]
        nxi = ar * xi + ai * xr + bi_scr[rws, :]
        br_scr[rws, :] = nxr
        bi_scr[rws, :] = nxi
        return nxr, nxi

    carry = (sr_scr[...], si_scr[...])
    if steps == 1:
        carry = step(0, carry)
    else:
        lo = jnp.clip(t_start - c * steps, 0, steps)
        carry = lax.fori_loop(lo, steps, step, carry)
    sr_scr[...] = carry[0]
    si_scr[...] = carry[1]
    xr_out[...] = carry[0]
    xi_out[...] = carry[1]

    ys = []
    for j in range(S5_TILES):
        ys.append(_mm(br_scr[:, j * half:(j + 1) * half], cre_ref[j], hp)
                  + _mm(bi_scr[:, j * half:(j + 1) * half], cim_ref[j], hp))
    if reorder:
        for s in range(S5_TILES):
            tm_scr[s] = ys[s]
        lax.fori_loop(0, steps, to_batch_major, 0)
        y = jnp.concatenate([bm_scr[s] for s in range(S5_TILES)], axis=1)
    else:
        y = jnp.concatenate(ys, axis=1)
    y = jax.nn.gelu(y + d_ref[...] * u)
    out = _mm(y, g1_ref[...], hp) * _sigmoid(_mm(y, g2_ref[...], hp))
    ya_ref[...] = out.reshape(ya_ref.shape)


def _s5_branch(x, batch, t_len, t_start, P, x0r, x0i, hp):
    seq = x.ndim == 3
    steps = _pick_s5_chunk(t_len) if seq else 1
    nchunks = t_len // steps
    rows = batch * steps
    if seq:
        x_spec = pl.BlockSpec((batch, steps, D_MODEL), lambda c: (0, c, 0))
        scratch_extra = [pltpu.VMEM((S5_TILES, rows, LANES), F32), pltpu.VMEM((S5_TILES, rows, LANES), F32)]
    else:
        x_spec = pl.BlockSpec((batch, D_MODEL), lambda c: (0, 0))
        scratch_extra = []
    state_spec = pl.BlockSpec((batch, S5_HALF), lambda c: (0, 0))
    consts = [P["g_mix"], P["w_u"], P["s5_bbar"], P["s5_ar"][batch], P["s5_ai"][batch], x0r, x0i,
              P["s5_cre"], P["s5_cim"], P["s5_d"], P["s5_glu1"], P["s5_glu2"]]
    return pl.pallas_call(
        functools.partial(_s5_kernel, batch=batch, steps=steps, t_start=t_start, hp=hp, reorder=seq),
        grid=(nchunks,),
        in_specs=[x_spec] + [_const_spec(a.shape) for a in consts],
        out_specs=(x_spec, state_spec, state_spec),
        out_shape=(jax.ShapeDtypeStruct(x.shape, F32),
                   jax.ShapeDtypeStruct((batch, S5_HALF), F32),
                   jax.ShapeDtypeStruct((batch, S5_HALF), F32)),
        scratch_shapes=[pltpu.VMEM((rows, S5_HALF), F32), pltpu.VMEM((rows, S5_HALF), F32),
                        pltpu.VMEM((batch, S5_HALF), F32), pltpu.VMEM((batch, S5_HALF), F32)] + scratch_extra,
        compiler_params=_cparams(("arbitrary",)),
        name="s5_branch_hp" if hp else "s5_branch",
    )(x, *consts)


N_KVEC = 5
N_VEC = N_KVEC + 1


def _rw_prep_kernel(x_ref, gm_ref, wrw_ref, sh0_ref, mu_ref, wl_ref, w0_ref, a0_ref, gup_ref,
                    kk_ref, ka_ref, rk_ref, ones_ref,
                    vec_o, g_o, bon_o, sh_o, carry_scr, *, seq, t_start, hp):
    h = _rms(x_ref[...], gm_ref[...])
    p = _mm(h, wrw_ref[...], hp)
    rows = p.shape[0]
    if seq:
        c = pl.program_id(1)

        @pl.when(c == 0)
        def _():
            carry_scr[...] = jnp.zeros_like(carry_scr)

        row = lax.broadcasted_iota(jnp.int32, p.shape, 0)
        prev = jnp.where(row == 0, carry_scr[0:1, :], pltpu.roll(p, 1, 0))
        prev = jnp.where(row + c * rows == t_start, sh0_ref[...], prev)
        carry_scr[0:1, :] = p[rows - 1:rows, :]

        @pl.when(c == pl.num_programs(1) - 1)
        def _():
            sh_o[...] = p[rows - 1:rows, :]
    else:
        prev = sh0_ref[...]
        sh_o[...] = p
    z = p + (prev - p) * mu_ref[...]
    r = z[:, :D_B]
    k = z[:, D_B:2 * D_B]
    v = z[:, 2 * D_B:3 * D_B]
    zwa = z[:, 3 * D_B:3 * D_B + LORA_W + LORA_A]
    zg = z[:, 3 * D_B + LORA_W + LORA_A:]
    lane = lax.broadcasted_iota(jnp.int32, zwa.shape, 1)
    tw = jnp.where(lane < LORA_W, jnp.tanh(zwa), zwa)
    lw = _mm(tw, wl_ref[...], hp)
    w_log = -_softplus(-(w0_ref[...] + lw[:, :D_B])) - 0.5
    decay = jnp.exp(-jnp.exp(w_log))
    a = _sigmoid(a0_ref[...] + lw[:, D_B:])
    g = _mm(_sigmoid(zg), gup_ref[...], hp)
    kk = k * kk_ref[...]
    n2 = _segsum(kk * kk, ones_ref[...], hp)
    kkn = kk * lax.rsqrt(jnp.maximum(n2, 1e-24))
    k2 = k * (1.0 + (a - 1.0) * ka_ref[...])
    rk = _segsum(r * k2 * rk_ref[...], ones_ref[...], hp)
    vec_o[0] = kkn
    vec_o[1] = decay
    vec_o[2] = -(kkn * a)
    vec_o[3] = k2
    vec_o[4] = r
    vec_o[5] = v
    g_o[...] = g
    bon_o[...] = rk * v


def _rw_prep(x, P, sh0, t_start, hp):
    seq = x.ndim == 3
    consts = [P["g_mix"], P["w_rw"]]
    consts2 = [P["rw_mu"], P["rw_wl"], P["rw_w0"], P["rw_a0"], P["rw_g_up"],
               P["rw_k_k"], P["rw_k_a"], P["rw_r_k"], P["ones_head"]]
    if seq:
        batch, t_len, _ = x.shape
        tc = _pick_tile(t_len, 544)
        grid = (batch, t_len // tc)
        row_spec = lambda w: pl.BlockSpec((None, tc, w), lambda b, c: (b, c, 0))
        vec_spec = pl.BlockSpec((N_VEC, None, tc, D_B), lambda b, c: (0, b, c, 0))
        sh_spec = pl.BlockSpec((None, 1, P_RW), lambda b, c: (b, 0, 0))
        sh0 = sh0.reshape(batch, 1, P_RW)
        vec_shape = (N_VEC, batch, t_len, D_B)
        g_shape = (batch, t_len, D_B)
        sh_shape = (batch, 1, P_RW)
        sem = ("arbitrary", "arbitrary")
    else:
        batch = x.shape[0]
        grid = (1,)
        row_spec = lambda w: pl.BlockSpec((batch, w), lambda i: (0, 0))
        vec_spec = pl.BlockSpec((N_VEC, batch, D_B), lambda i: (0, 0, 0))
        sh_spec = row_spec(P_RW)
        vec_shape = (N_VEC, batch, D_B)
        g_shape = (batch, D_B)
        sh_shape = (batch, P_RW)
        sem = ("arbitrary",)
    return pl.pallas_call(
        functools.partial(_rw_prep_kernel, seq=seq, t_start=t_start, hp=hp),
        grid=grid,
        in_specs=[row_spec(D_MODEL)] + [_const_spec(a.shape) for a in consts] + [sh_spec]
                 + [_const_spec(a.shape) for a in consts2],
        out_specs=(vec_spec, row_spec(D_B), row_spec(D_B), sh_spec),
        out_shape=(jax.ShapeDtypeStruct(vec_shape, F32), jax.ShapeDtypeStruct(g_shape, F32),
                   jax.ShapeDtypeStruct(g_shape, F32), jax.ShapeDtypeStruct(sh_shape, F32)),
        scratch_shapes=[pltpu.VMEM((SUBLANES, P_RW), F32)],
        compiler_params=_cparams(sem),
        name="rw_prep_hp" if hp else "rw_prep",
    )(x, *consts, sh0, *consts2)


RW_KH = RW_HEAD // 2
RW_VH = RW_HEAD // SUBLANES
RW_PAIRS = SUBLANES * RW_HEADS


def _pairs_to_rows(x_ref, q_scr, batch):
    for b in range(batch):
        for j in range(DB_TILES):
            q_scr[pl.ds((b * DB_TILES + j) * LANES, LANES), :] = x_ref[b, :, j * LANES:(j + 1) * LANES].T


def _rw_kin_kernel(x_ref, o_ref, q_scr, *, batch):
    _pairs_to_rows(x_ref, q_scr, batch)
    for kh in range(RW_KH):
        m0 = q_scr[pl.ds(2 * kh, RW_PAIRS, stride=RW_HEAD), :]
        m1 = q_scr[pl.ds(2 * kh + 1, RW_PAIRS, stride=RW_HEAD), :]
        o_ref[kh] = jnp.concatenate([m0, m1], axis=0).T


def _rw_vin_kernel(x_ref, o_ref, q_scr, *, batch):
    _pairs_to_rows(x_ref, q_scr, batch)
    for v in range(RW_HEAD):
        m = q_scr[pl.ds(v, RW_PAIRS, stride=RW_HEAD), :]
        o_ref[v // SUBLANES, pl.ds(v % SUBLANES, RW_TC, stride=SUBLANES), :] = jnp.concatenate([m, m], axis=0).T


def _rw_unlayout_kernel(o2_ref, o_ref, q_scr, *, batch):
    for v in range(RW_HEAD):
        zt = o2_ref[pl.ds(v * RW_TC, RW_TC), :].T
        q_scr[pl.ds(v, RW_PAIRS, stride=RW_HEAD), :] = zt[:RW_PAIRS]
    for b in range(batch):
        for j in range(DB_TILES):
            o_ref[b, :, j * LANES:(j + 1) * LANES] = q_scr[pl.ds((b * DB_TILES + j) * LANES, LANES), :].T


def _rw_scan_kernel(kin_ref, vin_ref, s0_ref, o_ref, sfin_ref, s_scr, *, t_start):
    c = pl.program_id(0)

    @pl.when(c == 0)
    def _():
        s_scr[...] = s0_ref[...]

    def krow(t, vec, kh):
        return jnp.broadcast_to(kin_ref[vec * RW_KH + kh, pl.ds(t, 1), :], (SUBLANES, LANES))

    def step(t, carry):
        acc = [None] * RW_VH
        for kh in range(RW_KH):
            kkb = krow(t, 0, kh)
            for vh in range(RW_VH):
                prod = s_scr[vh, kh] * kkb
                acc[vh] = prod if kh == 0 else acc[vh] + prod
        sa = [a + pltpu.roll(a, LANES // 2, 1) for a in acc]
        tv = pl.multiple_of(t * SUBLANES, SUBLANES)
        vv = [vin_ref[vh, pl.ds(tv, SUBLANES), :] for vh in range(RW_VH)]
        oacc = [None] * RW_VH
        for kh in range(RW_KH):
            wb = krow(t, 1, kh)
            nbb = krow(t, 2, kh)
            kb = krow(t, 3, kh)
            rb = krow(t, 4, kh)
            for vh in range(RW_VH):
                s = s_scr[vh, kh] * wb + sa[vh] * nbb + vv[vh] * kb
                s_scr[vh, kh] = s
                q = s * rb
                oacc[vh] = q if kh == 0 else oacc[vh] + q
        for vh in range(RW_VH):
            o_ref[pl.ds(vh * SUBLANES * RW_TC + t, SUBLANES, stride=RW_TC), :] = (
                oacc[vh] + pltpu.roll(oacc[vh], LANES // 2, 1))
        return carry

    lo = jnp.clip(t_start - c * RW_TC, 0, RW_TC)
    lax.fori_loop(lo, RW_TC, step, 0)
    sfin_ref[...] = s_scr[...]


def _rw_recurrence_long(vecs, s0, t_start):
    _, batch, t_len, _ = vecs.shape
    assert batch == SUBLANES and t_len % RW_TC == 0
    nchunks = t_len // RW_TC
    q_scr = pltpu.VMEM((RW_PAIRS * RW_HEAD, LANES), F32)
    kin = pl.pallas_call(
        functools.partial(_rw_kin_kernel, batch=batch),
        grid=(nchunks, N_KVEC),
        in_specs=[pl.BlockSpec((None, batch, RW_TC, D_B), lambda c, i: (i, 0, c, 0))],
        out_specs=pl.BlockSpec((None, RW_KH, RW_TC, LANES), lambda c, i: (c, i, 0, 0)),
        out_shape=jax.ShapeDtypeStruct((nchunks, N_KVEC * RW_KH, RW_TC, LANES), F32),
        scratch_shapes=[q_scr],
        compiler_params=_cparams(("arbitrary", "arbitrary")),
        name="rw_kin",
    )(vecs)
    vin = pl.pallas_call(
        functools.partial(_rw_vin_kernel, batch=batch),
        grid=(nchunks,),
        in_specs=[pl.BlockSpec((None, batch, RW_TC, D_B), lambda c: (N_KVEC, 0, c, 0))],
        out_specs=pl.BlockSpec((None, RW_VH, RW_TC * SUBLANES, LANES), lambda c: (c, 0, 0, 0)),
        out_shape=jax.ShapeDtypeStruct((nchunks, RW_VH, RW_TC * SUBLANES, LANES), F32),
        scratch_shapes=[q_scr],
        compiler_params=_cparams(("arbitrary",)),
        name="rw_vin",
    )(vecs)
    sshape = (RW_VH, RW_KH, SUBLANES, LANES)
    s = s0.reshape(batch, RW_HEADS, RW_VH, SUBLANES, RW_KH, 2)
    s = jnp.transpose(s, (2, 4, 3, 5, 0, 1)).reshape(sshape)
    state_spec = pl.BlockSpec(sshape, lambda c: (0, 0, 0, 0))
    o2, sfin = pl.pallas_call(
        functools.partial(_rw_scan_kernel, t_start=t_start),
        grid=(nchunks,),
        in_specs=[pl.BlockSpec((None, N_KVEC * RW_KH, RW_TC, LANES), lambda c: (c, 0, 0, 0)),
                  pl.BlockSpec((None, RW_VH, RW_TC * SUBLANES, LANES), lambda c: (c, 0, 0, 0)),
                  state_spec],
        out_specs=(pl.BlockSpec((None, RW_HEAD * RW_TC, LANES), lambda c: (c, 0, 0)), state_spec),
        out_shape=(jax.ShapeDtypeStruct((nchunks, RW_HEAD * RW_TC, LANES), F32),
                   jax.ShapeDtypeStruct(sshape, F32)),
        scratch_shapes=[pltpu.VMEM(sshape, F32)],
        compiler_params=_cparams(("arbitrary",)),
        name="rw_scan",
    )(kin, vin, s)
    o = pl.pallas_call(
        functools.partial(_rw_unlayout_kernel, batch=batch),
        grid=(nchunks,),
        in_specs=[pl.BlockSpec((None, RW_HEAD * RW_TC, LANES), lambda c: (c, 0, 0))],
        out_specs=pl.BlockSpec((batch, RW_TC, D_B), lambda c: (0, c, 0)),
        out_shape=jax.ShapeDtypeStruct((batch, t_len, D_B), F32),
        scratch_shapes=[q_scr],
        compiler_params=_cparams(("arbitrary",)),
        name="rw_unlayout",
    )(o2)
    sfin = sfin.reshape(RW_VH, RW_KH, SUBLANES, 2, batch, RW_HEADS)
    sfin = jnp.transpose(sfin, (4, 5, 0, 2, 1, 3)).reshape(batch, RW_HEADS, RW_HEAD, RW_HEAD)
    return o, sfin


def _rw_step_kernel(s_ref, kk_ref, w_ref, nb_ref, k_ref, r_ref, v_ref, ones_ref, s_o, o_o):
    s = s_ref[...]
    ones = ones_ref[...]
    sa = _segsum(s * kk_ref[...], ones, True)
    s = s * w_ref[...] + sa * nb_ref[...] + v_ref[...] * k_ref[...]
    s_o[...] = s
    o_o[...] = _segsum(s * r_ref[...], ones, True)


def _rw_recurrence_step(vecs, s0, ones_half):
    batch = vecs.shape[1]
    pairs = batch * RW_HEADS
    vrows = RW_HEAD // 2
    rows = pairs * vrows

    def kexp(a):
        a = a.reshape(pairs, 1, 1, RW_HEAD)
        return jnp.broadcast_to(a, (pairs, vrows, 2, RW_HEAD)).reshape(rows, LANES)

    vexp = jnp.broadcast_to(vecs[N_KVEC].reshape(rows, 2, 1), (rows, 2, RW_HEAD)).reshape(rows, LANES)
    tr = _pick_tile(rows, 2048)
    spec = pl.BlockSpec((tr, LANES), lambda i: (i, 0))
    big = jax.ShapeDtypeStruct((rows, LANES), F32)
    s_new, o = pl.pallas_call(
        _rw_step_kernel,
        grid=(rows // tr,),
        in_specs=[spec] * 7 + [_const_spec(ones_half.shape)],
        out_specs=(spec, spec),
        out_shape=(big, big),
        compiler_params=_cparams(("arbitrary",)),
        name="rw_step",
    )(s0.reshape(rows, LANES), *[kexp(vecs[i]) for i in range(N_KVEC)], vexp, ones_half)
    o = o.reshape(rows, 2, RW_HEAD)[:, :, 0].reshape(batch, D_B)
    return o, s_new.reshape(batch, RW_HEADS, RW_HEAD, RW_HEAD)


def _mix_kernel(*refs, hp, moe):
    (x_ref, ya_ref, o_ref, bon_ref, g_ref, gm_ref, wg_ref, gng_ref, gnb_ref, ones_ref,
     wbo_ref, wout_ref, gf_ref) = refs[:13]
    if moe:
        wr_ref, x1_o, h2_o, route_o = refs[13:]
    else:
        x1_o, h2_o = refs[13:]
    x = x_ref[...]
    h = _rms(x, gm_ref[...])
    gates = _mm(h, wg_ref[...], hp)
    o = o_ref[...]
    ones = ones_ref[...]
    inv_n = 1.0 / RW_HEAD
    mean = _segsum(o, ones, hp) * inv_n
    dlt = o - mean
    var = _segsum(dlt * dlt, ones, hp) * inv_n
    on = dlt * lax.rsqrt(var + GN_EPS) * gng_ref[...] + gnb_ref[...]
    yb = _mm((on + bon_ref[...]) * g_ref[...], wbo_ref[...], hp)
    m = _sigmoid(gates[:, :D_MODEL]) * ya_ref[...] + _sigmoid(gates[:, D_MODEL:]) * yb
    x1 = x + _mm(m, wout_ref[...], hp)
    h2 = _rms(x1, gf_ref[...])
    x1_o[...] = x1
    h2_o[...] = h2.astype(h2_o.dtype)
    if moe:
        logits = _mm(h2, wr_ref[...], hp)
        lane = lax.broadcasted_iota(jnp.int32, logits.shape, 1).astype(F32)
        neg = jnp.float32(-jnp.inf)
        lg = jnp.where(lane < N_EXPERTS, logits, neg)
        m1 = jnp.max(lg, axis=1, keepdims=True)
        i1 = jnp.min(jnp.where(lg == m1, lane, float(LANES)), axis=1, keepdims=True)
        lg2 = jnp.where(lane == i1, neg, lg)
        m2 = jnp.max(lg2, axis=1, keepdims=True)
        i2 = jnp.min(jnp.where(lg2 == m2, lane, float(LANES)), axis=1, keepdims=True)
        e = jnp.exp(m2 - m1)
        g1 = 1.0 / (1.0 + e)
        g2 = e / (1.0 + e)
        route_o[...] = jnp.where(lane == 0.0, i1, jnp.where(lane == 1.0, i2,
                                 jnp.where(lane == 2.0, g1, jnp.where(lane == 3.0, g2, 0.0))))


def _mix(x, ya, o, bon, g, P, hp, moe):
    rows = x.shape[0]
    tm = _pick_tile(rows, 344)
    row_spec = lambda w: pl.BlockSpec((tm, w), lambda i: (i, 0))
    consts = [P["g_mix"], P["w_gate"], P["rw_gn_g"], P["rw_gn_b"], P["ones_head"],
              P["rw_w_bo"], P["w_out"], P["g_ffn"]]
    if moe:
        consts.append(P["w_router"])
    h2_dtype = F32 if hp else BF16
    out_specs = [row_spec(D_MODEL), row_spec(D_MODEL)]
    out_shape = [jax.ShapeDtypeStruct((rows, D_MODEL), F32), jax.ShapeDtypeStruct((rows, D_MODEL), h2_dtype)]
    if moe:
        out_specs.append(row_spec(LANES))
        out_shape.append(jax.ShapeDtypeStruct((rows, LANES), F32))
    return pl.pallas_call(
        functools.partial(_mix_kernel, hp=hp, moe=moe),
        grid=(rows // tm,),
        in_specs=[row_spec(D_MODEL), row_spec(D_MODEL), row_spec(D_B), row_spec(D_B), row_spec(D_B)]
                 + [_const_spec(a.shape) for a in consts],
        out_specs=tuple(out_specs),
        out_shape=tuple(out_shape),
        compiler_params=_cparams(("arbitrary",)),
        name=("mix_moe" if moe else "mix") + ("_hp" if hp else ""),
    )(x, ya, o, bon, g, *consts)


def _ffn_kernel(h_ref, x1_ref, w1_ref, w3_ref, w2_ref, o_ref, acc, *, hp):
    f = pl.program_id(1)

    @pl.when(f == 0)
    def _():
        acc[...] = jnp.zeros_like(acc)

    h = h_ref[...]
    a = _mm(h, w1_ref[...], hp)
    b = _mm(h, w3_ref[...], hp)
    acc[...] += _mm(a * _sigmoid(a) * b, w2_ref[...], hp)

    @pl.when(f == pl.num_programs(1) - 1)
    def _():
        o_ref[...] = x1_ref[...] + acc[...]


def _ffn(h2, x1, w1, w3, w2, tm, tf, hp):
    rows = h2.shape[0]
    assert rows % tm == 0 and D_FF % tf == 0
    return pl.pallas_call(
        functools.partial(_ffn_kernel, hp=hp),
        grid=(rows // tm, D_FF // tf),
        in_specs=[pl.BlockSpec((tm, D_MODEL), lambda i, f: (i, 0)),
                  pl.BlockSpec((tm, D_MODEL), lambda i, f: (i, 0)),
                  pl.BlockSpec((D_MODEL, tf), lambda i, f: (0, f)),
                  pl.BlockSpec((D_MODEL, tf), lambda i, f: (0, f)),
                  pl.BlockSpec((tf, D_MODEL), lambda i, f: (f, 0))],
        out_specs=pl.BlockSpec((tm, D_MODEL), lambda i, f: (i, 0)),
        out_shape=jax.ShapeDtypeStruct((rows, D_MODEL), F32),
        scratch_shapes=[pltpu.VMEM((tm, D_MODEL), F32)],
        compiler_params=_cparams(("arbitrary", "arbitrary")),
        name="ffn_hp" if hp else "ffn",
    )(h2, x1, w1, w3, w2)


MOE_TM = 512
MOE_TF = 896


def _moe_kernel(te_ref, nv_ref, x_ref, w1_ref, w3_ref, w2_ref, o_ref, acc):
    i = pl.program_id(0)
    f = pl.program_id(1)

    @pl.when(f == 0)
    def _():
        acc[...] = jnp.zeros_like(acc)

    @pl.when(i < nv_ref[0])
    def _():
        x = x_ref[...]
        a = jnp.dot(x, w1_ref[...], preferred_element_type=F32)
        b = jnp.dot(x, w3_ref[...], preferred_element_type=F32)
        acc[...] += jnp.dot((a * _sigmoid(a) * b).astype(BF16), w2_ref[...], preferred_element_type=F32)

    @pl.when(f == pl.num_programs(1) - 1)
    def _():
        o_ref[...] = acc[...]


def _moe_experts(xs, tile_expert, n_valid, w1, w3, w2):
    rows = xs.shape[0]
    ntiles = rows // MOE_TM
    nf = D_EXPERT // MOE_TF
    assert ntiles * MOE_TM == rows and nf * MOE_TF == D_EXPERT

    def fblk(i, f, nv):
        return jnp.where(i < nv[0], f, nf - 1)

    grid_spec = pltpu.PrefetchScalarGridSpec(
        num_scalar_prefetch=2,
        grid=(ntiles, nf),
        in_specs=[pl.BlockSpec((MOE_TM, D_MODEL), lambda i, f, te, nv: (i, 0)),
                  pl.BlockSpec((None, D_MODEL, MOE_TF), lambda i, f, te, nv: (te[i], 0, fblk(i, f, nv))),
                  pl.BlockSpec((None, D_MODEL, MOE_TF), lambda i, f, te, nv: (te[i], 0, fblk(i, f, nv))),
                  pl.BlockSpec((None, MOE_TF, D_MODEL), lambda i, f, te, nv: (te[i], fblk(i, f, nv), 0))],
        out_specs=pl.BlockSpec((MOE_TM, D_MODEL), lambda i, f, te, nv: (i, 0)),
        scratch_shapes=[pltpu.VMEM((MOE_TM, D_MODEL), F32)])
    return pl.pallas_call(
        _moe_kernel,
        grid_spec=grid_spec,
        out_shape=jax.ShapeDtypeStruct((rows, D_MODEL), F32),
        compiler_params=_cparams(("arbitrary", "arbitrary")),
        name="moe_experts",
    )(tile_expert, n_valid, xs, w1, w3, w2)


def _moe_plan(expert_idx, src_row):
    n_pairs = expert_idx.shape[0] * 2
    flat_e = expert_idx.reshape(n_pairs)
    onehot = (flat_e[:, None] == jnp.arange(N_EXPERTS, dtype=jnp.int32)[None, :]).astype(jnp.int32)
    csum = jnp.cumsum(onehot, axis=0)
    rank = jnp.sum(csum * onehot, axis=1) - 1
    counts = csum[-1]
    padded = ((counts + MOE_TM - 1) // MOE_TM) * MOE_TM
    pend = jnp.cumsum(padded)
    pstart = pend - padded
    dest = pstart[flat_e] + rank
    ntiles = (n_pairs + N_EXPERTS * (MOE_TM - 1)) // MOE_TM + 1
    rows = ntiles * MOE_TM
    src = jnp.zeros((rows,), jnp.int32).at[dest].set(jnp.repeat(src_row, 2))
    tile_start = jnp.arange(ntiles, dtype=jnp.int32) * MOE_TM
    n_valid = (pend[-1] // MOE_TM).astype(jnp.int32)
    te = jnp.sum((pend[None, :] <= tile_start[:, None]).astype(jnp.int32), axis=1)
    te = jnp.minimum(te, N_EXPERTS - 1)
    last_e = te[jnp.maximum(n_valid - 1, 0)]
    te = jnp.where(jnp.arange(ntiles) < n_valid, te, last_e).astype(jnp.int32)
    return src, dest.reshape(-1, 2), te, n_valid.reshape(1)


def _combine_kernel(x1_ref, y1_ref, y2_ref, route_ref, gf_ref, o_ref):
    route = route_ref[...]
    x2 = x1_ref[...] + route[:, 2:3] * y1_ref[...] + route[:, 3:4] * y2_ref[...]
    o_ref[...] = _rms(x2, gf_ref[...])


def _combine_norm(x1, x1_spec, flat_blk0, grid, out_shape, out_spec, y1, y2, route, g_final, tm):
    nd = len(grid)

    def flat_map(*idx):
        lin = idx[0]
        for k in range(1, nd):
            lin = lin * grid[k] + idx[k]
        return (flat_blk0 + lin, 0)

    return pl.pallas_call(
        _combine_kernel,
        grid=grid,
        in_specs=[x1_spec, pl.BlockSpec((tm, D_MODEL), flat_map), pl.BlockSpec((tm, D_MODEL), flat_map),
                  pl.BlockSpec((tm, LANES), flat_map), _const_spec(g_final.shape)],
        out_specs=out_spec,
        out_shape=out_shape,
        compiler_params=_cparams(("arbitrary",) * nd),
        name="combine_norm",
    )(x1, y1, y2, route, g_final)


def _layer_params(l, W, hp, batches):
    wdt = F32 if hp else BF16
    row = lambda a: a.reshape(1, -1).astype(F32)
    w_in = W["w_in"][l]
    P = {
        "g_mix": row(W["norm_mix"][l]),
        "g_ffn": row(W["norm_ffn"][l]),
        "w_u": w_in[:, :D_A].astype(wdt),
        "w_rw": w_in[:, D_A:D_A + P_RW].astype(wdt),
        "w_gate": w_in[:, D_A + P_RW:].astype(wdt),
        "s5_d": row(W["s5_d"][l]),
        "s5_glu1": W["s5_glu1"][l].astype(wdt),
        "s5_glu2": W["s5_glu2"][l].astype(wdt),
        "rw_mu": row(W["rw_mu"][l]),
        "rw_w0": row(W["rw_w0"][l]),
        "rw_a0": row(W["rw_a0"][l]),
        "rw_g_up": W["rw_g_up"][l].astype(wdt),
        "rw_k_k": row(W["rw_k_k"][l]),
        "rw_k_a": row(W["rw_k_a"][l]),
        "rw_r_k": row(W["rw_r_k"][l]),
        "rw_gn_g": row(W["rw_gn_g"][l]),
        "rw_gn_b": row(W["rw_gn_b"][l]),
        "rw_w_bo": W["rw_w_bo"][l].astype(wdt),
        "w_out": W["w_out"][l].astype(wdt),
    }
    wl = jnp.zeros((LORA_W + LORA_A, 2 * D_B), F32)
    wl = wl.at[:LORA_W, :D_B].set(W["rw_w_up"][l]).at[LORA_W:, D_B:].set(W["rw_a_up"][l])
    P["rw_wl"] = wl.astype(wdt)
    P["ones_head"] = jnp.kron(jnp.eye(RW_HEADS, dtype=F32), jnp.ones((RW_HEAD, RW_HEAD), F32)).astype(BF16)

    ab_re, ab_im, bb_re, bb_im = _s5_discretize(W["s5_lam_re"][l], W["s5_lam_im"][l], W["s5_log_dt"][l],
                                                W["s5_b_re"][l], W["s5_b_im"][l])
    P["s5_ar"] = {b: jnp.broadcast_to(ab_re.reshape(1, S5_HALF), (b, S5_HALF)) for b in batches}
    P["s5_ai"] = {b: jnp.broadcast_to(ab_im.reshape(1, S5_HALF), (b, S5_HALF)) for b in batches}
    gpt = S5_GROUPS // S5_TILES
    eye = jnp.eye(gpt, dtype=F32)
    bb = jnp.stack([bb_re, bb_im]).reshape(2, S5_GROUP, S5_TILES, gpt, S5_STATE)
    bb = jnp.transpose(bb, (2, 1, 0, 3, 4))
    bbar = bb[:, None] * eye[None, :, None, None, :, None]
    P["s5_bbar"] = bbar.reshape(S5_TILES, LANES, 2 * gpt * S5_STATE).astype(wdt)
    cs = jnp.stack([W["s5_c_re"][l], -W["s5_c_im"][l]]).reshape(2, S5_TILES, gpt, S5_GROUP, S5_STATE)
    cs = jnp.transpose(cs, (1, 0, 2, 4, 3))
    cm = cs[:, :, :, :, None, :] * eye[None, None, :, None, :, None]
    cm = cm.reshape(S5_TILES, 2, gpt * S5_STATE, LANES).astype(wdt)
    P["s5_cre"] = cm[:, 0]
    P["s5_cim"] = cm[:, 1]
    return P


def _mixer_seq(x, t_start, s5r0, s5i0, rw_s0, rw_sh0, P, moe):
    batch, t_len, _ = x.shape
    ya, xr, xi = _s5_branch(x, batch, t_len, t_start, P, s5r0, s5i0, False)
    vecs, g, bon, sh = _rw_prep(x, P, rw_sh0, t_start, False)
    o, s_fin = _rw_recurrence_long(vecs, rw_s0, t_start)
    flat = lambda a: a.reshape(batch * t_len, a.shape[-1])
    outs = _mix(flat(x), flat(ya), flat(o), flat(bon), flat(g), P, False, moe)
    return outs, (xr, xi, s_fin, sh.reshape(batch, P_RW))


def _mixer_step(x, s5r0, s5i0, rw_s0, rw_sh0, P, moe, ones_half):
    batch = x.shape[0]
    ya, xr, xi = _s5_branch(x, batch, 1, 0, P, s5r0, s5i0, True)
    vecs, g, bon, sh = _rw_prep(x, P, rw_sh0, 0, True)
    o, s_fin = _rw_recurrence_step(vecs, rw_s0, ones_half)
    outs = _mix(x, ya, o, bon, g, P, True, moe)
    return outs, (xr, xi, s_fin, sh)


def kernel(x_prompt, x_sample, state_s5_re, state_s5_im, state_rwkv, state_shift, meta_tokens, norm_mix, w_in, s5_lam_re, s5_lam_im, s5_log_dt, s5_b_re, s5_b_im, s5_c_re, s5_c_im, s5_d, s5_glu1, s5_glu2, rw_mu, rw_w0, rw_w_up, rw_a0, rw_a_up, rw_g_up, rw_k_k, rw_k_a, rw_r_k, rw_gn_g, rw_gn_b, rw_w_bo, w_out, norm_ffn, ffn_w1, ffn_w3, ffn_w2, moe_router, moe_w1, moe_w3, moe_w2, norm_final):
    W = dict(norm_mix=norm_mix, w_in=w_in, s5_lam_re=s5_lam_re, s5_lam_im=s5_lam_im,
             s5_log_dt=s5_log_dt, s5_b_re=s5_b_re, s5_b_im=s5_b_im, s5_c_re=s5_c_re,
             s5_c_im=s5_c_im, s5_d=s5_d, s5_glu1=s5_glu1, s5_glu2=s5_glu2, rw_mu=rw_mu,
             rw_w0=rw_w0, rw_w_up=rw_w_up, rw_a0=rw_a0, rw_a_up=rw_a_up, rw_g_up=rw_g_up,
             rw_k_k=rw_k_k, rw_k_a=rw_k_a, rw_r_k=rw_r_k, rw_gn_g=rw_gn_g, rw_gn_b=rw_gn_b,
             rw_w_bo=rw_w_bo, w_out=w_out, norm_ffn=norm_ffn)
    bp, seq, _ = x_prompt.shape
    bs = x_sample.shape[0]
    assert x_sample.shape[1] == 1 and seq % RW_TC == 0 and bp == SUBLANES
    tp = N_META + seq
    t_pad = -(-tp // RW_TC) * RW_TC
    t_start = t_pad - tp
    out0 = t_pad - seq

    meta = jnp.broadcast_to(meta_tokens.astype(F32)[None], (bp, N_META, D_MODEL))
    xp = jnp.concatenate([jnp.zeros((bp, t_start, D_MODEL), F32), meta, x_prompt], axis=1)
    xs = x_sample.reshape(bs, D_MODEL)
    ones_half = jnp.kron(jnp.eye(2, dtype=F32), jnp.ones((RW_HEAD, RW_HEAD), F32)).astype(BF16)
    g_final = norm_final.reshape(1, D_MODEL).astype(F32)

    zero_s5 = jnp.zeros((bp, S5_HALF), F32)
    zero_rw = jnp.zeros((bp, RW_HEADS, RW_HEAD, RW_HEAD), F32)
    zero_sh = jnp.zeros((bp, P_RW), F32)

    p_states, s_states = [], []
    for l in range(DEPTH):
        moe = (l % 2 == 1)
        j = l // 2
        Pl = _layer_params(l, W, False, (bp,))
        Ph = _layer_params(l, W, True, (bs,))
        if moe:
            wr = jnp.zeros((D_MODEL, LANES), F32).at[:, :N_EXPERTS].set(moe_router[j])
            Pl["w_router"] = wr.astype(BF16)
            Ph["w_router"] = wr
        outs_p, st_p = _mixer_seq(xp, t_start, zero_s5, zero_s5, zero_rw, zero_sh, Pl, moe)
        outs_s, st_s = _mixer_step(xs, state_s5_re[l].reshape(bs, S5_HALF), state_s5_im[l].reshape(bs, S5_HALF),
                                   state_rwkv[l], state_shift[l], Ph, moe, ones_half)
        p_states.append(st_p)
        s_states.append(st_s)
        if not moe:
            rows = bp * t_pad
            xp = _ffn(outs_p[1], outs_p[0], ffn_w1[j].astype(BF16), ffn_w3[j].astype(BF16),
                      ffn_w2[j].astype(BF16), _pick_tile(rows, 688), 1408, False).reshape(bp, t_pad, D_MODEL)
            xs = _ffn(outs_s[1], outs_s[0], ffn_w1[j], ffn_w3[j], ffn_w2[j], bs, 256, True)
        else:
            assert l == DEPTH - 1
            n_p = bp * seq
            h2 = jnp.concatenate([outs_p[1], outs_s[1].astype(BF16)], axis=0)
            route_p = outs_p[2].reshape(bp, t_pad, LANES)[:, out0:].reshape(n_p, LANES)
            route = jnp.concatenate([route_p, outs_s[2]], axis=0)
            tok = jnp.arange(n_p, dtype=jnp.int32)
            src_row = jnp.concatenate([(tok // seq) * t_pad + out0 + tok % seq,
                                       bp * t_pad + jnp.arange(bs, dtype=jnp.int32)])
            src, dest, te, n_valid = _moe_plan(route[:, :2].astype(jnp.int32), src_row)
            y_sorted = _moe_experts(jnp.take(h2, src, axis=0), te, n_valid,
                                    moe_w1[j].astype(BF16), moe_w3[j].astype(BF16), moe_w2[j].astype(BF16))
            y1 = jnp.take(y_sorted, dest[:, 0], axis=0)
            y2 = jnp.take(y_sorted, dest[:, 1], axis=0)
            tm = RW_TC
            nblk = seq // tm
            y_prompt = _combine_norm(
                outs_p[0].reshape(bp, t_pad, D_MODEL),
                pl.BlockSpec((None, tm, D_MODEL), lambda b, s: (b, s + out0 // tm, 0)), 0, (bp, nblk),
                jax.ShapeDtypeStruct((bp, seq, D_MODEL), F32),
                pl.BlockSpec((None, tm, D_MODEL), lambda b, s: (b, s, 0)), y1, y2, route, g_final, tm)
            assert n_p % bs == 0
            y_sample = _combine_norm(
                outs_s[0], pl.BlockSpec((bs, D_MODEL), lambda i: (0, 0)), n_p // bs, (1,),
                jax.ShapeDtypeStruct((bs, D_MODEL), F32),
                pl.BlockSpec((bs, D_MODEL), lambda i: (0, 0)), y1, y2, route, g_final, bs)

    y_sample = y_sample.reshape(bs, 1, D_MODEL)

    def stack(states, b):
        re = jnp.stack([s[0].reshape(b, S5_GROUPS, S5_STATE) for s in states])
        im = jnp.stack([s[1].reshape(b, S5_GROUPS, S5_STATE) for s in states])
        rw = jnp.stack([s[2] for s in states])
        sh = jnp.stack([s[3] for s in states])
        return re, im, rw, sh

    p_re, p_im, p_rw, p_sh = stack(p_states, bp)
    s_re, s_im, s_rw, s_sh = stack(s_states, bs)
    return (y_prompt, y_sample, p_re, p_im, p_rw, p_sh, s_re, s_im, s_rw, s_sh)
```

```python
import functools

import jax
import jax.numpy as jnp
from jax import lax
from jax.experimental import pallas as pl
from jax.experimental.pallas import tpu as pltpu

F32 = jnp.float32
BF16 = jnp.bfloat16
HIGHEST = lax.Precision.HIGHEST

D_MODEL = 1024
DEPTH = 2
N_META = 16
D_A = 512
S5_GROUP = 16
S5_GROUPS = 32
S5_STATE = 64
S5_TILES = 4
S5_HALF = S5_GROUPS * S5_STATE
D_B = 512
RW_HEAD = 64
RW_HEADS = 8
LORA_W = 64
LORA_A = 64
LORA_G = 128
GN_EPS = 64e-5
P_RW = 3 * D_B + LORA_W + LORA_A + LORA_G
D_FF = 2816
N_EXPERTS = 8
D_EXPERT = 3584
RMS_EPS = 1e-6
LANES = 128
SUBLANES = 8
VMEM_LIMIT_MB = 56
RW_TC = LANES
DB_TILES = D_B // LANES


def _pick_tile(n, pref, mult=SUBLANES):
    best = None
    for d in range(mult, min(n, pref) + 1, mult):
        if n % d == 0:
            best = d
    return best if best is not None else n


def _pick_s5_chunk(t_pad, pref=136):
    cands = [d for d in range(SUBLANES, min(t_pad, pref) + 1, SUBLANES) if t_pad % d == 0]
    odd = [d for d in cands if (d // SUBLANES) % 2 == 1]
    return max(odd) if odd else max(cands)


def _cparams(sem):
    return pltpu.CompilerParams(dimension_semantics=sem,
                                vmem_limit_bytes=VMEM_LIMIT_MB * 1024 * 1024)


def _const_spec(shape):
    nd = len(shape)
    return pl.BlockSpec(shape, lambda *_: (0,) * nd, pipeline_mode=pl.Buffered(1))


def _mm(a, b, hp):
    if hp:
        return jnp.dot(a.astype(F32), b, precision=HIGHEST, preferred_element_type=F32)
    return jnp.dot(a.astype(BF16), b, preferred_element_type=F32)


def _segsum(x, ones, hp):
    hi = x.astype(BF16)
    r1 = x - hi.astype(F32)
    mid = r1.astype(BF16)
    out = (jnp.dot(hi, ones, preferred_element_type=F32)
           + jnp.dot(mid, ones, preferred_element_type=F32))
    if hp:
        lo = (r1 - mid.astype(F32)).astype(BF16)
        out = out + jnp.dot(lo, ones, preferred_element_type=F32)
    return out


def _sigmoid(x):
    return 1.0 / (1.0 + jnp.exp(-x))


def _softplus(x):
    return jnp.maximum(x, 0.0) + jnp.log1p(jnp.exp(-jnp.abs(x)))


def _rms(x, g):
    return x * lax.rsqrt(jnp.mean(x * x, axis=-1, keepdims=True) + RMS_EPS) * g


def _s5_disc_kernel(lr_ref, li_ref, ldt_ref, btr_ref, bti_ref, abr_o, abi_o, bbr_o, bbi_o):
    lr = lr_ref[...]
    li = li_ref[...]
    dt = jnp.exp(ldt_ref[...])
    mag = jnp.exp(lr * dt)
    ab_re = mag * jnp.cos(li * dt)
    ab_im = mag * jnp.sin(li * dt)
    den = lr * lr + li * li
    q_re = ((ab_re - 1.0) * lr + ab_im * li) / den
    q_im = (ab_im * lr - (ab_re - 1.0) * li) / den
    abr_o[...] = ab_re
    abi_o[...] = ab_im
    for c in range(S5_GROUP):
        b_re = btr_ref[c]
        b_im = bti_ref[c]
        bbr_o[c] = q_re * b_re - q_im * b_im
        bbi_o[c] = q_re * b_im + q_im * b_re


def _s5_discretize(lam_re, lam_im, log_dt, b_re, b_im):
    gp = jax.ShapeDtypeStruct((S5_GROUPS, S5_STATE), F32)
    cgp = jax.ShapeDtypeStruct((S5_GROUP, S5_GROUPS, S5_STATE), F32)
    return pl.pallas_call(_s5_disc_kernel, out_shape=(gp, gp, cgp, cgp), name="s5_disc")(
        lam_re, lam_im, log_dt[:, None],
        jnp.transpose(b_re, (2, 0, 1)), jnp.transpose(b_im, (2, 0, 1)))


def _s5_kernel(x_ref, gm_ref, wu_ref, bbar_ref, ar_ref, ai_ref, x0r_ref, x0i_ref,
               cre_ref, cim_ref, d_ref, g1_ref, g2_ref,
               ya_ref, xr_out, xi_out, br_scr, bi_scr, sr_scr, si_scr, *rest,
               batch, steps, t_start, hp, reorder):
    c = pl.program_id(0)

    @pl.when(c == 0)
    def _():
        sr_scr[...] = x0r_ref[...]
        si_scr[...] = x0i_ref[...]

    rows = batch * steps
    x = x_ref[...]
    if reorder:
        bm_scr, tm_scr = rest
        x = x.reshape(rows, D_MODEL)
    h = _rms(x, gm_ref[...])
    u = _mm(h, wu_ref[...], hp)

    def to_time_major(t, carry):
        for s in range(S5_TILES):
            tm_scr[s, pl.ds(pl.multiple_of(t * batch, batch), batch), :] = bm_scr[s, pl.ds(t, batch, stride=steps), :]
        return carry

    def to_batch_major(t, carry):
        for s in range(S5_TILES):
            bm_scr[s, pl.ds(t, batch, stride=steps), :] = tm_scr[s, pl.ds(pl.multiple_of(t * batch, batch), batch), :]
        return carry

    if reorder:
        for s in range(S5_TILES):
            bm_scr[s] = u[:, s * LANES:(s + 1) * LANES]
        lax.fori_loop(0, steps, to_time_major, 0)
        u_tm = jnp.concatenate([tm_scr[s] for s in range(S5_TILES)], axis=1)
    else:
        u_tm = u
    half = S5_HALF // S5_TILES
    for j in range(S5_TILES):
        bbj = _mm(u_tm[:, j * LANES:(j + 1) * LANES], bbar_ref[j], hp)
        br_scr[:, j * half:(j + 1) * half] = bbj[:, :half]
        bi_scr[:, j * half:(j + 1) * half] = bbj[:, half:]

    def step(t, carry):
        xr, xi = carry
        rws = pl.ds(pl.multiple_of(t * batch, batch), batch)
        ar = ar_ref[...]
        ai = ai_ref[...]
        nxr = ar * xr - ai * xi + br_scr[rws, :]
        nxi = ar * xi + ai * xr + bi_scr[rws, :]
        br_scr[rws, :] = nxr
        bi_scr[rws, :] = nxi
        return nxr, nxi

    carry = (sr_scr[...], si_scr[...])
    if steps == 1:
        carry = step(0, carry)
    else:
        lo = jnp.clip(t_start - c * steps, 0, steps)
        carry = lax.fori_loop(lo, steps, step, carry)
    sr_scr[...] = carry[0]
    si_scr[...] = carry[1]
    xr_out[...] = carry[0]
    xi_out[...] = carry[1]

    ys = []
    for j in range(S5_TILES):
        ys.append(_mm(br_scr[:, j * half:(j + 1) * half], cre_ref[j], hp)
                  + _mm(bi_scr[:, j * half:(j + 1) * half], cim_ref[j], hp))
    if reorder:
        for s in range(S5_TILES):
            tm_scr[s] = ys[s]
        lax.fori_loop(0, steps, to_batch_major, 0)
        y = jnp.concatenate([bm_scr[s] for s in range(S5_TILES)], axis=1)
    else:
        y = jnp.concatenate(ys, axis=1)
    y = jax.nn.gelu(y + d_ref[...] * u)
    out = _mm(y, g1_ref[...], hp) * _sigmoid(_mm(y, g2_ref[...], hp))
    ya_ref[...] = out.reshape(ya_ref.shape)


def _s5_branch(x, batch, t_len, t_start, P, x0r, x0i, hp):
    seq = x.ndim == 3
    steps = _pick_s5_chunk(t_len) if seq else 1
    nchunks = t_len // steps
    rows = batch * steps
    if seq:
        x_spec = pl.BlockSpec((batch, steps, D_MODEL), lambda c: (0, c, 0))
        scratch_extra = [pltpu.VMEM((S5_TILES, rows, LANES), F32), pltpu.VMEM((S5_TILES, rows, LANES), F32)]
    else:
        x_spec = pl.BlockSpec((batch, D_MODEL), lambda c: (0, 0))
        scratch_extra = []
    state_spec = pl.BlockSpec((batch, S5_HALF), lambda c: (0, 0))
    consts = [P["g_mix"], P["w_u"], P["s5_bbar"], P["s5_ar"][batch], P["s5_ai"][batch], x0r, x0i,
              P["s5_cre"], P["s5_cim"], P["s5_d"], P["s5_glu1"], P["s5_glu2"]]
    return pl.pallas_call(
        functools.partial(_s5_kernel, batch=batch, steps=steps, t_start=t_start, hp=hp, reorder=seq),
        grid=(nchunks,),
        in_specs=[x_spec] + [_const_spec(a.shape) for a in consts],
        out_specs=(x_spec, state_spec, state_spec),
        out_shape=(jax.ShapeDtypeStruct(x.shape, F32),
                   jax.ShapeDtypeStruct((batch, S5_HALF), F32),
                   jax.ShapeDtypeStruct((batch, S5_HALF), F32)),
        scratch_shapes=[pltpu.VMEM((rows, S5_HALF), F32), pltpu.VMEM((rows, S5_HALF), F32),
                        pltpu.VMEM((batch, S5_HALF), F32), pltpu.VMEM((batch, S5_HALF), F32)] + scratch_extra,
        compiler_params=_cparams(("arbitrary",)),
        name="s5_branch_hp" if hp else "s5_branch",
    )(x, *consts)


N_KVEC = 5
N_VEC = N_KVEC + 1


def _rw_prep_kernel(x_ref, gm_ref, wrw_ref, sh0_ref, mu_ref, wl_ref, w0_ref, a0_ref, gup_ref,
                    kk_ref, ka_ref, rk_ref, ones_ref,
                    vec_o, g_o, bon_o, sh_o, carry_scr, *, seq, t_start, hp):
    h = _rms(x_ref[...], gm_ref[...])
    p = _mm(h, wrw_ref[...], hp)
    rows = p.shape[0]
    if seq:
        c = pl.program_id(1)

        @pl.when(c == 0)
        def _():
            carry_scr[...] = jnp.zeros_like(carry_scr)

        row = lax.broadcasted_iota(jnp.int32, p.shape, 0)
        prev = jnp.where(row == 0, carry_scr[0:1, :], pltpu.roll(p, 1, 0))
        prev = jnp.where(row + c * rows == t_start, sh0_ref[...], prev)
        carry_scr[0:1, :] = p[rows - 1:rows, :]

        @pl.when(c == pl.num_programs(1) - 1)
        def _():
            sh_o[...] = p[rows - 1:rows, :]
    else:
        prev = sh0_ref[...]
        sh_o[...] = p
    z = p + (prev - p) * mu_ref[...]
    r = z[:, :D_B]
    k = z[:, D_B:2 * D_B]
    v = z[:, 2 * D_B:3 * D_B]
    zwa = z[:, 3 * D_B:3 * D_B + LORA_W + LORA_A]
    zg = z[:, 3 * D_B + LORA_W + LORA_A:]
    lane = lax.broadcasted_iota(jnp.int32, zwa.shape, 1)
    tw = jnp.where(lane < LORA_W, jnp.tanh(zwa), zwa)
    lw = _mm(tw, wl_ref[...], hp)
    w_log = -_softplus(-(w0_ref[...] + lw[:, :D_B])) - 0.5
    decay = jnp.exp(-jnp.exp(w_log))
    a = _sigmoid(a0_ref[...] + lw[:, D_B:])
    g = _mm(_sigmoid(zg), gup_ref[...], hp)
    kk = k * kk_ref[...]
    n2 = _segsum(kk * kk, ones_ref[...], hp)
    kkn = kk * lax.rsqrt(jnp.maximum(n2, 1e-24))
    k2 = k * (1.0 + (a - 1.0) * ka_ref[...])
    rk = _segsum(r * k2 * rk_ref[...], ones_ref[...], hp)
    vec_o[0] = kkn
    vec_o[1] = decay
    vec_o[2] = -(kkn * a)
    vec_o[3] = k2
    vec_o[4] = r
    vec_o[5] = v
    g_o[...] = g
    bon_o[...] = rk * v


def _rw_prep(x, P, sh0, t_start, hp):
    seq = x.ndim == 3
    consts = [P["g_mix"], P["w_rw"]]
    consts2 = [P["rw_mu"], P["rw_wl"], P["rw_w0"], P["rw_a0"], P["rw_g_up"],
               P["rw_k_k"], P["rw_k_a"], P["rw_r_k"], P["ones_head"]]
    if seq:
        batch, t_len, _ = x.shape
        tc = _pick_tile(t_len, 544)
        grid = (batch, t_len // tc)
        row_spec = lambda w: pl.BlockSpec((None, tc, w), lambda b, c: (b, c, 0))
        vec_spec = pl.BlockSpec((N_VEC, None, tc, D_B), lambda b, c: (0, b, c, 0))
        sh_spec = pl.BlockSpec((None, 1, P_RW), lambda b, c: (b, 0, 0))
        sh0 = sh0.reshape(batch, 1, P_RW)
        vec_shape = (N_VEC, batch, t_len, D_B)
        g_shape = (batch, t_len, D_B)
        sh_shape = (batch, 1, P_RW)
        sem = ("arbitrary", "arbitrary")
    else:
        batch = x.shape[0]
        grid = (1,)
        row_spec = lambda w: pl.BlockSpec((batch, w), lambda i: (0, 0))
        vec_spec = pl.BlockSpec((N_VEC, batch, D_B), lambda i: (0, 0, 0))
        sh_spec = row_spec(P_RW)
        vec_shape = (N_VEC, batch, D_B)
        g_shape = (batch, D_B)
        sh_shape = (batch, P_RW)
        sem = ("arbitrary",)
    return pl.pallas_call(
        functools.partial(_rw_prep_kernel, seq=seq, t_start=t_start, hp=hp),
        grid=grid,
        in_specs=[row_spec(D_MODEL)] + [_const_spec(a.shape) for a in consts] + [sh_spec]
                 + [_const_spec(a.shape) for a in consts2],
        out_specs=(vec_spec, row_spec(D_B), row_spec(D_B), sh_spec),
        out_shape=(jax.ShapeDtypeStruct(vec_shape, F32), jax.ShapeDtypeStruct(g_shape, F32),
                   jax.ShapeDtypeStruct(g_shape, F32), jax.ShapeDtypeStruct(sh_shape, F32)),
        scratch_shapes=[pltpu.VMEM((SUBLANES, P_RW), F32)],
        compiler_params=_cparams(sem),
        name="rw_prep_hp" if hp else "rw_prep",
    )(x, *consts, sh0, *consts2)


RW_VH = RW_HEAD // (2 * SUBLANES)
RW_PAIRS = SUBLANES * RW_HEADS
RW_SC = RW_TC // 2


def _pairs_to_rows(x_ref, q_scr, batch):
    for b in range(batch):
        for j in range(DB_TILES):
            q_scr[pl.ds((b * DB_TILES + j) * LANES, LANES), :] = x_ref[b, :, j * LANES:(j + 1) * LANES].T


def _rw_kin_kernel(x_ref, o_ref, q_scr, *, batch):
    _pairs_to_rows(x_ref, q_scr, batch)
    for k in range(RW_HEAD):
        m = q_scr[pl.ds(k, RW_PAIRS, stride=RW_HEAD), :]
        o_ref[k] = jnp.concatenate([m, m], axis=0).T


def _rw_vin_kernel(x_ref, o_ref, q_scr, *, batch):
    _pairs_to_rows(x_ref, q_scr, batch)
    for vh in range(RW_VH):
        for vs in range(SUBLANES):
            v0 = (vh * SUBLANES + vs) * 2
            m0 = q_scr[pl.ds(v0, RW_PAIRS, stride=RW_HEAD), :]
            m1 = q_scr[pl.ds(v0 + 1, RW_PAIRS, stride=RW_HEAD), :]
            o_ref[vh, pl.ds(vs, RW_TC, stride=SUBLANES), :] = jnp.concatenate([m0, m1], axis=0).T


def _rw_unlayout_kernel(o3_ref, o_ref, q_scr, *, batch):
    for vh in range(RW_VH):
        for vs in range(SUBLANES):
            v0 = (vh * SUBLANES + vs) * 2
            zt = o3_ref[vh, pl.ds(vs, RW_TC, stride=SUBLANES), :].T
            q_scr[pl.ds(v0, RW_PAIRS, stride=RW_HEAD), :] = zt[:RW_PAIRS]
            q_scr[pl.ds(v0 + 1, RW_PAIRS, stride=RW_HEAD), :] = zt[RW_PAIRS:]
    for b in range(batch):
        for j in range(DB_TILES):
            o_ref[b, :, j * LANES:(j + 1) * LANES] = q_scr[pl.ds((b * DB_TILES + j) * LANES, LANES), :].T


def _rw_scan_kernel(kin_ref, vin_ref, s0_ref, o_ref, sfin_ref, s_scr, *, t_start):
    c = pl.program_id(0)

    @pl.when(c == 0)
    def _():
        s_scr[...] = s0_ref[...]

    def krow(t, vec, k):
        return jnp.broadcast_to(kin_ref[vec * RW_HEAD + k, pl.ds(t, 1), :], (SUBLANES, LANES))

    def step(t, carry):
        acc = [[None, None] for _ in range(RW_VH)]
        for k in range(RW_HEAD):
            kkb = krow(t, 0, k)
            for vh in range(RW_VH):
                prod = s_scr[vh, k] * kkb
                prev = acc[vh][k % 2]
                acc[vh][k % 2] = prod if prev is None else prev + prod
        sa = [a[0] + a[1] for a in acc]
        tv = pl.multiple_of(t * SUBLANES, SUBLANES)
        vv = [vin_ref[vh, pl.ds(tv, SUBLANES), :] for vh in range(RW_VH)]
        oacc = [[None, None] for _ in range(RW_VH)]
        for k in range(RW_HEAD):
            wb = krow(t, 1, k)
            nbb = krow(t, 2, k)
            kb = krow(t, 3, k)
            rb = krow(t, 4, k)
            for vh in range(RW_VH):
                s = s_scr[vh, k] * wb + sa[vh] * nbb + vv[vh] * kb
                s_scr[vh, k] = s
                q = s * rb
                prev = oacc[vh][k % 2]
                oacc[vh][k % 2] = q if prev is None else prev + q
        for vh in range(RW_VH):
            o_ref[vh, pl.ds(tv, SUBLANES), :] = oacc[vh][0] + oacc[vh][1]
        return carry

    lo = jnp.clip(t_start - c * RW_SC, 0, RW_SC)
    lax.fori_loop(lo, RW_SC, step, 0)
    sfin_ref[...] = s_scr[...]


def _rw_recurrence_long(vecs, s0, t_start):
    _, batch, t_len, _ = vecs.shape
    assert batch == SUBLANES and t_len % RW_TC == 0
    nchunks = t_len // RW_TC
    q_scr = pltpu.VMEM((RW_PAIRS * RW_HEAD, LANES), F32)
    kin = pl.pallas_call(
        functools.partial(_rw_kin_kernel, batch=batch),
        grid=(nchunks, N_KVEC),
        in_specs=[pl.BlockSpec((None, batch, RW_TC, D_B), lambda c, i: (i, 0, c, 0))],
        out_specs=pl.BlockSpec((None, RW_HEAD, RW_TC, LANES), lambda c, i: (c, i, 0, 0)),
        out_shape=jax.ShapeDtypeStruct((nchunks, N_KVEC * RW_HEAD, RW_TC, LANES), F32),
        scratch_shapes=[q_scr],
        compiler_params=_cparams(("arbitrary", "arbitrary")),
        name="rw_kin",
    )(vecs)
    vin = pl.pallas_call(
        functools.partial(_rw_vin_kernel, batch=batch),
        grid=(nchunks,),
        in_specs=[pl.BlockSpec((None, batch, RW_TC, D_B), lambda c: (N_KVEC, 0, c, 0))],
        out_specs=pl.BlockSpec((None, RW_VH, RW_TC * SUBLANES, LANES), lambda c: (c, 0, 0, 0)),
        out_shape=jax.ShapeDtypeStruct((nchunks, RW_VH, RW_TC * SUBLANES, LANES), F32),
        scratch_shapes=[q_scr],
        compiler_params=_cparams(("arbitrary",)),
        name="rw_vin",
    )(vecs)
    sshape = (RW_VH, RW_HEAD, SUBLANES, LANES)
    s = s0.reshape(batch, RW_HEADS, RW_VH, SUBLANES, 2, RW_HEAD)
    s = jnp.transpose(s, (2, 5, 3, 4, 0, 1)).reshape(sshape)
    state_spec = pl.BlockSpec(sshape, lambda c: (0, 0, 0, 0))
    halves = RW_TC // RW_SC
    vblk = (None, RW_VH, RW_SC * SUBLANES, LANES)
    vmap = lambda c: (c // halves, 0, c % halves, 0)
    o3, sfin = pl.pallas_call(
        functools.partial(_rw_scan_kernel, t_start=t_start),
        grid=(nchunks * halves,),
        in_specs=[pl.BlockSpec((None, N_KVEC * RW_HEAD, RW_SC, LANES), vmap),
                  pl.BlockSpec(vblk, vmap), state_spec],
        out_specs=(pl.BlockSpec(vblk, vmap), state_spec),
        out_shape=(jax.ShapeDtypeStruct((nchunks, RW_VH, RW_TC * SUBLANES, LANES), F32),
                   jax.ShapeDtypeStruct(sshape, F32)),
        scratch_shapes=[pltpu.VMEM(sshape, F32)],
        compiler_params=_cparams(("arbitrary",)),
        name="rw_scan",
    )(kin, vin, s)
    o = pl.pallas_call(
        functools.partial(_rw_unlayout_kernel, batch=batch),
        grid=(nchunks,),
        in_specs=[pl.BlockSpec((None, RW_VH, RW_TC * SUBLANES, LANES), lambda c: (c, 0, 0, 0))],
        out_specs=pl.BlockSpec((batch, RW_TC, D_B), lambda c: (0, c, 0)),
        out_shape=jax.ShapeDtypeStruct((batch, t_len, D_B), F32),
        scratch_shapes=[q_scr],
        compiler_params=_cparams(("arbitrary",)),
        name="rw_unlayout",
    )(o3)
    sfin = sfin.reshape(RW_VH, RW_HEAD, SUBLANES, 2, batch, RW_HEADS)
    sfin = jnp.transpose(sfin, (4, 5, 0, 2, 3, 1)).reshape(batch, RW_HEADS, RW_HEAD, RW_HEAD)
    return o, sfin


STEP_PAIRS = 64
STEP_VROWS = RW_HEAD // 2


def _split3(x):
    hi = x.astype(BF16)
    r1 = x - hi.astype(F32)
    mid = r1.astype(BF16)
    return hi, mid, (r1 - mid.astype(F32)).astype(BF16)


def _rw_step_kernel(s_ref, km_ref, v_ref, sel_ref, vmask_ref, vexp_ref, ones_ref, s_o, o_o):
    sel = sel_ref[...]

    def rows_of(x):
        return sum(jnp.dot(sel, p, preferred_element_type=F32) for p in _split3(x))

    kx = rows_of(km_ref[...])
    kk, w, nb, k, r = [kx[:, i * LANES:(i + 1) * LANES] for i in range(N_KVEC)]
    vx = _segsum(rows_of(v_ref[...]) * vmask_ref[...], vexp_ref[...], True)
    s = s_ref[...]
    ones = ones_ref[...]
    sa = _segsum(s * kk, ones, True)
    s = s * w + sa * nb + vx * k
    s_o[...] = s
    o_o[...] = _segsum(s * r, ones, True)


def _rw_recurrence_step(vecs, s0, ones_half):
    batch = vecs.shape[1]
    pairs = batch * RW_HEADS
    rows = pairs * STEP_VROWS
    tr = STEP_PAIRS * STEP_VROWS
    assert pairs % STEP_PAIRS == 0
    kp = vecs[:N_KVEC].reshape(N_KVEC, pairs, 1, RW_HEAD)
    km = jnp.transpose(jnp.broadcast_to(kp, (N_KVEC, pairs, 2, RW_HEAD)), (1, 0, 2, 3)).reshape(pairs, N_KVEC * LANES)
    vp = vecs[N_KVEC].reshape(pairs, RW_HEAD)
    ridx = jnp.arange(tr, dtype=jnp.int32)
    sel = (ridx[:, None] // STEP_VROWS == jnp.arange(STEP_PAIRS, dtype=jnp.int32)[None, :]).astype(BF16)
    vidx = jnp.arange(RW_HEAD, dtype=jnp.int32)
    vmask = (vidx[None, :] // 2 == ridx[:, None] % STEP_VROWS).astype(F32)
    vexp = (vidx[:, None] % 2 == jnp.arange(LANES, dtype=jnp.int32)[None, :] // RW_HEAD).astype(BF16)
    spec = pl.BlockSpec((tr, LANES), lambda i: (i, 0))
    big = jax.ShapeDtypeStruct((rows, LANES), F32)
    consts = [sel, vmask, vexp, ones_half]
    s_new, o = pl.pallas_call(
        _rw_step_kernel,
        grid=(pairs // STEP_PAIRS,),
        in_specs=[spec, pl.BlockSpec((STEP_PAIRS, N_KVEC * LANES), lambda i: (i, 0)),
                  pl.BlockSpec((STEP_PAIRS, RW_HEAD), lambda i: (i, 0))]
                 + [_const_spec(a.shape) for a in consts],
        out_specs=(spec, spec),
        out_shape=(big, big),
        compiler_params=_cparams(("arbitrary",)),
        name="rw_step",
    )(s0.reshape(rows, LANES), km, vp, *consts)
    o = o.reshape(rows, 2, RW_HEAD)[:, :, 0].reshape(batch, D_B)
    return o, s_new.reshape(batch, RW_HEADS, RW_HEAD, RW_HEAD)


def _mix_kernel(*refs, hp, moe):
    (x_ref, ya_ref, o_ref, bon_ref, g_ref, gm_ref, wg_ref, gng_ref, gnb_ref, ones_ref,
     wbo_ref, wout_ref, gf_ref) = refs[:13]
    if moe:
        wr_ref, x1_o, h2_o, route_o = refs[13:]
    else:
        x1_o, h2_o = refs[13:]
    x = x_ref[...]
    h = _rms(x, gm_ref[...])
    gates = _mm(h, wg_ref[...], hp)
    o = o_ref[...]
    ones = ones_ref[...]
    inv_n = 1.0 / RW_HEAD
    mean = _segsum(o, ones, hp) * inv_n
    dlt = o - mean
    var = _segsum(dlt * dlt, ones, hp) * inv_n
    on = dlt * lax.rsqrt(var + GN_EPS) * gng_ref[...] + gnb_ref[...]
    yb = _mm((on + bon_ref[...]) * g_ref[...], wbo_ref[...], hp)
    m = _sigmoid(gates[:, :D_MODEL]) * ya_ref[...] + _sigmoid(gates[:, D_MODEL:]) * yb
    x1 = x + _mm(m, wout_ref[...], hp)
    h2 = _rms(x1, gf_ref[...])
    x1_o[...] = x1
    h2_o[...] = h2.astype(h2_o.dtype)
    if moe:
        logits = _mm(h2, wr_ref[...], hp)
        lane = lax.broadcasted_iota(jnp.int32, logits.shape, 1).astype(F32)
        neg = jnp.float32(-jnp.inf)
        lg = jnp.where(lane < N_EXPERTS, logits, neg)
        m1 = jnp.max(lg, axis=1, keepdims=True)
        i1 = jnp.min(jnp.where(lg == m1, lane, float(LANES)), axis=1, keepdims=True)
        lg2 = jnp.where(lane == i1, neg, lg)
        m2 = jnp.max(lg2, axis=1, keepdims=True)
        i2 = jnp.min(jnp.where(lg2 == m2, lane, float(LANES)), axis=1, keepdims=True)
        e = jnp.exp(m2 - m1)
        g1 = 1.0 / (1.0 + e)
        g2 = e / (1.0 + e)
        route_o[...] = jnp.where(lane == 0.0, i1, jnp.where(lane == 1.0, i2,
                                 jnp.where(lane == 2.0, g1, jnp.where(lane == 3.0, g2, 0.0))))


def _mix(x, ya, o, bon, g, P, hp, moe):
    rows = x.shape[0]
    tm = _pick_tile(rows, 344)
    row_spec = lambda w: pl.BlockSpec((tm, w), lambda i: (i, 0))
    consts = [P["g_mix"], P["w_gate"], P["rw_gn_g"], P["rw_gn_b"], P["ones_head"],
              P["rw_w_bo"], P["w_out"], P["g_ffn"]]
    if moe:
        consts.append(P["w_router"])
    h2_dtype = F32 if hp else BF16
    out_specs = [row_spec(D_MODEL), row_spec(D_MODEL)]
    out_shape = [jax.ShapeDtypeStruct((rows, D_MODEL), F32), jax.ShapeDtypeStruct((rows, D_MODEL), h2_dtype)]
    if moe:
        out_specs.append(row_spec(LANES))
        out_shape.append(jax.ShapeDtypeStruct((rows, LANES), F32))
    return pl.pallas_call(
        functools.partial(_mix_kernel, hp=hp, moe=moe),
        grid=(rows // tm,),
        in_specs=[row_spec(D_MODEL), row_spec(D_MODEL), row_spec(D_B), row_spec(D_B), row_spec(D_B)]
                 + [_const_spec(a.shape) for a in consts],
        out_specs=tuple(out_specs),
        out_shape=tuple(out_shape),
        compiler_params=_cparams(("arbitrary",)),
        name=("mix_moe" if moe else "mix") + ("_hp" if hp else ""),
    )(x, ya, o, bon, g, *consts)


def _ffn_kernel(h_ref, x1_ref, w1_ref, w3_ref, w2_ref, o_ref, acc, *, hp):
    f = pl.program_id(1)

    @pl.when(f == 0)
    def _():
        acc[...] = jnp.zeros_like(acc)

    h = h_ref[...]
    a = _mm(h, w1_ref[...], hp)
    b = _mm(h, w3_ref[...], hp)
    acc[...] += _mm(a * _sigmoid(a) * b, w2_ref[...], hp)

    @pl.when(f == pl.num_programs(1) - 1)
    def _():
        o_ref[...] = x1_ref[...] + acc[...]


def _ffn(h2, x1, w1, w3, w2, tm, tf, hp):
    rows = h2.shape[0]
    assert rows % tm == 0 and D_FF % tf == 0
    return pl.pallas_call(
        functools.partial(_ffn_kernel, hp=hp),
        grid=(rows // tm, D_FF // tf),
        in_specs=[pl.BlockSpec((tm, D_MODEL), lambda i, f: (i, 0)),
                  pl.BlockSpec((tm, D_MODEL), lambda i, f: (i, 0)),
                  pl.BlockSpec((D_MODEL, tf), lambda i, f: (0, f)),
                  pl.BlockSpec((D_MODEL, tf), lambda i, f: (0, f)),
                  pl.BlockSpec((tf, D_MODEL), lambda i, f: (f, 0))],
        out_specs=pl.BlockSpec((tm, D_MODEL), lambda i, f: (i, 0)),
        out_shape=jax.ShapeDtypeStruct((rows, D_MODEL), F32),
        scratch_shapes=[pltpu.VMEM((tm, D_MODEL), F32)],
        compiler_params=_cparams(("arbitrary", "arbitrary")),
        name="ffn_hp" if hp else "ffn",
    )(h2, x1, w1, w3, w2)


MOE_TM = 512
MOE_TF = 896


def _moe_kernel(te_ref, nv_ref, x_ref, w1_ref, w3_ref, w2_ref, o_ref, acc):
    i = pl.program_id(0)
    f = pl.program_id(1)

    @pl.when(f == 0)
    def _():
        acc[...] = jnp.zeros_like(acc)

    @pl.when(i < nv_ref[0])
    def _():
        x = x_ref[...]
        a = jnp.dot(x, w1_ref[...], preferred_element_type=F32)
        b = jnp.dot(x, w3_ref[...], preferred_element_type=F32)
        acc[...] += jnp.dot((a * _sigmoid(a) * b).astype(BF16), w2_ref[...], preferred_element_type=F32)

    @pl.when(f == pl.num_programs(1) - 1)
    def _():
        o_ref[...] = acc[...]


def _moe_experts(xs, tile_expert, n_valid, w1, w3, w2):
    rows = xs.shape[0]
    ntiles = rows // MOE_TM
    nf = D_EXPERT // MOE_TF
    assert ntiles * MOE_TM == rows and nf * MOE_TF == D_EXPERT

    def fblk(i, f, nv):
        return jnp.where(i < nv[0], f, nf - 1)

    grid_spec = pltpu.PrefetchScalarGridSpec(
        num_scalar_prefetch=2,
        grid=(ntiles, nf),
        in_specs=[pl.BlockSpec((MOE_TM, D_MODEL), lambda i, f, te, nv: (i, 0)),
                  pl.BlockSpec((None, D_MODEL, MOE_TF), lambda i, f, te, nv: (te[i], 0, fblk(i, f, nv))),
                  pl.BlockSpec((None, D_MODEL, MOE_TF), lambda i, f, te, nv: (te[i], 0, fblk(i, f, nv))),
                  pl.BlockSpec((None, MOE_TF, D_MODEL), lambda i, f, te, nv: (te[i], fblk(i, f, nv), 0))],
        out_specs=pl.BlockSpec((MOE_TM, D_MODEL), lambda i, f, te, nv: (i, 0)),
        scratch_shapes=[pltpu.VMEM((MOE_TM, D_MODEL), F32)])
    return pl.pallas_call(
        _moe_kernel,
        grid_spec=grid_spec,
        out_shape=jax.ShapeDtypeStruct((rows, D_MODEL), F32),
        compiler_params=_cparams(("arbitrary", "arbitrary")),
        name="moe_experts",
    )(tile_expert, n_valid, xs, w1, w3, w2)


def _moe_plan(expert_idx, src_row):
    n_pairs = expert_idx.shape[0] * 2
    flat_e = expert_idx.reshape(n_pairs)
    onehot = (flat_e[:, None] == jnp.arange(N_EXPERTS, dtype=jnp.int32)[None, :]).astype(jnp.int32)
    csum = jnp.cumsum(onehot, axis=0)
    rank = jnp.sum(csum * onehot, axis=1) - 1
    counts = csum[-1]
    padded = ((counts + MOE_TM - 1) // MOE_TM) * MOE_TM
    pend = jnp.cumsum(padded)
    pstart = pend - padded
    dest = pstart[flat_e] + rank
    ntiles = (n_pairs + N_EXPERTS * (MOE_TM - 1)) // MOE_TM + 1
    rows = ntiles * MOE_TM
    src = jnp.zeros((rows,), jnp.int32).at[dest].set(jnp.repeat(src_row, 2))
    tile_start = jnp.arange(ntiles, dtype=jnp.int32) * MOE_TM
    n_valid = (pend[-1] // MOE_TM).astype(jnp.int32)
    te = jnp.sum((pend[None, :] <= tile_start[:, None]).astype(jnp.int32), axis=1)
    te = jnp.minimum(te, N_EXPERTS - 1)
    last_e = te[jnp.maximum(n_valid - 1, 0)]
    te = jnp.where(jnp.arange(ntiles) < n_valid, te, last_e).astype(jnp.int32)
    return src, dest.reshape(-1, 2), te, n_valid.reshape(1)


def _combine_kernel(x1_ref, y1_ref, y2_ref, route_ref, gf_ref, o_ref):
    route = route_ref[...]
    x2 = x1_ref[...] + route[:, 2:3] * y1_ref[...] + route[:, 3:4] * y2_ref[...]
    o_ref[...] = _rms(x2, gf_ref[...])


def _combine_norm(x1, x1_spec, flat_blk0, grid, out_shape, out_spec, y1, y2, route, g_final, tm):
    nd = len(grid)

    def flat_map(*idx):
        lin = idx[0]
        for k in range(1, nd):
            lin = lin * grid[k] + idx[k]
        return (flat_blk0 + lin, 0)

    return pl.pallas_call(
        _combine_kernel,
        grid=grid,
        in_specs=[x1_spec, pl.BlockSpec((tm, D_MODEL), flat_map), pl.BlockSpec((tm, D_MODEL), flat_map),
                  pl.BlockSpec((tm, LANES), flat_map), _const_spec(g_final.shape)],
        out_specs=out_spec,
        out_shape=out_shape,
        compiler_params=_cparams(("arbitrary",) * nd),
        name="combine_norm",
    )(x1, y1, y2, route, g_final)


def _layer_params(l, W, hp, batches):
    wdt = F32 if hp else BF16
    row = lambda a: a.reshape(1, -1).astype(F32)
    w_in = W["w_in"][l]
    P = {
        "g_mix": row(W["norm_mix"][l]),
        "g_ffn": row(W["norm_ffn"][l]),
        "w_u": w_in[:, :D_A].astype(wdt),
        "w_rw": w_in[:, D_A:D_A + P_RW].astype(wdt),
        "w_gate": w_in[:, D_A + P_RW:].astype(wdt),
        "s5_d": row(W["s5_d"][l]),
        "s5_glu1": W["s5_glu1"][l].astype(wdt),
        "s5_glu2": W["s5_glu2"][l].astype(wdt),
        "rw_mu": row(W["rw_mu"][l]),
        "rw_w0": row(W["rw_w0"][l]),
        "rw_a0": row(W["rw_a0"][l]),
        "rw_g_up": W["rw_g_up"][l].astype(wdt),
        "rw_k_k": row(W["rw_k_k"][l]),
        "rw_k_a": row(W["rw_k_a"][l]),
        "rw_r_k": row(W["rw_r_k"][l]),
        "rw_gn_g": row(W["rw_gn_g"][l]),
        "rw_gn_b": row(W["rw_gn_b"][l]),
        "rw_w_bo": W["rw_w_bo"][l].astype(wdt),
        "w_out": W["w_out"][l].astype(wdt),
    }
    wl = jnp.zeros((LORA_W + LORA_A, 2 * D_B), F32)
    wl = wl.at[:LORA_W, :D_B].set(W["rw_w_up"][l]).at[LORA_W:, D_B:].set(W["rw_a_up"][l])
    P["rw_wl"] = wl.astype(wdt)
    P["ones_head"] = jnp.kron(jnp.eye(RW_HEADS, dtype=F32), jnp.ones((RW_HEAD, RW_HEAD), F32)).astype(BF16)

    ab_re, ab_im, bb_re, bb_im = _s5_discretize(W["s5_lam_re"][l], W["s5_lam_im"][l], W["s5_log_dt"][l],
                                                W["s5_b_re"][l], W["s5_b_im"][l])
    P["s5_ar"] = {b: jnp.broadcast_to(ab_re.reshape(1, S5_HALF), (b, S5_HALF)) for b in batches}
    P["s5_ai"] = {b: jnp.broadcast_to(ab_im.reshape(1, S5_HALF), (b, S5_HALF)) for b in batches}
    gpt = S5_GROUPS // S5_TILES
    eye = jnp.eye(gpt, dtype=F32)
    bb = jnp.stack([bb_re, bb_im]).reshape(2, S5_GROUP, S5_TILES, gpt, S5_STATE)
    bb = jnp.transpose(bb, (2, 1, 0, 3, 4))
    bbar = bb[:, None] * eye[None, :, None, None, :, None]
    P["s5_bbar"] = bbar.reshape(S5_TILES, LANES, 2 * gpt * S5_STATE).astype(wdt)
    cs = jnp.stack([W["s5_c_re"][l], -W["s5_c_im"][l]]).reshape(2, S5_TILES, gpt, S5_GROUP, S5_STATE)
    cs = jnp.transpose(cs, (1, 0, 2, 4, 3))
    cm = cs[:, :, :, :, None, :] * eye[None, None, :, None, :, None]
    cm = cm.reshape(S5_TILES, 2, gpt * S5_STATE, LANES).astype(wdt)
    P["s5_cre"] = cm[:, 0]
    P["s5_cim"] = cm[:, 1]
    return P


def _mixer_seq(x, t_start, s5r0, s5i0, rw_s0, rw_sh0, P, moe):
    batch, t_len, _ = x.shape
    ya, xr, xi = _s5_branch(x, batch, t_len, t_start, P, s5r0, s5i0, False)
    vecs, g, bon, sh = _rw_prep(x, P, rw_sh0, t_start, False)
    o, s_fin = _rw_recurrence_long(vecs, rw_s0, t_start)
    flat = lambda a: a.reshape(batch * t_len, a.shape[-1])
    outs = _mix(flat(x), flat(ya), flat(o), flat(bon), flat(g), P, False, moe)
    return outs, (xr, xi, s_fin, sh.reshape(batch, P_RW))


def _mixer_step(x, s5r0, s5i0, rw_s0, rw_sh0, P, moe, ones_half):
    batch = x.shape[0]
    ya, xr, xi = _s5_branch(x, batch, 1, 0, P, s5r0, s5i0, True)
    vecs, g, bon, sh = _rw_prep(x, P, rw_sh0, 0, True)
    o, s_fin = _rw_recurrence_step(vecs, rw_s0, ones_half)
    outs = _mix(x, ya, o, bon, g, P, True, moe)
    return outs, (xr, xi, s_fin, sh)


def kernel(x_prompt, x_sample, state_s5_re, state_s5_im, state_rwkv, state_shift, meta_tokens, norm_mix, w_in, s5_lam_re, s5_lam_im, s5_log_dt, s5_b_re, s5_b_im, s5_c_re, s5_c_im, s5_d, s5_glu1, s5_glu2, rw_mu, rw_w0, rw_w_up, rw_a0, rw_a_up, rw_g_up, rw_k_k, rw_k_a, rw_r_k, rw_gn_g, rw_gn_b, rw_w_bo, w_out, norm_ffn, ffn_w1, ffn_w3, ffn_w2, moe_router, moe_w1, moe_w3, moe_w2, norm_final):
    W = dict(norm_mix=norm_mix, w_in=w_in, s5_lam_re=s5_lam_re, s5_lam_im=s5_lam_im,
             s5_log_dt=s5_log_dt, s5_b_re=s5_b_re, s5_b_im=s5_b_im, s5_c_re=s5_c_re,
             s5_c_im=s5_c_im, s5_d=s5_d, s5_glu1=s5_glu1, s5_glu2=s5_glu2, rw_mu=rw_mu,
             rw_w0=rw_w0, rw_w_up=rw_w_up, rw_a0=rw_a0, rw_a_up=rw_a_up, rw_g_up=rw_g_up,
             rw_k_k=rw_k_k, rw_k_a=rw_k_a, rw_r_k=rw_r_k, rw_gn_g=rw_gn_g, rw_gn_b=rw_gn_b,
             rw_w_bo=rw_w_bo, w_out=w_out, norm_ffn=norm_ffn)
    bp, seq, _ = x_prompt.shape
    bs = x_sample.shape[0]
    assert x_sample.shape[1] == 1 and seq % RW_TC == 0 and bp == SUBLANES
    tp = N_META + seq
    t_pad = -(-tp // RW_TC) * RW_TC
    t_start = t_pad - tp
    out0 = t_pad - seq

    meta = jnp.broadcast_to(meta_tokens.astype(F32)[None], (bp, N_META, D_MODEL))
    xp = jnp.concatenate([jnp.zeros((bp, t_start, D_MODEL), F32), meta, x_prompt], axis=1)
    xs = x_sample.reshape(bs, D_MODEL)
    ones_half = jnp.kron(jnp.eye(2, dtype=F32), jnp.ones((RW_HEAD, RW_HEAD), F32)).astype(BF16)
    g_final = norm_final.reshape(1, D_MODEL).astype(F32)

    zero_s5 = jnp.zeros((bp, S5_HALF), F32)
    zero_rw = jnp.zeros((bp, RW_HEADS, RW_HEAD, RW_HEAD), F32)
    zero_sh = jnp.zeros((bp, P_RW), F32)

    p_states, s_states = [], []
    for l in range(DEPTH):
        moe = (l % 2 == 1)
        j = l // 2
        Pl = _layer_params(l, W, False, (bp,))
        Ph = _layer_params(l, W, True, (bs,))
        if moe:
            wr = jnp.zeros((D_MODEL, LANES), F32).at[:, :N_EXPERTS].set(moe_router[j])
            Pl["w_router"] = wr.astype(BF16)
            Ph["w_router"] = wr
        outs_p, st_p = _mixer_seq(xp, t_start, zero_s5, zero_s5, zero_rw, zero_sh, Pl, moe)
        outs_s, st_s = _mixer_step(xs, state_s5_re[l].reshape(bs, S5_HALF), state_s5_im[l].reshape(bs, S5_HALF),
                                   state_rwkv[l], state_shift[l], Ph, moe, ones_half)
        p_states.append(st_p)
        s_states.append(st_s)
        if not moe:
            rows = bp * t_pad
            xp = _ffn(outs_p[1], outs_p[0], ffn_w1[j].astype(BF16), ffn_w3[j].astype(BF16),
                      ffn_w2[j].astype(BF16), _pick_tile(rows, 688), 1408, False).reshape(bp, t_pad, D_MODEL)
            xs = _ffn(outs_s[1], outs_s[0], ffn_w1[j], ffn_w3[j], ffn_w2[j], bs, 256, True)
        else:
            assert l == DEPTH - 1
            n_p = bp * seq
            h2 = jnp.concatenate([outs_p[1], outs_s[1].astype(BF16)], axis=0)
            route_p = outs_p[2].reshape(bp, t_pad, LANES)[:, out0:].reshape(n_p, LANES)
            route = jnp.concatenate([route_p, outs_s[2]], axis=0)
            tok = jnp.arange(n_p, dtype=jnp.int32)
            src_row = jnp.concatenate([(tok // seq) * t_pad + out0 + tok % seq,
                                       bp * t_pad + jnp.arange(bs, dtype=jnp.int32)])
            src, dest, te, n_valid = _moe_plan(route[:, :2].astype(jnp.int32), src_row)
            y_sorted = _moe_experts(jnp.take(h2, src, axis=0), te, n_valid,
                                    moe_w1[j].astype(BF16), moe_w3[j].astype(BF16), moe_w2[j].astype(BF16))
            y1 = jnp.take(y_sorted, dest[:, 0], axis=0)
            y2 = jnp.take(y_sorted, dest[:, 1], axis=0)
            tm = RW_TC
            nblk = seq // tm
            y_prompt = _combine_norm(
                outs_p[0].reshape(bp, t_pad, D_MODEL),
                pl.BlockSpec((None, tm, D_MODEL), lambda b, s: (b, s + out0 // tm, 0)), 0, (bp, nblk),
                jax.ShapeDtypeStruct((bp, seq, D_MODEL), F32),
                pl.BlockSpec((None, tm, D_MODEL), lambda b, s: (b, s, 0)), y1, y2, route, g_final, tm)
            assert n_p % bs == 0
            y_sample = _combine_norm(
                outs_s[0], pl.BlockSpec((bs, D_MODEL), lambda i: (0, 0)), n_p // bs, (1,),
                jax.ShapeDtypeStruct((bs, D_MODEL), F32),
                pl.BlockSpec((bs, D_MODEL), lambda i: (0, 0)), y1, y2, route, g_final, bs)

    y_sample = y_sample.reshape(bs, 1, D_MODEL)

    def stack(states, b):
        re = jnp.stack([s[0].reshape(b, S5_GROUPS, S5_STATE) for s in states])
        im = jnp.stack([s[1].reshape(b, S5_GROUPS, S5_STATE) for s in states])
        rw = jnp.stack([s[2] for s in states])
        sh = jnp.stack([s[3] for s in states])
        return re, im, rw, sh

    p_re, p_im, p_rw, p_sh = stack(p_states, bp)
    s_re, s_im, s_rw, s_sh = stack(s_states, bs)
    return (y_prompt, y_sample, p_re, p_im, p_rw, p_sh, s_re, s_im, s_rw, s_sh)
```

```python
import functools

import jax
import jax.numpy as jnp
from jax import lax
from jax.experimental import pallas as pl
from jax.experimental.pallas import tpu as pltpu

F32 = jnp.float32
BF16 = jnp.bfloat16
HIGHEST = lax.Precision.HIGHEST

D_MODEL = 1024
DEPTH = 2
N_META = 16
D_A = 512
S5_GROUP = 16
S5_GROUPS = 32
S5_STATE = 64
S5_TILES = 4
S5_HALF = S5_GROUPS * S5_STATE
D_B = 512
RW_HEAD = 64
RW_HEADS = 8
LORA_W = 64
LORA_A = 64
LORA_G = 128
GN_EPS = 64e-5
P_RW = 3 * D_B + LORA_W + LORA_A + LORA_G
D_FF = 2816
N_EXPERTS = 8
D_EXPERT = 3584
RMS_EPS = 1e-6
LANES = 128
SUBLANES = 8
VMEM_LIMIT_MB = 56
RW_TC = LANES
DB_TILES = D_B // LANES


def _pick_tile(n, pref, mult=SUBLANES):
    best = None
    for d in range(mult, min(n, pref) + 1, mult):
        if n % d == 0:
            best = d
    return best if best is not None else n


def _pick_s5_chunk(t_pad, pref=136):
    cands = [d for d in range(SUBLANES, min(t_pad, pref) + 1, SUBLANES) if t_pad % d == 0]
    odd = [d for d in cands if (d // SUBLANES) % 2 == 1]
    return max(odd) if odd else max(cands)


def _cparams(sem):
    return pltpu.CompilerParams(dimension_semantics=sem,
                                vmem_limit_bytes=VMEM_LIMIT_MB * 1024 * 1024)


def _const_spec(shape):
    nd = len(shape)
    return pl.BlockSpec(shape, lambda *_: (0,) * nd, pipeline_mode=pl.Buffered(1))


def _mm(a, b, hp):
    if hp:
        return jnp.dot(a.astype(F32), b, precision=HIGHEST, preferred_element_type=F32)
    return jnp.dot(a.astype(BF16), b, preferred_element_type=F32)


def _segsum(x, ones, hp):
    hi = x.astype(BF16)
    r1 = x - hi.astype(F32)
    mid = r1.astype(BF16)
    out = (jnp.dot(hi, ones, preferred_element_type=F32)
           + jnp.dot(mid, ones, preferred_element_type=F32))
    if hp:
        lo = (r1 - mid.astype(F32)).astype(BF16)
        out = out + jnp.dot(lo, ones, preferred_element_type=F32)
    return out


def _sigmoid(x):
    return 1.0 / (1.0 + jnp.exp(-x))


def _softplus(x):
    return jnp.maximum(x, 0.0) + jnp.log1p(jnp.exp(-jnp.abs(x)))


def _rms(x, g):
    return x * lax.rsqrt(jnp.mean(x * x, axis=-1, keepdims=True) + RMS_EPS) * g


def _s5_disc_kernel(lr_ref, li_ref, ldt_ref, btr_ref, bti_ref, abr_o, abi_o, bbr_o, bbi_o):
    lr = lr_ref[...]
    li = li_ref[...]
    dt = jnp.exp(ldt_ref[...])
    mag = jnp.exp(lr * dt)
    ab_re = mag * jnp.cos(li * dt)
    ab_im = mag * jnp.sin(li * dt)
    den = lr * lr + li * li
    q_re = ((ab_re - 1.0) * lr + ab_im * li) / den
    q_im = (ab_im * lr - (ab_re - 1.0) * li) / den
    abr_o[...] = ab_re
    abi_o[...] = ab_im
    for c in range(S5_GROUP):
        b_re = btr_ref[c]
        b_im = bti_ref[c]
        bbr_o[c] = q_re * b_re - q_im * b_im
        bbi_o[c] = q_re * b_im + q_im * b_re


def _s5_discretize(lam_re, lam_im, log_dt, b_re, b_im):
    gp = jax.ShapeDtypeStruct((S5_GROUPS, S5_STATE), F32)
    cgp = jax.ShapeDtypeStruct((S5_GROUP, S5_GROUPS, S5_STATE), F32)
    return pl.pallas_call(_s5_disc_kernel, out_shape=(gp, gp, cgp, cgp), name="s5_disc")(
        lam_re, lam_im, log_dt[:, None],
        jnp.transpose(b_re, (2, 0, 1)), jnp.transpose(b_im, (2, 0, 1)))


def _s5_kernel(x_ref, gm_ref, wu_ref, bbar_ref, ar_ref, ai_ref, x0r_ref, x0i_ref,
               cre_ref, cim_ref, d_ref, g1_ref, g2_ref,
               ya_ref, xr_out, xi_out, br_scr, bi_scr, sr_scr, si_scr, *rest,
               batch, steps, t_start, hp, reorder):
    c = pl.program_id(0)

    @pl.when(c == 0)
    def _():
        sr_scr[...] = x0r_ref[...]
        si_scr[...] = x0i_ref[...]

    rows = batch * steps
    x = x_ref[...]
    if reorder:
        bm_scr, tm_scr = rest
        x = x.reshape(rows, D_MODEL)
    h = _rms(x, gm_ref[...])
    u = _mm(h, wu_ref[...], hp)

    def to_time_major(t, carry):
        for s in range(S5_TILES):
            tm_scr[s, pl.ds(pl.multiple_of(t * batch, batch), batch), :] = bm_scr[s, pl.ds(t, batch, stride=steps), :]
        return carry

    def to_batch_major(t, carry):
        for s in range(S5_TILES):
            bm_scr[s, pl.ds(t, batch, stride=steps), :] = tm_scr[s, pl.ds(pl.multiple_of(t * batch, batch), batch), :]
        return carry

    if reorder:
        for s in range(S5_TILES):
            bm_scr[s] = u[:, s * LANES:(s + 1) * LANES]
        lax.fori_loop(0, steps, to_time_major, 0)
        u_tm = jnp.concatenate([tm_scr[s] for s in range(S5_TILES)], axis=1)
    else:
        u_tm = u
    half = S5_HALF // S5_TILES
    for j in range(S5_TILES):
        bbj = _mm(u_tm[:, j * LANES:(j + 1) * LANES], bbar_ref[j], hp)
        br_scr[:, j * half:(j + 1) * half] = bbj[:, :half]
        bi_scr[:, j * half:(j + 1) * half] = bbj[:, half:]

    def step(t, carry):
        xr, xi = carry
        rws = pl.ds(pl.multiple_of(t * batch, batch), batch)
        ar = ar_ref[...]
        ai = ai_ref[...]
        nxr = ar * xr - ai * xi + br_scr[rws, :]
        nxi = ar * xi + ai * xr + bi_scr[rws, :]
        br_scr[rws, :] = nxr
        bi_scr[rws, :] = nxi
        return nxr, nxi

    carry = (sr_scr[...], si_scr[...])
    if steps == 1:
        carry = step(0, carry)
    else:
        lo = jnp.clip(t_start - c * steps, 0, steps)
        carry = lax.fori_loop(lo, steps, step, carry)
    sr_scr[...] = carry[0]
    si_scr[...] = carry[1]
    xr_out[...] = carry[0]
    xi_out[...] = carry[1]

    ys = []
    for j in range(S5_TILES):
        ys.append(_mm(br_scr[:, j * half:(j + 1) * half], cre_ref[j], hp)
                  + _mm(bi_scr[:, j * half:(j + 1) * half], cim_ref[j], hp))
    if reorder:
        for s in range(S5_TILES):
            tm_scr[s] = ys[s]
        lax.fori_loop(0, steps, to_batch_major, 0)
        y = jnp.concatenate([bm_scr[s] for s in range(S5_TILES)], axis=1)
    else:
        y = jnp.concatenate(ys, axis=1)
    y = jax.nn.gelu(y + d_ref[...] * u)
    out = _mm(y, g1_ref[...], hp) * _sigmoid(_mm(y, g2_ref[...], hp))
    ya_ref[...] = out.reshape(ya_ref.shape)


def _s5_branch(x, batch, t_len, t_start, P, x0r, x0i, hp):
    seq = x.ndim == 3
    steps = _pick_s5_chunk(t_len) if seq else 1
    nchunks = t_len // steps
    rows = batch * steps
    if seq:
        x_spec = pl.BlockSpec((batch, steps, D_MODEL), lambda c: (0, c, 0))
        scratch_extra = [pltpu.VMEM((S5_TILES, rows, LANES), F32), pltpu.VMEM((S5_TILES, rows, LANES), F32)]
    else:
        x_spec = pl.BlockSpec((batch, D_MODEL), lambda c: (0, 0))
        scratch_extra = []
    state_spec = pl.BlockSpec((batch, S5_HALF), lambda c: (0, 0))
    consts = [P["g_mix"], P["w_u"], P["s5_bbar"], P["s5_ar"][batch], P["s5_ai"][batch], x0r, x0i,
              P["s5_cre"], P["s5_cim"], P["s5_d"], P["s5_glu1"], P["s5_glu2"]]
    return pl.pallas_call(
        functools.partial(_s5_kernel, batch=batch, steps=steps, t_start=t_start, hp=hp, reorder=seq),
        grid=(nchunks,),
        in_specs=[x_spec] + [_const_spec(a.shape) for a in consts],
        out_specs=(x_spec, state_spec, state_spec),
        out_shape=(jax.ShapeDtypeStruct(x.shape, F32),
                   jax.ShapeDtypeStruct((batch, S5_HALF), F32),
                   jax.ShapeDtypeStruct((batch, S5_HALF), F32)),
        scratch_shapes=[pltpu.VMEM((rows, S5_HALF), F32), pltpu.VMEM((rows, S5_HALF), F32),
                        pltpu.VMEM((batch, S5_HALF), F32), pltpu.VMEM((batch, S5_HALF), F32)] + scratch_extra,
        compiler_params=_cparams(("arbitrary",)),
        name="s5_branch_hp" if hp else "s5_branch",
    )(x, *consts)


N_KVEC = 5
N_VEC = N_KVEC + 1


def _rw_prep_kernel(x_ref, gm_ref, wrw_ref, sh0_ref, mu_ref, wl_ref, w0_ref, a0_ref, gup_ref,
                    kk_ref, ka_ref, rk_ref, ones_ref,
                    vec_o, g_o, bon_o, sh_o, carry_scr, *, seq, t_start, hp):
    h = _rms(x_ref[...], gm_ref[...])
    p = _mm(h, wrw_ref[...], hp)
    rows = p.shape[0]
    if seq:
        c = pl.program_id(1)

        @pl.when(c == 0)
        def _():
            carry_scr[...] = jnp.zeros_like(carry_scr)

        row = lax.broadcasted_iota(jnp.int32, p.shape, 0)
        prev = jnp.where(row == 0, carry_scr[0:1, :], pltpu.roll(p, 1, 0))
        prev = jnp.where(row + c * rows == t_start, sh0_ref[...], prev)
        carry_scr[0:1, :] = p[rows - 1:rows, :]

        @pl.when(c == pl.num_programs(1) - 1)
        def _():
            sh_o[...] = p[rows - 1:rows, :]
    else:
        prev = sh0_ref[...]
        sh_o[...] = p
    z = p + (prev - p) * mu_ref[...]
    r = z[:, :D_B]
    k = z[:, D_B:2 * D_B]
    v = z[:, 2 * D_B:3 * D_B]
    zwa = z[:, 3 * D_B:3 * D_B + LORA_W + LORA_A]
    zg = z[:, 3 * D_B + LORA_W + LORA_A:]
    lane = lax.broadcasted_iota(jnp.int32, zwa.shape, 1)
    tw = jnp.where(lane < LORA_W, jnp.tanh(zwa), zwa)
    lw = _mm(tw, wl_ref[...], hp)
    w_log = -_softplus(-(w0_ref[...] + lw[:, :D_B])) - 0.5
    decay = jnp.exp(-jnp.exp(w_log))
    a = _sigmoid(a0_ref[...] + lw[:, D_B:])
    g = _mm(_sigmoid(zg), gup_ref[...], hp)
    kk = k * kk_ref[...]
    n2 = _segsum(kk * kk, ones_ref[...], hp)
    kkn = kk * lax.rsqrt(jnp.maximum(n2, 1e-24))
    k2 = k * (1.0 + (a - 1.0) * ka_ref[...])
    rk = _segsum(r * k2 * rk_ref[...], ones_ref[...], hp)
    vec_o[0] = kkn
    vec_o[1] = decay
    vec_o[2] = -(kkn * a)
    vec_o[3] = k2
    vec_o[4] = r
    vec_o[5] = v
    g_o[...] = g
    bon_o[...] = rk * v


def _rw_prep(x, P, sh0, t_start, hp):
    seq = x.ndim == 3
    consts = [P["g_mix"], P["w_rw"]]
    consts2 = [P["rw_mu"], P["rw_wl"], P["rw_w0"], P["rw_a0"], P["rw_g_up"],
               P["rw_k_k"], P["rw_k_a"], P["rw_r_k"], P["ones_head"]]
    if seq:
        batch, t_len, _ = x.shape
        tc = _pick_tile(t_len, 544)
        grid = (batch, t_len // tc)
        row_spec = lambda w: pl.BlockSpec((None, tc, w), lambda b, c: (b, c, 0))
        vec_spec = pl.BlockSpec((N_VEC, None, tc, D_B), lambda b, c: (0, b, c, 0))
        sh_spec = pl.BlockSpec((None, 1, P_RW), lambda b, c: (b, 0, 0))
        sh0 = sh0.reshape(batch, 1, P_RW)
        vec_shape = (N_VEC, batch, t_len, D_B)
        g_shape = (batch, t_len, D_B)
        sh_shape = (batch, 1, P_RW)
        sem = ("arbitrary", "arbitrary")
    else:
        batch = x.shape[0]
        grid = (1,)
        row_spec = lambda w: pl.BlockSpec((batch, w), lambda i: (0, 0))
        vec_spec = pl.BlockSpec((N_VEC, batch, D_B), lambda i: (0, 0, 0))
        sh_spec = row_spec(P_RW)
        vec_shape = (N_VEC, batch, D_B)
        g_shape = (batch, D_B)
        sh_shape = (batch, P_RW)
        sem = ("arbitrary",)
    return pl.pallas_call(
        functools.partial(_rw_prep_kernel, seq=seq, t_start=t_start, hp=hp),
        grid=grid,
        in_specs=[row_spec(D_MODEL)] + [_const_spec(a.shape) for a in consts] + [sh_spec]
                 + [_const_spec(a.shape) for a in consts2],
        out_specs=(vec_spec, row_spec(D_B), row_spec(D_B), sh_spec),
        out_shape=(jax.ShapeDtypeStruct(vec_shape, F32), jax.ShapeDtypeStruct(g_shape, F32),
                   jax.ShapeDtypeStruct(g_shape, F32), jax.ShapeDtypeStruct(sh_shape, F32)),
        scratch_shapes=[pltpu.VMEM((SUBLANES, P_RW), F32)],
        compiler_params=_cparams(sem),
        name="rw_prep_hp" if hp else "rw_prep",
    )(x, *consts, sh0, *consts2)


RW_VH = RW_HEAD // (2 * SUBLANES)
RW_PAIRS = SUBLANES * RW_HEADS
RW_SC = RW_TC // 2


def _pairs_to_rows(x_ref, q_scr, batch):
    for b in range(batch):
        for j in range(DB_TILES):
            q_scr[pl.ds((b * DB_TILES + j) * LANES, LANES), :] = x_ref[b, :, j * LANES:(j + 1) * LANES].T


def _rw_kin_kernel(x_ref, o_ref, q_scr, *, batch):
    _pairs_to_rows(x_ref, q_scr, batch)
    for k in range(RW_HEAD):
        m = q_scr[pl.ds(k, RW_PAIRS, stride=RW_HEAD), :]
        o_ref[k] = jnp.concatenate([m, m], axis=0).T


def _rw_vin_kernel(x_ref, o_ref, q_scr, *, batch):
    _pairs_to_rows(x_ref, q_scr, batch)
    for vh in range(RW_VH):
        for vs in range(SUBLANES):
            v0 = (vh * SUBLANES + vs) * 2
            m0 = q_scr[pl.ds(v0, RW_PAIRS, stride=RW_HEAD), :]
            m1 = q_scr[pl.ds(v0 + 1, RW_PAIRS, stride=RW_HEAD), :]
            o_ref[vh, pl.ds(vs, RW_TC, stride=SUBLANES), :] = jnp.concatenate([m0, m1], axis=0).T


def _rw_unlayout_kernel(o3_ref, o_ref, q_scr, *, batch):
    for vh in range(RW_VH):
        for vs in range(SUBLANES):
            v0 = (vh * SUBLANES + vs) * 2
            zt = o3_ref[vh, pl.ds(vs, RW_TC, stride=SUBLANES), :].T
            q_scr[pl.ds(v0, RW_PAIRS, stride=RW_HEAD), :] = zt[:RW_PAIRS]
            q_scr[pl.ds(v0 + 1, RW_PAIRS, stride=RW_HEAD), :] = zt[RW_PAIRS:]
    for b in range(batch):
        for j in range(DB_TILES):
            o_ref[b, :, j * LANES:(j + 1) * LANES] = q_scr[pl.ds((b * DB_TILES + j) * LANES, LANES), :].T


def _rw_scan_kernel(kin_ref, vin_ref, s0_ref, o_ref, sfin_ref, s_scr, *, t_start):
    c = pl.program_id(0)

    @pl.when(c == 0)
    def _():
        s_scr[...] = s0_ref[...]

    def krow(t, vec, k):
        return jnp.broadcast_to(kin_ref[vec * RW_HEAD + k, pl.ds(t, 1), :], (SUBLANES, LANES))

    def acc_add(acc, vh, k, x):
        prev = acc[vh][k % 2]
        acc[vh][k % 2] = x if prev is None else prev + x

    def first_sa(t):
        acc = [[None, None] for _ in range(RW_VH)]
        for k in range(RW_HEAD):
            kkb = krow(t, 0, k)
            for vh in range(RW_VH):
                acc_add(acc, vh, k, s_scr[vh, k] * kkb)
        return tuple(a[0] + a[1] for a in acc)

    def step(t, sa):
        tn = jnp.minimum(t + 1, RW_SC - 1)
        tv = pl.multiple_of(t * SUBLANES, SUBLANES)
        vv = [vin_ref[vh, pl.ds(tv, SUBLANES), :] for vh in range(RW_VH)]
        oacc = [[None, None] for _ in range(RW_VH)]
        nacc = [[None, None] for _ in range(RW_VH)]
        for k in range(RW_HEAD):
            wb = krow(t, 1, k)
            nbb = krow(t, 2, k)
            kb = krow(t, 3, k)
            rb = krow(t, 4, k)
            kkn = krow(tn, 0, k)
            for vh in range(RW_VH):
                s = s_scr[vh, k] * wb + sa[vh] * nbb + vv[vh] * kb
                s_scr[vh, k] = s
                acc_add(oacc, vh, k, s * rb)
                acc_add(nacc, vh, k, s * kkn)
        for vh in range(RW_VH):
            o_ref[vh, pl.ds(tv, SUBLANES), :] = oacc[vh][0] + oacc[vh][1]
        return tuple(a[0] + a[1] for a in nacc)

    lo = jnp.clip(t_start - c * RW_SC, 0, RW_SC)
    lax.fori_loop(lo, RW_SC, step, first_sa(jnp.minimum(lo, RW_SC - 1)))
    sfin_ref[...] = s_scr[...]


def _rw_recurrence_long(vecs, s0, t_start):
    _, batch, t_len, _ = vecs.shape
    assert batch == SUBLANES and t_len % RW_TC == 0
    nchunks = t_len // RW_TC
    q_scr = pltpu.VMEM((RW_PAIRS * RW_HEAD, LANES), F32)
    kin = pl.pallas_call(
        functools.partial(_rw_kin_kernel, batch=batch),
        grid=(nchunks, N_KVEC),
        in_specs=[pl.BlockSpec((None, batch, RW_TC, D_B), lambda c, i: (i, 0, c, 0))],
        out_specs=pl.BlockSpec((None, RW_HEAD, RW_TC, LANES), lambda c, i: (c, i, 0, 0)),
        out_shape=jax.ShapeDtypeStruct((nchunks, N_KVEC * RW_HEAD, RW_TC, LANES), F32),
        scratch_shapes=[q_scr],
        compiler_params=_cparams(("arbitrary", "arbitrary")),
        name="rw_kin",
    )(vecs)
    vin = pl.pallas_call(
        functools.partial(_rw_vin_kernel, batch=batch),
        grid=(nchunks,),
        in_specs=[pl.BlockSpec((None, batch, RW_TC, D_B), lambda c: (N_KVEC, 0, c, 0))],
        out_specs=pl.BlockSpec((None, RW_VH, RW_TC * SUBLANES, LANES), lambda c: (c, 0, 0, 0)),
        out_shape=jax.ShapeDtypeStruct((nchunks, RW_VH, RW_TC * SUBLANES, LANES), F32),
        scratch_shapes=[q_scr],
        compiler_params=_cparams(("arbitrary",)),
        name="rw_vin",
    )(vecs)
    sshape = (RW_VH, RW_HEAD, SUBLANES, LANES)
    s = s0.reshape(batch, RW_HEADS, RW_VH, SUBLANES, 2, RW_HEAD)
    s = jnp.transpose(s, (2, 5, 3, 4, 0, 1)).reshape(sshape)
    state_spec = pl.BlockSpec(sshape, lambda c: (0, 0, 0, 0))
    halves = RW_TC // RW_SC
    vblk = (None, RW_VH, RW_SC * SUBLANES, LANES)
    vmap = lambda c: (c // halves, 0, c % halves, 0)
    o3, sfin = pl.pallas_call(
        functools.partial(_rw_scan_kernel, t_start=t_start),
        grid=(nchunks * halves,),
        in_specs=[pl.BlockSpec((None, N_KVEC * RW_HEAD, RW_SC, LANES), vmap),
                  pl.BlockSpec(vblk, vmap), state_spec],
        out_specs=(pl.BlockSpec(vblk, vmap), state_spec),
        out_shape=(jax.ShapeDtypeStruct((nchunks, RW_VH, RW_TC * SUBLANES, LANES), F32),
                   jax.ShapeDtypeStruct(sshape, F32)),
        scratch_shapes=[pltpu.VMEM(sshape, F32)],
        compiler_params=_cparams(("arbitrary",)),
        name="rw_scan",
    )(kin, vin, s)
    o = pl.pallas_call(
        functools.partial(_rw_unlayout_kernel, batch=batch),
        grid=(nchunks,),
        in_specs=[pl.BlockSpec((None, RW_VH, RW_TC * SUBLANES, LANES), lambda c: (c, 0, 0, 0))],
        out_specs=pl.BlockSpec((batch, RW_TC, D_B), lambda c: (0, c, 0)),
        out_shape=jax.ShapeDtypeStruct((batch, t_len, D_B), F32),
        scratch_shapes=[q_scr],
        compiler_params=_cparams(("arbitrary",)),
        name="rw_unlayout",
    )(o3)
    sfin = sfin.reshape(RW_VH, RW_HEAD, SUBLANES, 2, batch, RW_HEADS)
    sfin = jnp.transpose(sfin, (4, 5, 0, 2, 3, 1)).reshape(batch, RW_HEADS, RW_HEAD, RW_HEAD)
    return o, sfin


STEP_PAIRS = 64
STEP_VROWS = RW_HEAD // 2


def _split3(x):
    hi = x.astype(BF16)
    r1 = x - hi.astype(F32)
    mid = r1.astype(BF16)
    return hi, mid, (r1 - mid.astype(F32)).astype(BF16)


def _rw_step_kernel(s_ref, km_ref, v_ref, sel_ref, vmask_ref, vexp_ref, ones_ref, s_o, o_o):
    sel = sel_ref[...]

    def rows_of(x):
        return sum(jnp.dot(sel, p, preferred_element_type=F32) for p in _split3(x))

    kx = rows_of(km_ref[...])
    kk, w, nb, k, r = [kx[:, i * LANES:(i + 1) * LANES] for i in range(N_KVEC)]
    vx = _segsum(rows_of(v_ref[...]) * vmask_ref[...], vexp_ref[...], True)
    s = s_ref[...]
    ones = ones_ref[...]
    sa = _segsum(s * kk, ones, True)
    s = s * w + sa * nb + vx * k
    s_o[...] = s
    o_o[...] = _segsum(s * r, ones, True)


def _rw_recurrence_step(vecs, s0, ones_half):
    batch = vecs.shape[1]
    pairs = batch * RW_HEADS
    rows = pairs * STEP_VROWS
    tr = STEP_PAIRS * STEP_VROWS
    assert pairs % STEP_PAIRS == 0
    kp = vecs[:N_KVEC].reshape(N_KVEC, pairs, 1, RW_HEAD)
    km = jnp.transpose(jnp.broadcast_to(kp, (N_KVEC, pairs, 2, RW_HEAD)), (1, 0, 2, 3)).reshape(pairs, N_KVEC * LANES)
    vp = vecs[N_KVEC].reshape(pairs, RW_HEAD)
    ridx = jnp.arange(tr, dtype=jnp.int32)
    sel = (ridx[:, None] // STEP_VROWS == jnp.arange(STEP_PAIRS, dtype=jnp.int32)[None, :]).astype(BF16)
    vidx = jnp.arange(RW_HEAD, dtype=jnp.int32)
    vmask = (vidx[None, :] // 2 == ridx[:, None] % STEP_VROWS).astype(F32)
    vexp = (vidx[:, None] % 2 == jnp.arange(LANES, dtype=jnp.int32)[None, :] // RW_HEAD).astype(BF16)
    spec = pl.BlockSpec((tr, LANES), lambda i: (i, 0))
    big = jax.ShapeDtypeStruct((rows, LANES), F32)
    consts = [sel, vmask, vexp, ones_half]
    s_new, o = pl.pallas_call(
        _rw_step_kernel,
        grid=(pairs // STEP_PAIRS,),
        in_specs=[spec, pl.BlockSpec((STEP_PAIRS, N_KVEC * LANES), lambda i: (i, 0)),
                  pl.BlockSpec((STEP_PAIRS, RW_HEAD), lambda i: (i, 0))]
                 + [_const_spec(a.shape) for a in consts],
        out_specs=(spec, spec),
        out_shape=(big, big),
        compiler_params=_cparams(("arbitrary",)),
        name="rw_step",
    )(s0.reshape(rows, LANES), km, vp, *consts)
    o = o.reshape(rows, 2, RW_HEAD)[:, :, 0].reshape(batch, D_B)
    return o, s_new.reshape(batch, RW_HEADS, RW_HEAD, RW_HEAD)


def _mix_kernel(*refs, hp, moe):
    (x_ref, ya_ref, o_ref, bon_ref, g_ref, gm_ref, wg_ref, gng_ref, gnb_ref, ones_ref,
     wbo_ref, wout_ref, gf_ref) = refs[:13]
    if moe:
        wr_ref, x1_o, h2_o, route_o = refs[13:]
    else:
        x1_o, h2_o = refs[13:]
    x = x_ref[...]
    h = _rms(x, gm_ref[...])
    gates = _mm(h, wg_ref[...], hp)
    o = o_ref[...]
    ones = ones_ref[...]
    inv_n = 1.0 / RW_HEAD
    mean = _segsum(o, ones, hp) * inv_n
    dlt = o - mean
    var = _segsum(dlt * dlt, ones, hp) * inv_n
    on = dlt * lax.rsqrt(var + GN_EPS) * gng_ref[...] + gnb_ref[...]
    yb = _mm((on + bon_ref[...]) * g_ref[...], wbo_ref[...], hp)
    m = _sigmoid(gates[:, :D_MODEL]) * ya_ref[...] + _sigmoid(gates[:, D_MODEL:]) * yb
    x1 = x + _mm(m, wout_ref[...], hp)
    h2 = _rms(x1, gf_ref[...])
    x1_o[...] = x1
    h2_o[...] = h2.astype(h2_o.dtype)
    if moe:
        logits = _mm(h2, wr_ref[...], hp)
        lane = lax.broadcasted_iota(jnp.int32, logits.shape, 1).astype(F32)
        neg = jnp.float32(-jnp.inf)
        lg = jnp.where(lane < N_EXPERTS, logits, neg)
        m1 = jnp.max(lg, axis=1, keepdims=True)
        i1 = jnp.min(jnp.where(lg == m1, lane, float(LANES)), axis=1, keepdims=True)
        lg2 = jnp.where(lane == i1, neg, lg)
        m2 = jnp.max(lg2, axis=1, keepdims=True)
        i2 = jnp.min(jnp.where(lg2 == m2, lane, float(LANES)), axis=1, keepdims=True)
        e = jnp.exp(m2 - m1)
        g1 = 1.0 / (1.0 + e)
        g2 = e / (1.0 + e)
        route_o[...] = jnp.where(lane == 0.0, i1, jnp.where(lane == 1.0, i2,
                                 jnp.where(lane == 2.0, g1, jnp.where(lane == 3.0, g2, 0.0))))


def _mix(x, ya, o, bon, g, P, hp, moe):
    rows = x.shape[0]
    tm = _pick_tile(rows, 344)
    row_spec = lambda w: pl.BlockSpec((tm, w), lambda i: (i, 0))
    consts = [P["g_mix"], P["w_gate"], P["rw_gn_g"], P["rw_gn_b"], P["ones_head"],
              P["rw_w_bo"], P["w_out"], P["g_ffn"]]
    if moe:
        consts.append(P["w_router"])
    h2_dtype = F32 if hp else BF16
    out_specs = [row_spec(D_MODEL), row_spec(D_MODEL)]
    out_shape = [jax.ShapeDtypeStruct((rows, D_MODEL), F32), jax.ShapeDtypeStruct((rows, D_MODEL), h2_dtype)]
    if moe:
        out_specs.append(row_spec(LANES))
        out_shape.append(jax.ShapeDtypeStruct((rows, LANES), F32))
    return pl.pallas_call(
        functools.partial(_mix_kernel, hp=hp, moe=moe),
        grid=(rows // tm,),
        in_specs=[row_spec(D_MODEL), row_spec(D_MODEL), row_spec(D_B), row_spec(D_B), row_spec(D_B)]
                 + [_const_spec(a.shape) for a in consts],
        out_specs=tuple(out_specs),
        out_shape=tuple(out_shape),
        compiler_params=_cparams(("arbitrary",)),
        name=("mix_moe" if moe else "mix") + ("_hp" if hp else ""),
    )(x, ya, o, bon, g, *consts)


def _ffn_kernel(h_ref, x1_ref, w1_ref, w3_ref, w2_ref, o_ref, *, hp):
    h = h_ref[...]
    a = _mm(h, w1_ref[...], hp)
    b = _mm(h, w3_ref[...], hp)
    o_ref[...] = x1_ref[...] + _mm(a * _sigmoid(a) * b, w2_ref[...], hp)


def _ffn(h2, x1, w1, w3, w2, tm, hp):
    rows = h2.shape[0]
    assert rows % tm == 0
    row_spec = pl.BlockSpec((tm, D_MODEL), lambda i: (i, 0))
    return pl.pallas_call(
        functools.partial(_ffn_kernel, hp=hp),
        grid=(rows // tm,),
        in_specs=[row_spec, row_spec, _const_spec(w1.shape), _const_spec(w3.shape), _const_spec(w2.shape)],
        out_specs=row_spec,
        out_shape=jax.ShapeDtypeStruct((rows, D_MODEL), F32),
        compiler_params=_cparams(("arbitrary",)),
        name="ffn_hp" if hp else "ffn",
    )(h2, x1, w1, w3, w2)


MOE_TM = 512
MOE_TF = 1792


def _moe_kernel(te_ref, nv_ref, x_ref, w1_ref, w3_ref, w2_ref, o_ref, acc):
    i = pl.program_id(0)
    f = pl.program_id(1)

    @pl.when(f == 0)
    def _():
        acc[...] = jnp.zeros_like(acc)

    @pl.when(i < nv_ref[0])
    def _():
        x = x_ref[...]
        a = jnp.dot(x, w1_ref[...], preferred_element_type=F32)
        b = jnp.dot(x, w3_ref[...], preferred_element_type=F32)
        acc[...] += jnp.dot((a * _sigmoid(a) * b).astype(BF16), w2_ref[...], preferred_element_type=F32)

    @pl.when(f == pl.num_programs(1) - 1)
    def _():
        o_ref[...] = acc[...]


def _moe_experts(xs, tile_expert, n_valid, w1, w3, w2):
    rows = xs.shape[0]
    ntiles = rows // MOE_TM
    nf = D_EXPERT // MOE_TF
    assert ntiles * MOE_TM == rows and nf * MOE_TF == D_EXPERT

    def fblk(i, f, nv):
        return jnp.where(i < nv[0], f, nf - 1)

    grid_spec = pltpu.PrefetchScalarGridSpec(
        num_scalar_prefetch=2,
        grid=(ntiles, nf),
        in_specs=[pl.BlockSpec((MOE_TM, D_MODEL), lambda i, f, te, nv: (i, 0)),
                  pl.BlockSpec((None, D_MODEL, MOE_TF), lambda i, f, te, nv: (te[i], 0, fblk(i, f, nv))),
                  pl.BlockSpec((None, D_MODEL, MOE_TF), lambda i, f, te, nv: (te[i], 0, fblk(i, f, nv))),
                  pl.BlockSpec((None, MOE_TF, D_MODEL), lambda i, f, te, nv: (te[i], fblk(i, f, nv), 0))],
        out_specs=pl.BlockSpec((MOE_TM, D_MODEL), lambda i, f, te, nv: (i, 0)),
        scratch_shapes=[pltpu.VMEM((MOE_TM, D_MODEL), F32)])
    return pl.pallas_call(
        _moe_kernel,
        grid_spec=grid_spec,
        out_shape=jax.ShapeDtypeStruct((rows, D_MODEL), F32),
        compiler_params=_cparams(("arbitrary", "arbitrary")),
        name="moe_experts",
    )(tile_expert, n_valid, xs, w1, w3, w2)


def _moe_plan(expert_idx, src_row):
    n_pairs = expert_idx.shape[0] * 2
    flat_e = expert_idx.reshape(n_pairs)
    onehot = (flat_e[:, None] == jnp.arange(N_EXPERTS, dtype=jnp.int32)[None, :]).astype(jnp.int32)
    csum = jnp.cumsum(onehot, axis=0)
    rank = jnp.sum(csum * onehot, axis=1) - 1
    counts = csum[-1]
    padded = ((counts + MOE_TM - 1) // MOE_TM) * MOE_TM
    pend = jnp.cumsum(padded)
    pstart = pend - padded
    dest = pstart[flat_e] + rank
    ntiles = (n_pairs + N_EXPERTS * (MOE_TM - 1)) // MOE_TM + 1
    rows = ntiles * MOE_TM
    tile_start = jnp.arange(ntiles, dtype=jnp.int32) * MOE_TM
    n_valid = (pend[-1] // MOE_TM).astype(jnp.int32)
    te = jnp.sum((pend[None, :] <= tile_start[:, None]).astype(jnp.int32), axis=1)
    te = jnp.minimum(te, N_EXPERTS - 1)
    order = jnp.argsort(flat_e, stable=True).astype(jnp.int32)
    gstart = jnp.cumsum(counts) - counts
    slot = jnp.arange(rows, dtype=jnp.int32)
    e_slot = jnp.repeat(te, MOE_TM)
    off = slot - pstart[e_slot]
    filled = (off < counts[e_slot]) & (slot < pend[-1])
    pair = order[jnp.clip(gstart[e_slot] + off, 0, n_pairs - 1)]
    src = jnp.where(filled, src_row[pair // 2], 0).astype(jnp.int32)
    last_e = te[jnp.maximum(n_valid - 1, 0)]
    te = jnp.where(jnp.arange(ntiles) < n_valid, te, last_e).astype(jnp.int32)
    return src, dest.reshape(-1, 2), te, n_valid.reshape(1)


def _combine_kernel(x1_ref, y1_ref, y2_ref, route_ref, gf_ref, o_ref):
    route = route_ref[...]
    x2 = x1_ref[...] + route[:, 2:3] * y1_ref[...] + route[:, 3:4] * y2_ref[...]
    o_ref[...] = _rms(x2, gf_ref[...])


def _combine_norm(x1, x1_spec, flat_blk0, grid, out_shape, out_spec, y1, y2, route, g_final, tm):
    nd = len(grid)

    def flat_map(*idx):
        lin = idx[0]
        for k in range(1, nd):
            lin = lin * grid[k] + idx[k]
        return (flat_blk0 + lin, 0)

    return pl.pallas_call(
        _combine_kernel,
        grid=grid,
        in_specs=[x1_spec, pl.BlockSpec((tm, D_MODEL), flat_map), pl.BlockSpec((tm, D_MODEL), flat_map),
                  pl.BlockSpec((tm, LANES), flat_map), _const_spec(g_final.shape)],
        out_specs=out_spec,
        out_shape=out_shape,
        compiler_params=_cparams(("arbitrary",) * nd),
        name="combine_norm",
    )(x1, y1, y2, route, g_final)


def _layer_params(l, W, hp, batches):
    wdt = F32 if hp else BF16
    row = lambda a: a.reshape(1, -1).astype(F32)
    w_in = W["w_in"][l]
    P = {
        "g_mix": row(W["norm_mix"][l]),
        "g_ffn": row(W["norm_ffn"][l]),
        "w_u": w_in[:, :D_A].astype(wdt),
        "w_rw": w_in[:, D_A:D_A + P_RW].astype(wdt),
        "w_gate": w_in[:, D_A + P_RW:].astype(wdt),
        "s5_d": row(W["s5_d"][l]),
        "s5_glu1": W["s5_glu1"][l].astype(wdt),
        "s5_glu2": W["s5_glu2"][l].astype(wdt),
        "rw_mu": row(W["rw_mu"][l]),
        "rw_w0": row(W["rw_w0"][l]),
        "rw_a0": row(W["rw_a0"][l]),
        "rw_g_up": W["rw_g_up"][l].astype(wdt),
        "rw_k_k": row(W["rw_k_k"][l]),
        "rw_k_a": row(W["rw_k_a"][l]),
        "rw_r_k": row(W["rw_r_k"][l]),
        "rw_gn_g": row(W["rw_gn_g"][l]),
        "rw_gn_b": row(W["rw_gn_b"][l]),
        "rw_w_bo": W["rw_w_bo"][l].astype(wdt),
        "w_out": W["w_out"][l].astype(wdt),
    }
    wl = jnp.zeros((LORA_W + LORA_A, 2 * D_B), F32)
    wl = wl.at[:LORA_W, :D_B].set(W["rw_w_up"][l]).at[LORA_W:, D_B:].set(W["rw_a_up"][l])
    P["rw_wl"] = wl.astype(wdt)
    P["ones_head"] = jnp.kron(jnp.eye(RW_HEADS, dtype=F32), jnp.ones((RW_HEAD, RW_HEAD), F32)).astype(BF16)

    ab_re, ab_im, bb_re, bb_im = _s5_discretize(W["s5_lam_re"][l], W["s5_lam_im"][l], W["s5_log_dt"][l],
                                                W["s5_b_re"][l], W["s5_b_im"][l])
    P["s5_ar"] = {b: jnp.broadcast_to(ab_re.reshape(1, S5_HALF), (b, S5_HALF)) for b in batches}
    P["s5_ai"] = {b: jnp.broadcast_to(ab_im.reshape(1, S5_HALF), (b, S5_HALF)) for b in batches}
    gpt = S5_GROUPS // S5_TILES
    eye = jnp.eye(gpt, dtype=F32)
    bb = jnp.stack([bb_re, bb_im]).reshape(2, S5_GROUP, S5_TILES, gpt, S5_STATE)
    bb = jnp.transpose(bb, (2, 1, 0, 3, 4))
    bbar = bb[:, None] * eye[None, :, None, None, :, None]
    P["s5_bbar"] = bbar.reshape(S5_TILES, LANES, 2 * gpt * S5_STATE).astype(wdt)
    cs = jnp.stack([W["s5_c_re"][l], -W["s5_c_im"][l]]).reshape(2, S5_TILES, gpt, S5_GROUP, S5_STATE)
    cs = jnp.transpose(cs, (1, 0, 2, 4, 3))
    cm = cs[:, :, :, :, None, :] * eye[None, None, :, None, :, None]
    cm = cm.reshape(S5_TILES, 2, gpt * S5_STATE, LANES).astype(wdt)
    P["s5_cre"] = cm[:, 0]
    P["s5_cim"] = cm[:, 1]
    return P


def _mixer_seq(x, t_start, s5r0, s5i0, rw_s0, rw_sh0, P, moe):
    batch, t_len, _ = x.shape
    ya, xr, xi = _s5_branch(x, batch, t_len, t_start, P, s5r0, s5i0, False)
    vecs, g, bon, sh = _rw_prep(x, P, rw_sh0, t_start, False)
    o, s_fin = _rw_recurrence_long(vecs, rw_s0, t_start)
    flat = lambda a: a.reshape(batch * t_len, a.shape[-1])
    outs = _mix(flat(x), flat(ya), flat(o), flat(bon), flat(g), P, False, moe)
    return outs, (xr, xi, s_fin, sh.reshape(batch, P_RW))


def _mixer_step(x, s5r0, s5i0, rw_s0, rw_sh0, P, moe, ones_half):
    batch = x.shape[0]
    ya, xr, xi = _s5_branch(x, batch, 1, 0, P, s5r0, s5i0, True)
    vecs, g, bon, sh = _rw_prep(x, P, rw_sh0, 0, True)
    o, s_fin = _rw_recurrence_step(vecs, rw_s0, ones_half)
    outs = _mix(x, ya, o, bon, g, P, True, moe)
    return outs, (xr, xi, s_fin, sh)


def kernel(x_prompt, x_sample, state_s5_re, state_s5_im, state_rwkv, state_shift, meta_tokens, norm_mix, w_in, s5_lam_re, s5_lam_im, s5_log_dt, s5_b_re, s5_b_im, s5_c_re, s5_c_im, s5_d, s5_glu1, s5_glu2, rw_mu, rw_w0, rw_w_up, rw_a0, rw_a_up, rw_g_up, rw_k_k, rw_k_a, rw_r_k, rw_gn_g, rw_gn_b, rw_w_bo, w_out, norm_ffn, ffn_w1, ffn_w3, ffn_w2, moe_router, moe_w1, moe_w3, moe_w2, norm_final):
    W = dict(norm_mix=norm_mix, w_in=w_in, s5_lam_re=s5_lam_re, s5_lam_im=s5_lam_im,
             s5_log_dt=s5_log_dt, s5_b_re=s5_b_re, s5_b_im=s5_b_im, s5_c_re=s5_c_re,
             s5_c_im=s5_c_im, s5_d=s5_d, s5_glu1=s5_glu1, s5_glu2=s5_glu2, rw_mu=rw_mu,
             rw_w0=rw_w0, rw_w_up=rw_w_up, rw_a0=rw_a0, rw_a_up=rw_a_up, rw_g_up=rw_g_up,
             rw_k_k=rw_k_k, rw_k_a=rw_k_a, rw_r_k=rw_r_k, rw_gn_g=rw_gn_g, rw_gn_b=rw_gn_b,
             rw_w_bo=rw_w_bo, w_out=w_out, norm_ffn=norm_ffn)
    bp, seq, _ = x_prompt.shape
    bs = x_sample.shape[0]
    assert x_sample.shape[1] == 1 and seq % RW_TC == 0 and bp == SUBLANES
    tp = N_META + seq
    t_pad = -(-tp // RW_TC) * RW_TC
    t_start = t_pad - tp
    out0 = t_pad - seq

    meta = jnp.broadcast_to(meta_tokens.astype(F32)[None], (bp, N_META, D_MODEL))
    xp = jnp.concatenate([jnp.zeros((bp, t_start, D_MODEL), F32), meta, x_prompt], axis=1)
    xs = x_sample.reshape(bs, D_MODEL)
    ones_half = jnp.kron(jnp.eye(2, dtype=F32), jnp.ones((RW_HEAD, RW_HEAD), F32)).astype(BF16)
    g_final = norm_final.reshape(1, D_MODEL).astype(F32)

    zero_s5 = jnp.zeros((bp, S5_HALF), F32)
    zero_rw = jnp.zeros((bp, RW_HEADS, RW_HEAD, RW_HEAD), F32)
    zero_sh = jnp.zeros((bp, P_RW), F32)

    p_states, s_states = [], []
    for l in range(DEPTH):
        moe = (l % 2 == 1)
        j = l // 2
        Pl = _layer_params(l, W, False, (bp,))
        Ph = _layer_params(l, W, True, (bs,))
        if moe:
            wr = jnp.zeros((D_MODEL, LANES), F32).at[:, :N_EXPERTS].set(moe_router[j])
            Pl["w_router"] = wr.astype(BF16)
            Ph["w_router"] = wr
        outs_p, st_p = _mixer_seq(xp, t_start, zero_s5, zero_s5, zero_rw, zero_sh, Pl, moe)
        outs_s, st_s = _mixer_step(xs, state_s5_re[l].reshape(bs, S5_HALF), state_s5_im[l].reshape(bs, S5_HALF),
                                   state_rwkv[l], state_shift[l], Ph, moe, ones_half)
        p_states.append(st_p)
        s_states.append(st_s)
        if not moe:
            rows = bp * t_pad
            xp = _ffn(outs_p[1], outs_p[0], ffn_w1[j].astype(BF16), ffn_w3[j].astype(BF16),
                      ffn_w2[j].astype(BF16), _pick_tile(rows, 544), False).reshape(bp, t_pad, D_MODEL)
            xs = _ffn(outs_s[1], outs_s[0], ffn_w1[j], ffn_w3[j], ffn_w2[j], bs, True)
        else:
            assert l == DEPTH - 1
            n_p = bp * seq
            h2 = jnp.concatenate([outs_p[1], outs_s[1].astype(BF16)], axis=0)
            route_p = outs_p[2].reshape(bp, t_pad, LANES)[:, out0:].reshape(n_p, LANES)
            route = jnp.concatenate([route_p, outs_s[2]], axis=0)
            tok = jnp.arange(n_p, dtype=jnp.int32)
            src_row = jnp.concatenate([(tok // seq) * t_pad + out0 + tok % seq,
                                       bp * t_pad + jnp.arange(bs, dtype=jnp.int32)])
            src, dest, te, n_valid = _moe_plan(route[:, :2].astype(jnp.int32), src_row)
            take = lambda a, idx: a.at[idx].get(mode="promise_in_bounds")
            y_sorted = _moe_experts(take(h2, src), te, n_valid,
                                    moe_w1[j].astype(BF16), moe_w3[j].astype(BF16), moe_w2[j].astype(BF16))
            y1 = take(y_sorted, dest[:, 0])
            y2 = take(y_sorted, dest[:, 1])
            tm = RW_TC
            nblk = seq // tm
            y_prompt = _combine_norm(
                outs_p[0].reshape(bp, t_pad, D_MODEL),
                pl.BlockSpec((None, tm, D_MODEL), lambda b, s: (b, s + out0 // tm, 0)), 0, (bp, nblk),
                jax.ShapeDtypeStruct((bp, seq, D_MODEL), F32),
                pl.BlockSpec((None, tm, D_MODEL), lambda b, s: (b, s, 0)), y1, y2, route, g_final, tm)
            assert n_p % bs == 0
            y_sample = _combine_norm(
                outs_s[0], pl.BlockSpec((bs, D_MODEL), lambda i: (0, 0)), n_p // bs, (1,),
                jax.ShapeDtypeStruct((bs, D_MODEL), F32),
                pl.BlockSpec((bs, D_MODEL), lambda i: (0, 0)), y1, y2, route, g_final, bs)

    y_sample = y_sample.reshape(bs, 1, D_MODEL)

    def stack(states, b):
        re = jnp.stack([s[0].reshape(b, S5_GROUPS, S5_STATE) for s in states])
        im = jnp.stack([s[1].reshape(b, S5_GROUPS, S5_STATE) for s in states])
        rw = jnp.stack([s[2] for s in states])
        sh = jnp.stack([s[3] for s in states])
        return re, im, rw, sh

    p_re, p_im, p_rw, p_sh = stack(p_states, bp)
    s_re, s_im, s_rw, s_sh = stack(s_states, bs)
    return (y_prompt, y_sample, p_re, p_im, p_rw, p_sh, s_re, s_im, s_rw, s_sh)
```

```python
import functools

import jax
import jax.numpy as jnp
from jax import lax
from jax.experimental import pallas as pl
from jax.experimental.pallas import tpu as pltpu

F32 = jnp.float32
BF16 = jnp.bfloat16
HIGHEST = lax.Precision.HIGHEST

D_MODEL = 1024
DEPTH = 2
N_META = 16
D_A = 512
S5_GROUP = 16
S5_GROUPS = 32
S5_STATE = 64
S5_TILES = 4
S5_HALF = S5_GROUPS * S5_STATE
D_B = 512
RW_HEAD = 64
RW_HEADS = 8
LORA_W = 64
LORA_A = 64
LORA_G = 128
GN_EPS = 64e-5
P_RW = 3 * D_B + LORA_W + LORA_A + LORA_G
D_FF = 2816
N_EXPERTS = 8
D_EXPERT = 3584
RMS_EPS = 1e-6
LANES = 128
SUBLANES = 8
VMEM_LIMIT_MB = 56
RW_TC = LANES
DB_TILES = D_B // LANES


def _pick_tile(n, pref, mult=SUBLANES):
    best = None
    for d in range(mult, min(n, pref) + 1, mult):
        if n % d == 0:
            best = d
    return best if best is not None else n


def _pick_s5_chunk(t_pad, pref=136):
    cands = [d for d in range(SUBLANES, min(t_pad, pref) + 1, SUBLANES) if t_pad % d == 0]
    odd = [d for d in cands if (d // SUBLANES) % 2 == 1]
    return max(odd) if odd else max(cands)


def _cparams(sem):
    return pltpu.CompilerParams(dimension_semantics=sem,
                                vmem_limit_bytes=VMEM_LIMIT_MB * 1024 * 1024)


def _const_spec(shape):
    nd = len(shape)
    return pl.BlockSpec(shape, lambda *_: (0,) * nd, pipeline_mode=pl.Buffered(1))


def _mm(a, b, hp):
    if hp:
        return jnp.dot(a.astype(F32), b, precision=HIGHEST, preferred_element_type=F32)
    return jnp.dot(a.astype(BF16), b, preferred_element_type=F32)


def _segsum(x, ones, hp):
    hi = x.astype(BF16)
    r1 = x - hi.astype(F32)
    mid = r1.astype(BF16)
    out = (jnp.dot(hi, ones, preferred_element_type=F32)
           + jnp.dot(mid, ones, preferred_element_type=F32))
    if hp:
        lo = (r1 - mid.astype(F32)).astype(BF16)
        out = out + jnp.dot(lo, ones, preferred_element_type=F32)
    return out


def _sigmoid(x):
    return 1.0 / (1.0 + jnp.exp(-x))


def _softplus(x):
    return jnp.maximum(x, 0.0) + jnp.log1p(jnp.exp(-jnp.abs(x)))


def _rms(x, g):
    return x * lax.rsqrt(jnp.mean(x * x, axis=-1, keepdims=True) + RMS_EPS) * g


def _s5_disc_kernel(lr_ref, li_ref, ldt_ref, btr_ref, bti_ref, abr_o, abi_o, bbr_o, bbi_o):
    lr = lr_ref[...]
    li = li_ref[...]
    dt = jnp.exp(ldt_ref[...])
    mag = jnp.exp(lr * dt)
    ab_re = mag * jnp.cos(li * dt)
    ab_im = mag * jnp.sin(li * dt)
    den = lr * lr + li * li
    q_re = ((ab_re - 1.0) * lr + ab_im * li) / den
    q_im = (ab_im * lr - (ab_re - 1.0) * li) / den
    abr_o[...] = ab_re
    abi_o[...] = ab_im
    for c in range(S5_GROUP):
        b_re = btr_ref[c]
        b_im = bti_ref[c]
        bbr_o[c] = q_re * b_re - q_im * b_im
        bbi_o[c] = q_re * b_im + q_im * b_re


def _s5_discretize(lam_re, lam_im, log_dt, b_re, b_im):
    gp = jax.ShapeDtypeStruct((S5_GROUPS, S5_STATE), F32)
    cgp = jax.ShapeDtypeStruct((S5_GROUP, S5_GROUPS, S5_STATE), F32)
    return pl.pallas_call(_s5_disc_kernel, out_shape=(gp, gp, cgp, cgp), name="s5_disc")(
        lam_re, lam_im, log_dt[:, None],
        jnp.transpose(b_re, (2, 0, 1)), jnp.transpose(b_im, (2, 0, 1)))


def _s5_kernel(x_ref, gm_ref, wu_ref, bbar_ref, ar_ref, ai_ref, x0r_ref, x0i_ref,
               cre_ref, cim_ref, d_ref, g1_ref, g2_ref,
               ya_ref, xr_out, xi_out, br_scr, bi_scr, sr_scr, si_scr, *rest,
               batch, steps, t_start, hp, reorder):
    c = pl.program_id(0)

    @pl.when(c == 0)
    def _():
        sr_scr[...] = x0r_ref[...]
        si_scr[...] = x0i_ref[...]

    rows = batch * steps
    x = x_ref[...]
    if reorder:
        bm_scr, tm_scr = rest
        x = x.reshape(rows, D_MODEL)
    h = _rms(x, gm_ref[...])
    u = _mm(h, wu_ref[...], hp)

    def to_time_major(t, carry):
        for s in range(S5_TILES):
            tm_scr[s, pl.ds(pl.multiple_of(t * batch, batch), batch), :] = bm_scr[s, pl.ds(t, batch, stride=steps), :]
        return carry

    def to_batch_major(t, carry):
        for s in range(S5_TILES):
            bm_scr[s, pl.ds(t, batch, stride=steps), :] = tm_scr[s, pl.ds(pl.multiple_of(t * batch, batch), batch), :]
        return carry

    if reorder:
        for s in range(S5_TILES):
            bm_scr[s] = u[:, s * LANES:(s + 1) * LANES]
        lax.fori_loop(0, steps, to_time_major, 0)
        u_tm = jnp.concatenate([tm_scr[s] for s in range(S5_TILES)], axis=1)
    else:
        u_tm = u
    half = S5_HALF // S5_TILES
    for j in range(S5_TILES):
        bbj = _mm(u_tm[:, j * LANES:(j + 1) * LANES], bbar_ref[j], hp)
        br_scr[:, j * half:(j + 1) * half] = bbj[:, :half]
        bi_scr[:, j * half:(j + 1) * half] = bbj[:, half:]

    def step(t, carry):
        xr, xi = carry
        rws = pl.ds(pl.multiple_of(t * batch, batch), batch)
        ar = ar_ref[...]
        ai = ai_ref[...]
        nxr = ar * xr - ai * xi + br_scr[rws, :]
        nxi = ar * xi + ai * xr + bi_scr[rws, :]
        br_scr[rws, :] = nxr
        bi_scr[rws, :] = nxi
        return nxr, nxi

    carry = (sr_scr[...], si_scr[...])
    if steps == 1:
        carry = step(0, carry)
    else:
        lo = jnp.clip(t_start - c * steps, 0, steps)
        carry = lax.fori_loop(lo, steps, step, carry)
    sr_scr[...] = carry[0]
    si_scr[...] = carry[1]
    xr_out[...] = carry[0]
    xi_out[...] = carry[1]

    ys = []
    for j in range(S5_TILES):
        ys.append(_mm(br_scr[:, j * half:(j + 1) * half], cre_ref[j], hp)
                  + _mm(bi_scr[:, j * half:(j + 1) * half], cim_ref[j], hp))
    if reorder:
        for s in range(S5_TILES):
            tm_scr[s] = ys[s]
        lax.fori_loop(0, steps, to_batch_major, 0)
        y = jnp.concatenate([bm_scr[s] for s in range(S5_TILES)], axis=1)
    else:
        y = jnp.concatenate(ys, axis=1)
    y = jax.nn.gelu(y + d_ref[...] * u)
    out = _mm(y, g1_ref[...], hp) * _sigmoid(_mm(y, g2_ref[...], hp))
    ya_ref[...] = out.reshape(ya_ref.shape)


def _s5_branch(x, batch, t_len, t_start, P, x0r, x0i, hp):
    seq = x.ndim == 3
    steps = _pick_s5_chunk(t_len) if seq else 1
    nchunks = t_len // steps
    rows = batch * steps
    if seq:
        x_spec = pl.BlockSpec((batch, steps, D_MODEL), lambda c: (0, c, 0))
        scratch_extra = [pltpu.VMEM((S5_TILES, rows, LANES), F32), pltpu.VMEM((S5_TILES, rows, LANES), F32)]
    else:
        x_spec = pl.BlockSpec((batch, D_MODEL), lambda c: (0, 0))
        scratch_extra = []
    state_spec = pl.BlockSpec((batch, S5_HALF), lambda c: (0, 0))
    consts = [P["g_mix"], P["w_u"], P["s5_bbar"], P["s5_ar"][batch], P["s5_ai"][batch], x0r, x0i,
              P["s5_cre"], P["s5_cim"], P["s5_d"], P["s5_glu1"], P["s5_glu2"]]
    return pl.pallas_call(
        functools.partial(_s5_kernel, batch=batch, steps=steps, t_start=t_start, hp=hp, reorder=seq),
        grid=(nchunks,),
        in_specs=[x_spec] + [_const_spec(a.shape) for a in consts],
        out_specs=(x_spec, state_spec, state_spec),
        out_shape=(jax.ShapeDtypeStruct(x.shape, F32),
                   jax.ShapeDtypeStruct((batch, S5_HALF), F32),
                   jax.ShapeDtypeStruct((batch, S5_HALF), F32)),
        scratch_shapes=[pltpu.VMEM((rows, S5_HALF), F32), pltpu.VMEM((rows, S5_HALF), F32),
                        pltpu.VMEM((batch, S5_HALF), F32), pltpu.VMEM((batch, S5_HALF), F32)] + scratch_extra,
        compiler_params=_cparams(("arbitrary",)),
        name="s5_branch_hp" if hp else "s5_branch",
    )(x, *consts)


N_KVEC = 5
N_VEC = N_KVEC + 1


def _rw_prep_kernel(x_ref, gm_ref, wrw_ref, sh0_ref, mu_ref, wl_ref, w0_ref, a0_ref, gup_ref,
                    kk_ref, ka_ref, rk_ref, ones_ref,
                    vec_o, g_o, bon_o, sh_o, carry_scr, *, seq, t_start, hp):
    h = _rms(x_ref[...], gm_ref[...])
    p = _mm(h, wrw_ref[...], hp)
    rows = p.shape[0]
    if seq:
        c = pl.program_id(1)

        @pl.when(c == 0)
        def _():
            carry_scr[...] = jnp.zeros_like(carry_scr)

        row = lax.broadcasted_iota(jnp.int32, p.shape, 0)
        prev = jnp.where(row == 0, carry_scr[0:1, :], pltpu.roll(p, 1, 0))
        prev = jnp.where(row + c * rows == t_start, sh0_ref[...], prev)
        carry_scr[0:1, :] = p[rows - 1:rows, :]

        @pl.when(c == pl.num_programs(1) - 1)
        def _():
            sh_o[...] = p[rows - 1:rows, :]
    else:
        prev = sh0_ref[...]
        sh_o[...] = p
    z = p + (prev - p) * mu_ref[...]
    r = z[:, :D_B]
    k = z[:, D_B:2 * D_B]
    v = z[:, 2 * D_B:3 * D_B]
    zwa = z[:, 3 * D_B:3 * D_B + LORA_W + LORA_A]
    zg = z[:, 3 * D_B + LORA_W + LORA_A:]
    lane = lax.broadcasted_iota(jnp.int32, zwa.shape, 1)
    tw = jnp.where(lane < LORA_W, jnp.tanh(zwa), zwa)
    lw = _mm(tw, wl_ref[...], hp)
    w_log = -_softplus(-(w0_ref[...] + lw[:, :D_B])) - 0.5
    decay = jnp.exp(-jnp.exp(w_log))
    a = _sigmoid(a0_ref[...] + lw[:, D_B:])
    g = _mm(_sigmoid(zg), gup_ref[...], hp)
    kk = k * kk_ref[...]
    n2 = _segsum(kk * kk, ones_ref[...], hp)
    kkn = kk * lax.rsqrt(jnp.maximum(n2, 1e-24))
    k2 = k * (1.0 + (a - 1.0) * ka_ref[...])
    rk = _segsum(r * k2 * rk_ref[...], ones_ref[...], hp)
    vec_o[0] = kkn
    vec_o[1] = decay
    vec_o[2] = -(kkn * a)
    vec_o[3] = k2
    vec_o[4] = r
    vec_o[5] = v
    g_o[...] = g
    bon_o[...] = rk * v


def _rw_prep(x, P, sh0, t_start, hp):
    seq = x.ndim == 3
    consts = [P["g_mix"], P["w_rw"]]
    consts2 = [P["rw_mu"], P["rw_wl"], P["rw_w0"], P["rw_a0"], P["rw_g_up"],
               P["rw_k_k"], P["rw_k_a"], P["rw_r_k"], P["ones_head"]]
    if seq:
        batch, t_len, _ = x.shape
        tc = _pick_tile(t_len, 544)
        grid = (batch, t_len // tc)
        row_spec = lambda w: pl.BlockSpec((None, tc, w), lambda b, c: (b, c, 0))
        vec_spec = pl.BlockSpec((N_VEC, None, tc, D_B), lambda b, c: (0, b, c, 0))
        sh_spec = pl.BlockSpec((None, 1, P_RW), lambda b, c: (b, 0, 0))
        sh0 = sh0.reshape(batch, 1, P_RW)
        vec_shape = (N_VEC, batch, t_len, D_B)
        g_shape = (batch, t_len, D_B)
        sh_shape = (batch, 1, P_RW)
        sem = ("arbitrary", "arbitrary")
    else:
        batch = x.shape[0]
        grid = (1,)
        row_spec = lambda w: pl.BlockSpec((batch, w), lambda i: (0, 0))
        vec_spec = pl.BlockSpec((N_VEC, batch, D_B), lambda i: (0, 0, 0))
        sh_spec = row_spec(P_RW)
        vec_shape = (N_VEC, batch, D_B)
        g_shape = (batch, D_B)
        sh_shape = (batch, P_RW)
        sem = ("arbitrary",)
    return pl.pallas_call(
        functools.partial(_rw_prep_kernel, seq=seq, t_start=t_start, hp=hp),
        grid=grid,
        in_specs=[row_spec(D_MODEL)] + [_const_spec(a.shape) for a in consts] + [sh_spec]
                 + [_const_spec(a.shape) for a in consts2],
        out_specs=(vec_spec, row_spec(D_B), row_spec(D_B), sh_spec),
        out_shape=(jax.ShapeDtypeStruct(vec_shape, F32), jax.ShapeDtypeStruct(g_shape, F32),
                   jax.ShapeDtypeStruct(g_shape, F32), jax.ShapeDtypeStruct(sh_shape, F32)),
        scratch_shapes=[pltpu.VMEM((SUBLANES, P_RW), F32)],
        compiler_params=_cparams(sem),
        name="rw_prep_hp" if hp else "rw_prep",
    )(x, *consts, sh0, *consts2)


RW_VH = RW_HEAD // (2 * SUBLANES)
RW_PAIRS = SUBLANES * RW_HEADS
RW_SC = RW_TC // 2


RW_QP = RW_HEAD + 4


def _pair_rows(i):
    return pl.ds(i, RW_PAIRS, stride=RW_QP)


def _pairs_to_rows(x_ref, q_scr, batch):
    for b in range(batch):
        for j in range(DB_TILES):
            tile = x_ref[b, :, j * LANES:(j + 1) * LANES].T
            pair = b * RW_HEADS + 2 * j
            q_scr[pl.ds(pair * RW_QP, RW_HEAD), :] = tile[:RW_HEAD]
            q_scr[pl.ds((pair + 1) * RW_QP, RW_HEAD), :] = tile[RW_HEAD:]


def _rw_kin_kernel(x_ref, o_ref, q_scr, *, batch):
    _pairs_to_rows(x_ref, q_scr, batch)
    for k in range(RW_HEAD):
        m = q_scr[_pair_rows(k), :]
        o_ref[k] = jnp.concatenate([m, m], axis=0).T


def _rw_vin_kernel(x_ref, o_ref, q_scr, *, batch):
    _pairs_to_rows(x_ref, q_scr, batch)
    for vh in range(RW_VH):
        for vs in range(SUBLANES):
            v0 = (vh * SUBLANES + vs) * 2
            pair_tile = jnp.concatenate([q_scr[_pair_rows(v0), :], q_scr[_pair_rows(v0 + 1), :]], axis=0)
            o_ref[vh, pl.ds(vs, RW_TC, stride=SUBLANES), :] = pair_tile.T


def _rw_unlayout_kernel(o3_ref, o_ref, q_scr, *, batch):
    for vh in range(RW_VH):
        for vs in range(SUBLANES):
            v0 = (vh * SUBLANES + vs) * 2
            zt = o3_ref[vh, pl.ds(vs, RW_TC, stride=SUBLANES), :].T
            q_scr[_pair_rows(v0), :] = zt[:RW_PAIRS]
            q_scr[_pair_rows(v0 + 1), :] = zt[RW_PAIRS:]
    for b in range(batch):
        for j in range(DB_TILES):
            pair = b * RW_HEADS + 2 * j
            tile = jnp.concatenate([q_scr[pl.ds(pair * RW_QP, RW_HEAD), :],
                                    q_scr[pl.ds((pair + 1) * RW_QP, RW_HEAD), :]], axis=0)
            o_ref[b, :, j * LANES:(j + 1) * LANES] = tile.T


def _rw_scan_kernel(kin_ref, vin_ref, s0_ref, o_ref, sfin_ref, s_scr, *, t_start):
    c = pl.program_id(0)

    @pl.when(c == 0)
    def _():
        s_scr[...] = s0_ref[...]

    def krow(t, vec, k):
        return jnp.broadcast_to(kin_ref[vec * RW_HEAD + k, pl.ds(t, 1), :], (SUBLANES, LANES))

    def acc_add(acc, vh, k, x):
        prev = acc[vh][k % 2]
        acc[vh][k % 2] = x if prev is None else prev + x

    def first_sa(t):
        acc = [[None, None] for _ in range(RW_VH)]
        for k in range(RW_HEAD):
            kkb = krow(t, 0, k)
            for vh in range(RW_VH):
                acc_add(acc, vh, k, s_scr[vh, k] * kkb)
        return tuple(a[0] + a[1] for a in acc)

    def step(t, sa):
        tn = jnp.minimum(t + 1, RW_SC - 1)
        tv = pl.multiple_of(t * SUBLANES, SUBLANES)
        vv = [vin_ref[vh, pl.ds(tv, SUBLANES), :] for vh in range(RW_VH)]
        oacc = [[None, None] for _ in range(RW_VH)]
        nacc = [[None, None] for _ in range(RW_VH)]
        for k in range(RW_HEAD):
            wb = krow(t, 1, k)
            nbb = krow(t, 2, k)
            kb = krow(t, 3, k)
            rb = krow(t, 4, k)
            kkn = krow(tn, 0, k)
            for vh in range(RW_VH):
                s = s_scr[vh, k] * wb + sa[vh] * nbb + vv[vh] * kb
                s_scr[vh, k] = s
                acc_add(oacc, vh, k, s * rb)
                acc_add(nacc, vh, k, s * kkn)
        for vh in range(RW_VH):
            o_ref[vh, pl.ds(tv, SUBLANES), :] = oacc[vh][0] + oacc[vh][1]
        return tuple(a[0] + a[1] for a in nacc)

    lo = jnp.clip(t_start - c * RW_SC, 0, RW_SC)

    @pl.when(lo > 0)
    def _():
        o_ref[...] = jnp.zeros_like(o_ref)

    lax.fori_loop(lo, RW_SC, step, first_sa(jnp.minimum(lo, RW_SC - 1)))
    sfin_ref[...] = s_scr[...]


def _rw_recurrence_long(vecs, s0, t_start):
    _, batch, t_len, _ = vecs.shape
    assert batch == SUBLANES and t_len % RW_TC == 0
    nchunks = t_len // RW_TC
    q_scr = pltpu.VMEM((RW_PAIRS * RW_QP, LANES), F32)
    kin = pl.pallas_call(
        functools.partial(_rw_kin_kernel, batch=batch),
        grid=(nchunks, N_KVEC),
        in_specs=[pl.BlockSpec((None, batch, RW_TC, D_B), lambda c, i: (i, 0, c, 0))],
        out_specs=pl.BlockSpec((None, RW_HEAD, RW_TC, LANES), lambda c, i: (c, i, 0, 0)),
        out_shape=jax.ShapeDtypeStruct((nchunks, N_KVEC * RW_HEAD, RW_TC, LANES), F32),
        scratch_shapes=[q_scr],
        compiler_params=_cparams(("arbitrary", "arbitrary")),
        name="rw_kin",
    )(vecs)
    vin = pl.pallas_call(
        functools.partial(_rw_vin_kernel, batch=batch),
        grid=(nchunks,),
        in_specs=[pl.BlockSpec((None, batch, RW_TC, D_B), lambda c: (N_KVEC, 0, c, 0))],
        out_specs=pl.BlockSpec((None, RW_VH, RW_TC * SUBLANES, LANES), lambda c: (c, 0, 0, 0)),
        out_shape=jax.ShapeDtypeStruct((nchunks, RW_VH, RW_TC * SUBLANES, LANES), F32),
        scratch_shapes=[q_scr],
        compiler_params=_cparams(("arbitrary",)),
        name="rw_vin",
    )(vecs)
    sshape = (RW_VH, RW_HEAD, SUBLANES, LANES)
    s = s0.reshape(batch, RW_HEADS, RW_VH, SUBLANES, 2, RW_HEAD)
    s = jnp.transpose(s, (2, 5, 3, 4, 0, 1)).reshape(sshape)
    state_spec = pl.BlockSpec(sshape, lambda c: (0, 0, 0, 0))
    halves = RW_TC // RW_SC
    vblk = (None, RW_VH, RW_SC * SUBLANES, LANES)
    vmap = lambda c: (c // halves, 0, c % halves, 0)
    o3, sfin = pl.pallas_call(
        functools.partial(_rw_scan_kernel, t_start=t_start),
        grid=(nchunks * halves,),
        in_specs=[pl.BlockSpec((None, N_KVEC * RW_HEAD, RW_SC, LANES), vmap),
                  pl.BlockSpec(vblk, vmap), state_spec],
        out_specs=(pl.BlockSpec(vblk, vmap), state_spec),
        out_shape=(jax.ShapeDtypeStruct((nchunks, RW_VH, RW_TC * SUBLANES, LANES), F32),
                   jax.ShapeDtypeStruct(sshape, F32)),
        scratch_shapes=[pltpu.VMEM(sshape, F32)],
        compiler_params=_cparams(("arbitrary",)),
        name="rw_scan",
    )(kin, vin, s)
    o = pl.pallas_call(
        functools.partial(_rw_unlayout_kernel, batch=batch),
        grid=(nchunks,),
        in_specs=[pl.BlockSpec((None, RW_VH, RW_TC * SUBLANES, LANES), lambda c: (c, 0, 0, 0))],
        out_specs=pl.BlockSpec((batch, RW_TC, D_B), lambda c: (0, c, 0)),
        out_shape=jax.ShapeDtypeStruct((batch, t_len, D_B), F32),
        scratch_shapes=[q_scr],
        compiler_params=_cparams(("arbitrary",)),
        name="rw_unlayout",
    )(o3)
    sfin = sfin.reshape(RW_VH, RW_HEAD, SUBLANES, 2, batch, RW_HEADS)
    sfin = jnp.transpose(sfin, (4, 5, 0, 2, 3, 1)).reshape(batch, RW_HEADS, RW_HEAD, RW_HEAD)
    return o, sfin


STEP_PAIRS = 64
STEP_VROWS = RW_HEAD // 2


def _split3(x):
    hi = x.astype(BF16)
    r1 = x - hi.astype(F32)
    mid = r1.astype(BF16)
    return hi, mid, (r1 - mid.astype(F32)).astype(BF16)


def _rw_step_kernel(s_ref, km_ref, v_ref, sel_ref, vmask_ref, vexp_ref, ones_ref, s_o, o_o):
    sel = sel_ref[...]

    def rows_of(x):
        return sum(jnp.dot(sel, p, preferred_element_type=F32) for p in _split3(x))

    kx = rows_of(km_ref[...])
    kk, w, nb, k, r = [kx[:, i * LANES:(i + 1) * LANES] for i in range(N_KVEC)]
    vx = _segsum(rows_of(v_ref[...]) * vmask_ref[...], vexp_ref[...], True)
    s = s_ref[...]
    ones = ones_ref[...]
    sa = _segsum(s * kk, ones, True)
    s = s * w + sa * nb + vx * k
    s_o[...] = s
    o_o[...] = _segsum(s * r, ones, True)


def _rw_recurrence_step(vecs, s0, ones_half):
    batch = vecs.shape[1]
    pairs = batch * RW_HEADS
    rows = pairs * STEP_VROWS
    tr = STEP_PAIRS * STEP_VROWS
    assert pairs % STEP_PAIRS == 0
    kp = vecs[:N_KVEC].reshape(N_KVEC, pairs, 1, RW_HEAD)
    km = jnp.transpose(jnp.broadcast_to(kp, (N_KVEC, pairs, 2, RW_HEAD)), (1, 0, 2, 3)).reshape(pairs, N_KVEC * LANES)
    vp = vecs[N_KVEC].reshape(pairs, RW_HEAD)
    ridx = jnp.arange(tr, dtype=jnp.int32)
    sel = (ridx[:, None] // STEP_VROWS == jnp.arange(STEP_PAIRS, dtype=jnp.int32)[None, :]).astype(BF16)
    vidx = jnp.arange(RW_HEAD, dtype=jnp.int32)
    vmask = (vidx[None, :] // 2 == ridx[:, None] % STEP_VROWS).astype(F32)
    vexp = (vidx[:, None] % 2 == jnp.arange(LANES, dtype=jnp.int32)[None, :] // RW_HEAD).astype(BF16)
    spec = pl.BlockSpec((tr, LANES), lambda i: (i, 0))
    big = jax.ShapeDtypeStruct((rows, LANES), F32)
    consts = [sel, vmask, vexp, ones_half]
    s_new, o = pl.pallas_call(
        _rw_step_kernel,
        grid=(pairs // STEP_PAIRS,),
        in_specs=[spec, pl.BlockSpec((STEP_PAIRS, N_KVEC * LANES), lambda i: (i, 0)),
                  pl.BlockSpec((STEP_PAIRS, RW_HEAD), lambda i: (i, 0))]
                 + [_const_spec(a.shape) for a in consts],
        out_specs=(spec, spec),
        out_shape=(big, big),
        compiler_params=_cparams(("arbitrary",)),
        name="rw_step",
    )(s0.reshape(rows, LANES), km, vp, *consts)
    o = o.reshape(rows, 2, RW_HEAD)[:, :, 0].reshape(batch, D_B)
    return o, s_new.reshape(batch, RW_HEADS, RW_HEAD, RW_HEAD)


def _mix_kernel(*refs, hp, moe):
    (x_ref, ya_ref, o_ref, bon_ref, g_ref, gm_ref, wg_ref, gng_ref, gnb_ref, ones_ref,
     wbo_ref, wout_ref, gf_ref) = refs[:13]
    if moe:
        wr_ref, x1_o, h2_o, route_o = refs[13:]
    else:
        x1_o, h2_o = refs[13:]
    x = x_ref[...]
    h = _rms(x, gm_ref[...])
    gates = _mm(h, wg_ref[...], hp)
    o = o_ref[...]
    ones = ones_ref[...]
    inv_n = 1.0 / RW_HEAD
    mean = _segsum(o, ones, hp) * inv_n
    dlt = o - mean
    var = _segsum(dlt * dlt, ones, hp) * inv_n
    on = dlt * lax.rsqrt(var + GN_EPS) * gng_ref[...] + gnb_ref[...]
    yb = _mm((on + bon_ref[...]) * g_ref[...], wbo_ref[...], hp)
    m = _sigmoid(gates[:, :D_MODEL]) * ya_ref[...] + _sigmoid(gates[:, D_MODEL:]) * yb
    x1 = x + _mm(m, wout_ref[...], hp)
    h2 = _rms(x1, gf_ref[...])
    x1_o[...] = x1
    h2_o[...] = h2.astype(h2_o.dtype)
    if moe:
        logits = _mm(h2, wr_ref[...], hp)
        lane = lax.broadcasted_iota(jnp.int32, logits.shape, 1).astype(F32)
        neg = jnp.float32(-jnp.inf)
        lg = jnp.where(lane < N_EXPERTS, logits, neg)
        m1 = jnp.max(lg, axis=1, keepdims=True)
        i1 = jnp.min(jnp.where(lg == m1, lane, float(LANES)), axis=1, keepdims=True)
        lg2 = jnp.where(lane == i1, neg, lg)
        m2 = jnp.max(lg2, axis=1, keepdims=True)
        i2 = jnp.min(jnp.where(lg2 == m2, lane, float(LANES)), axis=1, keepdims=True)
        e = jnp.exp(m2 - m1)
        g1 = 1.0 / (1.0 + e)
        g2 = e / (1.0 + e)
        route_o[...] = jnp.where(lane == 0.0, i1, jnp.where(lane == 1.0, i2,
                                 jnp.where(lane == 2.0, g1, jnp.where(lane == 3.0, g2, 0.0))))


def _mix(x, ya, o, bon, g, P, hp, moe):
    rows = x.shape[0]
    tm = _pick_tile(rows, 344)
    row_spec = lambda w: pl.BlockSpec((tm, w), lambda i: (i, 0))
    consts = [P["g_mix"], P["w_gate"], P["rw_gn_g"], P["rw_gn_b"], P["ones_head"],
              P["rw_w_bo"], P["w_out"], P["g_ffn"]]
    if moe:
        consts.append(P["w_router"])
    h2_dtype = F32 if (hp or moe) else BF16
    out_specs = [row_spec(D_MODEL), row_spec(D_MODEL)]
    out_shape = [jax.ShapeDtypeStruct((rows, D_MODEL), F32), jax.ShapeDtypeStruct((rows, D_MODEL), h2_dtype)]
    if moe:
        out_specs.append(row_spec(LANES))
        out_shape.append(jax.ShapeDtypeStruct((rows, LANES), F32))
    return pl.pallas_call(
        functools.partial(_mix_kernel, hp=hp, moe=moe),
        grid=(rows // tm,),
        in_specs=[row_spec(D_MODEL), row_spec(D_MODEL), row_spec(D_B), row_spec(D_B), row_spec(D_B)]
                 + [_const_spec(a.shape) for a in consts],
        out_specs=tuple(out_specs),
        out_shape=tuple(out_shape),
        compiler_params=_cparams(("arbitrary",)),
        name=("mix_moe" if moe else "mix") + ("_hp" if hp else ""),
    )(x, ya, o, bon, g, *consts)


def _ffn_kernel(h_ref, x1_ref, w1_ref, w3_ref, w2_ref, o_ref, *, hp):
    h = h_ref[...]
    a = _mm(h, w1_ref[...], hp)
    b = _mm(h, w3_ref[...], hp)
    o_ref[...] = x1_ref[...] + _mm(a * _sigmoid(a) * b, w2_ref[...], hp)


def _ffn(h2, x1, w1, w3, w2, tm, hp):
    rows = h2.shape[0]
    assert rows % tm == 0
    row_spec = pl.BlockSpec((tm, D_MODEL), lambda i: (i, 0))
    return pl.pallas_call(
        functools.partial(_ffn_kernel, hp=hp),
        grid=(rows // tm,),
        in_specs=[row_spec, row_spec, _const_spec(w1.shape), _const_spec(w3.shape), _const_spec(w2.shape)],
        out_specs=row_spec,
        out_shape=jax.ShapeDtypeStruct((rows, D_MODEL), F32),
        compiler_params=_cparams(("arbitrary",)),
        name="ffn_hp" if hp else "ffn",
    )(h2, x1, w1, w3, w2)


MOE_TM = 512
MOE_TF = 1792


def _moe_kernel(te_ref, nv_ref, x_ref, w1_ref, w3_ref, w2_ref, o_ref, acc):
    i = pl.program_id(0)
    f = pl.program_id(1)

    @pl.when(f == 0)
    def _():
        acc[...] = jnp.zeros_like(acc)

    @pl.when(i < nv_ref[0])
    def _():
        x = x_ref[...].astype(BF16)
        a = jnp.dot(x, w1_ref[...], preferred_element_type=F32)
        b = jnp.dot(x, w3_ref[...], preferred_element_type=F32)
        acc[...] += jnp.dot((a * _sigmoid(a) * b).astype(BF16), w2_ref[...], preferred_element_type=F32)

    @pl.when(f == pl.num_programs(1) - 1)
    def _():
        o_ref[...] = acc[...]


def _moe_experts(xs, tile_expert, n_valid, w1, w3, w2):
    rows = xs.shape[0]
    ntiles = rows // MOE_TM
    nf = D_EXPERT // MOE_TF
    assert ntiles * MOE_TM == rows and nf * MOE_TF == D_EXPERT

    def fblk(i, f, nv):
        return jnp.where(i < nv[0], f, nf - 1)

    grid_spec = pltpu.PrefetchScalarGridSpec(
        num_scalar_prefetch=2,
        grid=(ntiles, nf),
        in_specs=[pl.BlockSpec((MOE_TM, D_MODEL), lambda i, f, te, nv: (i, 0)),
                  pl.BlockSpec((None, D_MODEL, MOE_TF), lambda i, f, te, nv: (te[i], 0, fblk(i, f, nv))),
                  pl.BlockSpec((None, D_MODEL, MOE_TF), lambda i, f, te, nv: (te[i], 0, fblk(i, f, nv))),
                  pl.BlockSpec((None, MOE_TF, D_MODEL), lambda i, f, te, nv: (te[i], fblk(i, f, nv), 0))],
        out_specs=pl.BlockSpec((MOE_TM, D_MODEL), lambda i, f, te, nv: (i, 0)),
        scratch_shapes=[pltpu.VMEM((MOE_TM, D_MODEL), F32)])
    return pl.pallas_call(
        _moe_kernel,
        grid_spec=grid_spec,
        out_shape=jax.ShapeDtypeStruct((rows, D_MODEL), F32),
        compiler_params=_cparams(("arbitrary", "arbitrary")),
        name="moe_experts",
    )(tile_expert, n_valid, xs, w1, w3, w2)


def _moe_plan(expert_idx, src_row):
    n_pairs = expert_idx.shape[0] * 2
    flat_e = expert_idx.reshape(n_pairs)
    onehot = (flat_e[:, None] == jnp.arange(N_EXPERTS, dtype=jnp.int32)[None, :]).astype(jnp.int32)
    csum = jnp.cumsum(onehot, axis=0)
    rank = jnp.sum(csum * onehot, axis=1) - 1
    counts = csum[-1]
    padded = ((counts + MOE_TM - 1) // MOE_TM) * MOE_TM
    pend = jnp.cumsum(padded)
    pstart = pend - padded
    dest = pstart[flat_e] + rank
    ntiles = (n_pairs + N_EXPERTS * (MOE_TM - 1)) // MOE_TM + 1
    rows = ntiles * MOE_TM
    tile_start = jnp.arange(ntiles, dtype=jnp.int32) * MOE_TM
    n_valid = (pend[-1] // MOE_TM).astype(jnp.int32)
    te = jnp.sum((pend[None, :] <= tile_start[:, None]).astype(jnp.int32), axis=1)
    te = jnp.minimum(te, N_EXPERTS - 1)
    order = jnp.argsort(flat_e, stable=True).astype(jnp.int32)
    gstart = jnp.cumsum(counts) - counts
    slot = jnp.arange(rows, dtype=jnp.int32)
    e_slot = jnp.repeat(te, MOE_TM)
    off = slot - pstart[e_slot]
    filled = (off < counts[e_slot]) & (slot < pend[-1])
    pair = order[jnp.clip(gstart[e_slot] + off, 0, n_pairs - 1)]
    src = jnp.where(filled, src_row[pair // 2], 0).astype(jnp.int32)
    last_e = te[jnp.maximum(n_valid - 1, 0)]
    te = jnp.where(jnp.arange(ntiles) < n_valid, te, last_e).astype(jnp.int32)
    return src, dest.reshape(-1, 2), te, n_valid.reshape(1)


def _combine_kernel(x1_ref, y1_ref, y2_ref, route_ref, gf_ref, o_ref):
    route = route_ref[...]
    x2 = x1_ref[...] + route[:, 2:3] * y1_ref[...] + route[:, 3:4] * y2_ref[...]
    o_ref[...] = _rms(x2, gf_ref[...])


def _combine_norm(x1, x1_spec, flat_blk0, grid, out_shape, out_spec, y1, y2, route, g_final, tm):
    nd = len(grid)

    def flat_map(*idx):
        lin = idx[0]
        for k in range(1, nd):
            lin = lin * grid[k] + idx[k]
        return (flat_blk0 + lin, 0)

    return pl.pallas_call(
        _combine_kernel,
        grid=grid,
        in_specs=[x1_spec, pl.BlockSpec((tm, D_MODEL), flat_map), pl.BlockSpec((tm, D_MODEL), flat_map),
                  pl.BlockSpec((tm, LANES), flat_map), _const_spec(g_final.shape)],
        out_specs=out_spec,
        out_shape=out_shape,
        compiler_params=_cparams(("arbitrary",) * nd),
        name="combine_norm",
    )(x1, y1, y2, route, g_final)


def _layer_params(l, W, hp, batches):
    wdt = F32 if hp else BF16
    row = lambda a: a.reshape(1, -1).astype(F32)
    w_in = W["w_in"][l]
    P = {
        "g_mix": row(W["norm_mix"][l]),
        "g_ffn": row(W["norm_ffn"][l]),
        "w_u": w_in[:, :D_A].astype(wdt),
        "w_rw": w_in[:, D_A:D_A + P_RW].astype(wdt),
        "w_gate": w_in[:, D_A + P_RW:].astype(wdt),
        "s5_d": row(W["s5_d"][l]),
        "s5_glu1": W["s5_glu1"][l].astype(wdt),
        "s5_glu2": W["s5_glu2"][l].astype(wdt),
        "rw_mu": row(W["rw_mu"][l]),
        "rw_w0": row(W["rw_w0"][l]),
        "rw_a0": row(W["rw_a0"][l]),
        "rw_g_up": W["rw_g_up"][l].astype(wdt),
        "rw_k_k": row(W["rw_k_k"][l]),
        "rw_k_a": row(W["rw_k_a"][l]),
        "rw_r_k": row(W["rw_r_k"][l]),
        "rw_gn_g": row(W["rw_gn_g"][l]),
        "rw_gn_b": row(W["rw_gn_b"][l]),
        "rw_w_bo": W["rw_w_bo"][l].astype(wdt),
        "w_out": W["w_out"][l].astype(wdt),
    }
    wl = jnp.zeros((LORA_W + LORA_A, 2 * D_B), F32)
    wl = wl.at[:LORA_W, :D_B].set(W["rw_w_up"][l]).at[LORA_W:, D_B:].set(W["rw_a_up"][l])
    P["rw_wl"] = wl.astype(wdt)
    P["ones_head"] = jnp.kron(jnp.eye(RW_HEADS, dtype=F32), jnp.ones((RW_HEAD, RW_HEAD), F32)).astype(BF16)

    ab_re, ab_im, bb_re, bb_im = _s5_discretize(W["s5_lam_re"][l], W["s5_lam_im"][l], W["s5_log_dt"][l],
                                                W["s5_b_re"][l], W["s5_b_im"][l])
    P["s5_ar"] = {b: jnp.broadcast_to(ab_re.reshape(1, S5_HALF), (b, S5_HALF)) for b in batches}
    P["s5_ai"] = {b: jnp.broadcast_to(ab_im.reshape(1, S5_HALF), (b, S5_HALF)) for b in batches}
    gpt = S5_GROUPS // S5_TILES
    eye = jnp.eye(gpt, dtype=F32)
    bb = jnp.stack([bb_re, bb_im]).reshape(2, S5_GROUP, S5_TILES, gpt, S5_STATE)
    bb = jnp.transpose(bb, (2, 1, 0, 3, 4))
    bbar = bb[:, None] * eye[None, :, None, None, :, None]
    P["s5_bbar"] = bbar.reshape(S5_TILES, LANES, 2 * gpt * S5_STATE).astype(wdt)
    cs = jnp.stack([W["s5_c_re"][l], -W["s5_c_im"][l]]).reshape(2, S5_TILES, gpt, S5_GROUP, S5_STATE)
    cs = jnp.transpose(cs, (1, 0, 2, 4, 3))
    cm = cs[:, :, :, :, None, :] * eye[None, None, :, None, :, None]
    cm = cm.reshape(S5_TILES, 2, gpt * S5_STATE, LANES).astype(wdt)
    P["s5_cre"] = cm[:, 0]
    P["s5_cim"] = cm[:, 1]
    return P


def _mixer_seq(x, t_start, s5r0, s5i0, rw_s0, rw_sh0, P, moe):
    batch, t_len, _ = x.shape
    ya, xr, xi = _s5_branch(x, batch, t_len, t_start, P, s5r0, s5i0, False)
    vecs, g, bon, sh = _rw_prep(x, P, rw_sh0, t_start, False)
    o, s_fin = _rw_recurrence_long(vecs, rw_s0, t_start)
    flat = lambda a: a.reshape(batch * t_len, a.shape[-1])
    outs = _mix(flat(x), flat(ya), flat(o), flat(bon), flat(g), P, False, moe)
    return outs, (xr, xi, s_fin, sh.reshape(batch, P_RW))


def _mixer_step(x, s5r0, s5i0, rw_s0, rw_sh0, P, moe, ones_half):
    batch = x.shape[0]
    ya, xr, xi = _s5_branch(x, batch, 1, 0, P, s5r0, s5i0, True)
    vecs, g, bon, sh = _rw_prep(x, P, rw_sh0, 0, True)
    o, s_fin = _rw_recurrence_step(vecs, rw_s0, ones_half)
    outs = _mix(x, ya, o, bon, g, P, True, moe)
    return outs, (xr, xi, s_fin, sh)


def kernel(x_prompt, x_sample, state_s5_re, state_s5_im, state_rwkv, state_shift, meta_tokens, norm_mix, w_in, s5_lam_re, s5_lam_im, s5_log_dt, s5_b_re, s5_b_im, s5_c_re, s5_c_im, s5_d, s5_glu1, s5_glu2, rw_mu, rw_w0, rw_w_up, rw_a0, rw_a_up, rw_g_up, rw_k_k, rw_k_a, rw_r_k, rw_gn_g, rw_gn_b, rw_w_bo, w_out, norm_ffn, ffn_w1, ffn_w3, ffn_w2, moe_router, moe_w1, moe_w3, moe_w2, norm_final):
    W = dict(norm_mix=norm_mix, w_in=w_in, s5_lam_re=s5_lam_re, s5_lam_im=s5_lam_im,
             s5_log_dt=s5_log_dt, s5_b_re=s5_b_re, s5_b_im=s5_b_im, s5_c_re=s5_c_re,
             s5_c_im=s5_c_im, s5_d=s5_d, s5_glu1=s5_glu1, s5_glu2=s5_glu2, rw_mu=rw_mu,
             rw_w0=rw_w0, rw_w_up=rw_w_up, rw_a0=rw_a0, rw_a_up=rw_a_up, rw_g_up=rw_g_up,
             rw_k_k=rw_k_k, rw_k_a=rw_k_a, rw_r_k=rw_r_k, rw_gn_g=rw_gn_g, rw_gn_b=rw_gn_b,
             rw_w_bo=rw_w_bo, w_out=w_out, norm_ffn=norm_ffn)
    bp, seq, _ = x_prompt.shape
    bs = x_sample.shape[0]
    assert x_sample.shape[1] == 1 and seq % RW_TC == 0 and bp == SUBLANES
    tp = N_META + seq
    t_pad = -(-tp // RW_TC) * RW_TC
    t_start = t_pad - tp
    out0 = t_pad - seq

    meta = jnp.broadcast_to(meta_tokens.astype(F32)[None], (bp, N_META, D_MODEL))
    xp = jnp.concatenate([jnp.zeros((bp, t_start, D_MODEL), F32), meta, x_prompt], axis=1)
    xs = x_sample.reshape(bs, D_MODEL)
    ones_half = jnp.kron(jnp.eye(2, dtype=F32), jnp.ones((RW_HEAD, RW_HEAD), F32)).astype(BF16)
    g_final = norm_final.reshape(1, D_MODEL).astype(F32)

    zero_s5 = jnp.zeros((bp, S5_HALF), F32)
    zero_rw = jnp.zeros((bp, RW_HEADS, RW_HEAD, RW_HEAD), F32)
    zero_sh = jnp.zeros((bp, P_RW), F32)

    p_states, s_states = [], []
    for l in range(DEPTH):
        moe = (l % 2 == 1)
        j = l // 2
        Pl = _layer_params(l, W, False, (bp,))
        Ph = _layer_params(l, W, True, (bs,))
        if moe:
            wr = jnp.zeros((D_MODEL, LANES), F32).at[:, :N_EXPERTS].set(moe_router[j])
            Pl["w_router"] = wr.astype(BF16)
            Ph["w_router"] = wr
        outs_p, st_p = _mixer_seq(xp, t_start, zero_s5, zero_s5, zero_rw, zero_sh, Pl, moe)
        outs_s, st_s = _mixer_step(xs, state_s5_re[l].reshape(bs, S5_HALF), state_s5_im[l].reshape(bs, S5_HALF),
                                   state_rwkv[l], state_shift[l], Ph, moe, ones_half)
        p_states.append(st_p)
        s_states.append(st_s)
        if not moe:
            rows = bp * t_pad
            xp = _ffn(outs_p[1], outs_p[0], ffn_w1[j].astype(BF16), ffn_w3[j].astype(BF16),
                      ffn_w2[j].astype(BF16), _pick_tile(rows, 544), False).reshape(bp, t_pad, D_MODEL)
            xs = _ffn(outs_s[1], outs_s[0], ffn_w1[j], ffn_w3[j], ffn_w2[j], bs, True)
        else:
            assert l == DEPTH - 1
            n_p = bp * seq
            h2 = jnp.concatenate([outs_p[1], outs_s[1]], axis=0)
            route_p = outs_p[2].reshape(bp, t_pad, LANES)[:, out0:].reshape(n_p, LANES)
            route = jnp.concatenate([route_p, outs_s[2]], axis=0)
            tok = jnp.arange(n_p, dtype=jnp.int32)
            src_row = jnp.concatenate([(tok // seq) * t_pad + out0 + tok % seq,
                                       bp * t_pad + jnp.arange(bs, dtype=jnp.int32)])
            src, dest, te, n_valid = _moe_plan(route[:, :2].astype(jnp.int32), src_row)
            take = functools.partial(jnp.take, axis=0, mode="clip")
            y_sorted = _moe_experts(take(h2, src), te, n_valid,
                                    moe_w1[j].astype(BF16), moe_w3[j].astype(BF16), moe_w2[j].astype(BF16))
            y1 = take(y_sorted, dest[:, 0])
            y2 = take(y_sorted, dest[:, 1])
            tm = RW_TC
            nblk = seq // tm
            y_prompt = _combine_norm(
                outs_p[0].reshape(bp, t_pad, D_MODEL),
                pl.BlockSpec((None, tm, D_MODEL), lambda b, s: (b, s + out0 // tm, 0)), 0, (bp, nblk),
                jax.ShapeDtypeStruct((bp, seq, D_MODEL), F32),
                pl.BlockSpec((None, tm, D_MODEL), lambda b, s: (b, s, 0)), y1, y2, route, g_final, tm)
            assert n_p % bs == 0
            y_sample = _combine_norm(
                outs_s[0], pl.BlockSpec((bs, D_MODEL), lambda i: (0, 0)), n_p // bs, (1,),
                jax.ShapeDtypeStruct((bs, D_MODEL), F32),
                pl.BlockSpec((bs, D_MODEL), lambda i: (0, 0)), y1, y2, route, g_final, bs)

    y_sample = y_sample.reshape(bs, 1, D_MODEL)

    def stack(states, b):
        re = jnp.stack([s[0].reshape(b, S5_GROUPS, S5_STATE) for s in states])
        im = jnp.stack([s[1].reshape(b, S5_GROUPS, S5_STATE) for s in states])
        rw = jnp.stack([s[2] for s in states])
        sh = jnp.stack([s[3] for s in states])
        return re, im, rw, sh

    p_re, p_im, p_rw, p_sh = stack(p_states, bp)
    s_re, s_im, s_rw, s_sh = stack(s_states, bs)
    return (y_prompt, y_sample, p_re, p_im, p_rw, p_sh, s_re, s_im, s_rw, s_sh)
```

```python
import functools

import jax
import jax.numpy as jnp
from jax import lax
from jax.experimental import pallas as pl
from jax.experimental.pallas import tpu as pltpu

F32 = jnp.float32
BF16 = jnp.bfloat16
HIGHEST = lax.Precision.HIGHEST

D_MODEL = 1024
DEPTH = 2
N_META = 16
D_A = 512
S5_GROUP = 16
S5_GROUPS = 32
S5_STATE = 64
S5_TILES = 4
S5_HALF = S5_GROUPS * S5_STATE
D_B = 512
RW_HEAD = 64
RW_HEADS = 8
LORA_W = 64
LORA_A = 64
LORA_G = 128
GN_EPS = 64e-5
P_RW = 3 * D_B + LORA_W + LORA_A + LORA_G
D_FF = 2816
N_EXPERTS = 8
D_EXPERT = 3584
RMS_EPS = 1e-6
LANES = 128
SUBLANES = 8
VMEM_LIMIT_MB = 56
RW_TC = LANES
DB_TILES = D_B // LANES


def _pick_tile(n, pref, mult=SUBLANES):
    best = None
    for d in range(mult, min(n, pref) + 1, mult):
        if n % d == 0:
            best = d
    return best if best is not None else n


def _pick_s5_chunk(t_pad, pref=136):
    cands = [d for d in range(SUBLANES, min(t_pad, pref) + 1, SUBLANES) if t_pad % d == 0]
    odd = [d for d in cands if (d // SUBLANES) % 2 == 1]
    return max(odd) if odd else max(cands)


def _cparams(sem):
    return pltpu.CompilerParams(dimension_semantics=sem,
                                vmem_limit_bytes=VMEM_LIMIT_MB * 1024 * 1024)


def _const_spec(shape):
    nd = len(shape)
    return pl.BlockSpec(shape, lambda *_: (0,) * nd, pipeline_mode=pl.Buffered(1))


def _mm(a, b, hp):
    if hp:
        return jnp.dot(a.astype(F32), b, precision=HIGHEST, preferred_element_type=F32)
    return jnp.dot(a.astype(BF16), b, preferred_element_type=F32)


def _segsum(x, ones, hp):
    hi = x.astype(BF16)
    r1 = x - hi.astype(F32)
    mid = r1.astype(BF16)
    out = (jnp.dot(hi, ones, preferred_element_type=F32)
           + jnp.dot(mid, ones, preferred_element_type=F32))
    if hp:
        lo = (r1 - mid.astype(F32)).astype(BF16)
        out = out + jnp.dot(lo, ones, preferred_element_type=F32)
    return out


def _sigmoid(x):
    return 1.0 / (1.0 + jnp.exp(-x))


def _softplus(x):
    return jnp.maximum(x, 0.0) + jnp.log1p(jnp.exp(-jnp.abs(x)))


def _rms(x, g):
    return x * lax.rsqrt(jnp.mean(x * x, axis=-1, keepdims=True) + RMS_EPS) * g


def _s5_disc_kernel(lr_ref, li_ref, ldt_ref, btr_ref, bti_ref, abr_o, abi_o, bbr_o, bbi_o):
    lr = lr_ref[...]
    li = li_ref[...]
    dt = jnp.exp(ldt_ref[...])
    mag = jnp.exp(lr * dt)
    ab_re = mag * jnp.cos(li * dt)
    ab_im = mag * jnp.sin(li * dt)
    den = lr * lr + li * li
    q_re = ((ab_re - 1.0) * lr + ab_im * li) / den
    q_im = (ab_im * lr - (ab_re - 1.0) * li) / den
    abr_o[...] = ab_re
    abi_o[...] = ab_im
    for c in range(S5_GROUP):
        b_re = btr_ref[c]
        b_im = bti_ref[c]
        bbr_o[c] = q_re * b_re - q_im * b_im
        bbi_o[c] = q_re * b_im + q_im * b_re


def _s5_discretize(lam_re, lam_im, log_dt, b_re, b_im):
    gp = jax.ShapeDtypeStruct((S5_GROUPS, S5_STATE), F32)
    cgp = jax.ShapeDtypeStruct((S5_GROUP, S5_GROUPS, S5_STATE), F32)
    return pl.pallas_call(_s5_disc_kernel, out_shape=(gp, gp, cgp, cgp), name="s5_disc")(
        lam_re, lam_im, log_dt[:, None],
        jnp.transpose(b_re, (2, 0, 1)), jnp.transpose(b_im, (2, 0, 1)))


def _s5_kernel(x_ref, gm_ref, wu_ref, bbar_ref, ar_ref, ai_ref, x0r_ref, x0i_ref,
               cre_ref, cim_ref, d_ref, g1_ref, g2_ref,
               ya_ref, xr_out, xi_out, br_scr, bi_scr, sr_scr, si_scr, *rest,
               batch, steps, t_start, hp, reorder):
    c = pl.program_id(0)

    @pl.when(c == 0)
    def _():
        sr_scr[...] = x0r_ref[...]
        si_scr[...] = x0i_ref[...]

    rows = batch * steps
    x = x_ref[...]
    if reorder:
        bm_scr, tm_scr = rest
        x = x.reshape(rows, D_MODEL)
    h = _rms(x, gm_ref[...])
    u = _mm(h, wu_ref[...], hp)

    def to_time_major(t, carry):
        for s in range(S5_TILES):
            tm_scr[s, pl.ds(pl.multiple_of(t * batch, batch), batch), :] = bm_scr[s, pl.ds(t, batch, stride=steps), :]
        return carry

    def to_batch_major(t, carry):
        for s in range(S5_TILES):
            bm_scr[s, pl.ds(t, batch, stride=steps), :] = tm_scr[s, pl.ds(pl.multiple_of(t * batch, batch), batch), :]
        return carry

    if reorder:
        for s in range(S5_TILES):
            bm_scr[s] = u[:, s * LANES:(s + 1) * LANES]
        lax.fori_loop(0, steps, to_time_major, 0)
        u_tm = jnp.concatenate([tm_scr[s] for s in range(S5_TILES)], axis=1)
    else:
        u_tm = u
    half = S5_HALF // S5_TILES
    for j in range(S5_TILES):
        bbj = _mm(u_tm[:, j * LANES:(j + 1) * LANES], bbar_ref[j], hp)
        br_scr[:, j * half:(j + 1) * half] = bbj[:, :half]
        bi_scr[:, j * half:(j + 1) * half] = bbj[:, half:]

    def step(t, carry):
        xr, xi = carry
        rws = pl.ds(pl.multiple_of(t * batch, batch), batch)
        ar = ar_ref[...]
        ai = ai_ref[...]
        nxr = ar * xr - ai * xi + br_scr[rws, :]
        nxi = ar * xi + ai * xr + bi_scr[rws, :]
        br_scr[rws, :] = nxr
        bi_scr[rws, :] = nxi
        return nxr, nxi

    carry = (sr_scr[...], si_scr[...])
    if steps == 1:
        carry = step(0, carry)
    else:
        lo = jnp.clip(t_start - c * steps, 0, steps)
        carry = lax.fori_loop(lo, steps, step, carry)
    sr_scr[...] = carry[0]
    si_scr[...] = carry[1]
    xr_out[...] = carry[0]
    xi_out[...] = carry[1]

    ys = []
    for j in range(S5_TILES):
        ys.append(_mm(br_scr[:, j * half:(j + 1) * half], cre_ref[j], hp)
                  + _mm(bi_scr[:, j * half:(j + 1) * half], cim_ref[j], hp))
    if reorder:
        for s in range(S5_TILES):
            tm_scr[s] = ys[s]
        lax.fori_loop(0, steps, to_batch_major, 0)
        y = jnp.concatenate([bm_scr[s] for s in range(S5_TILES)], axis=1)
    else:
        y = jnp.concatenate(ys, axis=1)
    y = jax.nn.gelu(y + d_ref[...] * u)
    out = _mm(y, g1_ref[...], hp) * _sigmoid(_mm(y, g2_ref[...], hp))
    ya_ref[...] = out.reshape(ya_ref.shape)


def _s5_branch(x, batch, t_len, t_start, P, x0r, x0i, hp):
    seq = x.ndim == 3
    steps = _pick_s5_chunk(t_len) if seq else 1
    nchunks = t_len // steps
    rows = batch * steps
    if seq:
        x_spec = pl.BlockSpec((batch, steps, D_MODEL), lambda c: (0, c, 0))
        scratch_extra = [pltpu.VMEM((S5_TILES, rows, LANES), F32), pltpu.VMEM((S5_TILES, rows, LANES), F32)]
    else:
        x_spec = pl.BlockSpec((batch, D_MODEL), lambda c: (0, 0))
        scratch_extra = []
    state_spec = pl.BlockSpec((batch, S5_HALF), lambda c: (0, 0))
    consts = [P["g_mix"], P["w_u"], P["s5_bbar"], P["s5_ar"][batch], P["s5_ai"][batch], x0r, x0i,
              P["s5_cre"], P["s5_cim"], P["s5_d"], P["s5_glu1"], P["s5_glu2"]]
    return pl.pallas_call(
        functools.partial(_s5_kernel, batch=batch, steps=steps, t_start=t_start, hp=hp, reorder=seq),
        grid=(nchunks,),
        in_specs=[x_spec] + [_const_spec(a.shape) for a in consts],
        out_specs=(x_spec, state_spec, state_spec),
        out_shape=(jax.ShapeDtypeStruct(x.shape, F32),
                   jax.ShapeDtypeStruct((batch, S5_HALF), F32),
                   jax.ShapeDtypeStruct((batch, S5_HALF), F32)),
        scratch_shapes=[pltpu.VMEM((rows, S5_HALF), F32), pltpu.VMEM((rows, S5_HALF), F32),
                        pltpu.VMEM((batch, S5_HALF), F32), pltpu.VMEM((batch, S5_HALF), F32)] + scratch_extra,
        compiler_params=_cparams(("arbitrary",)),
        name="s5_branch_hp" if hp else "s5_branch",
    )(x, *consts)


N_KVEC = 5
N_VEC = N_KVEC + 1


def _rw_prep_kernel(x_ref, gm_ref, wrw_ref, sh0_ref, mu_ref, wl_ref, w0_ref, a0_ref, gup_ref,
                    kk_ref, ka_ref, rk_ref, ones_ref,
                    vec_o, g_o, bon_o, sh_o, carry_scr, *, seq, t_start, hp):
    h = _rms(x_ref[...], gm_ref[...])
    p = _mm(h, wrw_ref[...], hp)
    rows = p.shape[0]
    if seq:
        c = pl.program_id(1)

        @pl.when(c == 0)
        def _():
            carry_scr[...] = jnp.zeros_like(carry_scr)

        row = lax.broadcasted_iota(jnp.int32, p.shape, 0)
        prev = jnp.where(row == 0, carry_scr[0:1, :], pltpu.roll(p, 1, 0))
        prev = jnp.where(row + c * rows == t_start, sh0_ref[...], prev)
        carry_scr[0:1, :] = p[rows - 1:rows, :]

        @pl.when(c == pl.num_programs(1) - 1)
        def _():
            sh_o[...] = p[rows - 1:rows, :]
    else:
        prev = sh0_ref[...]
        sh_o[...] = p
    z = p + (prev - p) * mu_ref[...]
    r = z[:, :D_B]
    k = z[:, D_B:2 * D_B]
    v = z[:, 2 * D_B:3 * D_B]
    zwa = z[:, 3 * D_B:3 * D_B + LORA_W + LORA_A]
    zg = z[:, 3 * D_B + LORA_W + LORA_A:]
    lane = lax.broadcasted_iota(jnp.int32, zwa.shape, 1)
    tw = jnp.where(lane < LORA_W, jnp.tanh(zwa), zwa)
    lw = _mm(tw, wl_ref[...], hp)
    w_log = -_softplus(-(w0_ref[...] + lw[:, :D_B])) - 0.5
    decay = jnp.exp(-jnp.exp(w_log))
    a = _sigmoid(a0_ref[...] + lw[:, D_B:])
    g = _mm(_sigmoid(zg), gup_ref[...], hp)
    kk = k * kk_ref[...]
    n2 = _segsum(kk * kk, ones_ref[...], hp)
    kkn = kk * lax.rsqrt(jnp.maximum(n2, 1e-24))
    k2 = k * (1.0 + (a - 1.0) * ka_ref[...])
    rk = _segsum(r * k2 * rk_ref[...], ones_ref[...], hp)
    vec_o[0] = kkn
    vec_o[1] = decay
    vec_o[2] = -(kkn * a)
    vec_o[3] = k2
    vec_o[4] = r
    vec_o[5] = v
    g_o[...] = g
    bon_o[...] = rk * v


def _rw_prep(x, P, sh0, t_start, hp):
    seq = x.ndim == 3
    consts = [P["g_mix"], P["w_rw"]]
    consts2 = [P["rw_mu"], P["rw_wl"], P["rw_w0"], P["rw_a0"], P["rw_g_up"],
               P["rw_k_k"], P["rw_k_a"], P["rw_r_k"], P["ones_head"]]
    if seq:
        batch, t_len, _ = x.shape
        tc = _pick_tile(t_len, 544)
        grid = (batch, t_len // tc)
        row_spec = lambda w: pl.BlockSpec((None, tc, w), lambda b, c: (b, c, 0))
        vec_spec = pl.BlockSpec((N_VEC, None, tc, D_B), lambda b, c: (0, b, c, 0))
        sh_spec = pl.BlockSpec((None, 1, P_RW), lambda b, c: (b, 0, 0))
        sh0 = sh0.reshape(batch, 1, P_RW)
        vec_shape = (N_VEC, batch, t_len, D_B)
        g_shape = (batch, t_len, D_B)
        sh_shape = (batch, 1, P_RW)
        sem = ("arbitrary", "arbitrary")
    else:
        batch = x.shape[0]
        grid = (1,)
        row_spec = lambda w: pl.BlockSpec((batch, w), lambda i: (0, 0))
        vec_spec = pl.BlockSpec((N_VEC, batch, D_B), lambda i: (0, 0, 0))
        sh_spec = row_spec(P_RW)
        vec_shape = (N_VEC, batch, D_B)
        g_shape = (batch, D_B)
        sh_shape = (batch, P_RW)
        sem = ("arbitrary",)
    return pl.pallas_call(
        functools.partial(_rw_prep_kernel, seq=seq, t_start=t_start, hp=hp),
        grid=grid,
        in_specs=[row_spec(D_MODEL)] + [_const_spec(a.shape) for a in consts] + [sh_spec]
                 + [_const_spec(a.shape) for a in consts2],
        out_specs=(vec_spec, row_spec(D_B), row_spec(D_B), sh_spec),
        out_shape=(jax.ShapeDtypeStruct(vec_shape, F32), jax.ShapeDtypeStruct(g_shape, F32),
                   jax.ShapeDtypeStruct(g_shape, F32), jax.ShapeDtypeStruct(sh_shape, F32)),
        scratch_shapes=[pltpu.VMEM((SUBLANES, P_RW), F32)],
        compiler_params=_cparams(sem),
        name="rw_prep_hp" if hp else "rw_prep",
    )(x, *consts, sh0, *consts2)


RW_VH = RW_HEAD // (2 * SUBLANES)
RW_PAIRS = SUBLANES * RW_HEADS
RW_SC = RW_TC // 2


RW_QP = RW_HEAD + 4


def _pair_rows(i):
    return pl.ds(i, RW_PAIRS, stride=RW_QP)


def _pairs_to_rows(x_ref, q_scr, batch):
    for b in range(batch):
        for j in range(DB_TILES):
            tile = x_ref[b, :, j * LANES:(j + 1) * LANES].T
            pair = b * RW_HEADS + 2 * j
            q_scr[pl.ds(pair * RW_QP, RW_HEAD), :] = tile[:RW_HEAD]
            q_scr[pl.ds((pair + 1) * RW_QP, RW_HEAD), :] = tile[RW_HEAD:]


def _rw_kin_kernel(x_ref, o_ref, q_scr, *, batch):
    _pairs_to_rows(x_ref, q_scr, batch)
    for k in range(RW_HEAD):
        m = q_scr[_pair_rows(k), :]
        o_ref[k] = jnp.concatenate([m, m], axis=0).T


def _rw_vin_kernel(x_ref, o_ref, q_scr, *, batch):
    _pairs_to_rows(x_ref, q_scr, batch)
    for vh in range(RW_VH):
        for vs in range(SUBLANES):
            v0 = (vh * SUBLANES + vs) * 2
            pair_tile = jnp.concatenate([q_scr[_pair_rows(v0), :], q_scr[_pair_rows(v0 + 1), :]], axis=0)
            o_ref[vh, pl.ds(vs, RW_TC, stride=SUBLANES), :] = pair_tile.T


def _rw_unlayout_kernel(o3_ref, o_ref, q_scr, *, batch):
    for vh in range(RW_VH):
        for vs in range(SUBLANES):
            v0 = (vh * SUBLANES + vs) * 2
            zt = o3_ref[vh, pl.ds(vs, RW_TC, stride=SUBLANES), :].T
            q_scr[_pair_rows(v0), :] = zt[:RW_PAIRS]
            q_scr[_pair_rows(v0 + 1), :] = zt[RW_PAIRS:]
    for b in range(batch):
        for j in range(DB_TILES):
            pair = b * RW_HEADS + 2 * j
            tile = jnp.concatenate([q_scr[pl.ds(pair * RW_QP, RW_HEAD), :],
                                    q_scr[pl.ds((pair + 1) * RW_QP, RW_HEAD), :]], axis=0)
            o_ref[b, :, j * LANES:(j + 1) * LANES] = tile.T


def _rw_scan_kernel(kin_ref, vin_ref, s0_ref, o_ref, sfin_ref, s_scr, *, t_start):
    c = pl.program_id(0)

    @pl.when(c == 0)
    def _():
        s_scr[...] = s0_ref[...]

    def krow(t, vec, k):
        return jnp.broadcast_to(kin_ref[vec * RW_HEAD + k, pl.ds(t, 1), :], (SUBLANES, LANES))

    def acc_add(acc, vh, k, x):
        prev = acc[vh][k % 2]
        acc[vh][k % 2] = x if prev is None else prev + x

    def first_sa(t):
        acc = [[None, None] for _ in range(RW_VH)]
        for k in range(RW_HEAD):
            kkb = krow(t, 0, k)
            for vh in range(RW_VH):
                acc_add(acc, vh, k, s_scr[vh, k] * kkb)
        return tuple(a[0] + a[1] for a in acc)

    def step(t, sa):
        tn = jnp.minimum(t + 1, RW_SC - 1)
        tv = pl.multiple_of(t * SUBLANES, SUBLANES)
        vv = [vin_ref[vh, pl.ds(tv, SUBLANES), :] for vh in range(RW_VH)]
        oacc = [[None, None] for _ in range(RW_VH)]
        nacc = [[None, None] for _ in range(RW_VH)]
        for k in range(RW_HEAD):
            wb = krow(t, 1, k)
            nbb = krow(t, 2, k)
            kb = krow(t, 3, k)
            rb = krow(t, 4, k)
            kkn = krow(tn, 0, k)
            for vh in range(RW_VH):
                s = s_scr[vh, k] * wb + sa[vh] * nbb + vv[vh] * kb
                s_scr[vh, k] = s
                acc_add(oacc, vh, k, s * rb)
                acc_add(nacc, vh, k, s * kkn)
        for vh in range(RW_VH):
            o_ref[vh, pl.ds(tv, SUBLANES), :] = oacc[vh][0] + oacc[vh][1]
        return tuple(a[0] + a[1] for a in nacc)

    lo = jnp.clip(t_start - c * RW_SC, 0, RW_SC)

    @pl.when(lo > 0)
    def _():
        o_ref[...] = jnp.zeros_like(o_ref)

    lax.fori_loop(lo, RW_SC, step, first_sa(jnp.minimum(lo, RW_SC - 1)))
    sfin_ref[...] = s_scr[...]


def _rw_recurrence_long(vecs, s0, t_start):
    _, batch, t_len, _ = vecs.shape
    assert batch == SUBLANES and t_len % RW_TC == 0
    nchunks = t_len // RW_TC
    q_scr = pltpu.VMEM((RW_PAIRS * RW_QP, LANES), F32)
    kin = pl.pallas_call(
        functools.partial(_rw_kin_kernel, batch=batch),
        grid=(nchunks, N_KVEC),
        in_specs=[pl.BlockSpec((None, batch, RW_TC, D_B), lambda c, i: (i, 0, c, 0))],
        out_specs=pl.BlockSpec((None, RW_HEAD, RW_TC, LANES), lambda c, i: (c, i, 0, 0)),
        out_shape=jax.ShapeDtypeStruct((nchunks, N_KVEC * RW_HEAD, RW_TC, LANES), F32),
        scratch_shapes=[q_scr],
        compiler_params=_cparams(("arbitrary", "arbitrary")),
        name="rw_kin",
    )(vecs)
    vin = pl.pallas_call(
        functools.partial(_rw_vin_kernel, batch=batch),
        grid=(nchunks,),
        in_specs=[pl.BlockSpec((None, batch, RW_TC, D_B), lambda c: (N_KVEC, 0, c, 0))],
        out_specs=pl.BlockSpec((None, RW_VH, RW_TC * SUBLANES, LANES), lambda c: (c, 0, 0, 0)),
        out_shape=jax.ShapeDtypeStruct((nchunks, RW_VH, RW_TC * SUBLANES, LANES), F32),
        scratch_shapes=[q_scr],
        compiler_params=_cparams(("arbitrary",)),
        name="rw_vin",
    )(vecs)
    sshape = (RW_VH, RW_HEAD, SUBLANES, LANES)
    s = s0.reshape(batch, RW_HEADS, RW_VH, SUBLANES, 2, RW_HEAD)
    s = jnp.transpose(s, (2, 5, 3, 4, 0, 1)).reshape(sshape)
    state_spec = pl.BlockSpec(sshape, lambda c: (0, 0, 0, 0))
    halves = RW_TC // RW_SC
    vblk = (None, RW_VH, RW_SC * SUBLANES, LANES)
    vmap = lambda c: (c // halves, 0, c % halves, 0)
    o3, sfin = pl.pallas_call(
        functools.partial(_rw_scan_kernel, t_start=t_start),
        grid=(nchunks * halves,),
        in_specs=[pl.BlockSpec((None, N_KVEC * RW_HEAD, RW_SC, LANES), vmap),
                  pl.BlockSpec(vblk, vmap), state_spec],
        out_specs=(pl.BlockSpec(vblk, vmap), state_spec),
        out_shape=(jax.ShapeDtypeStruct((nchunks, RW_VH, RW_TC * SUBLANES, LANES), F32),
                   jax.ShapeDtypeStruct(sshape, F32)),
        scratch_shapes=[pltpu.VMEM(sshape, F32)],
        compiler_params=_cparams(("arbitrary",)),
        name="rw_scan",
    )(kin, vin, s)
    o = pl.pallas_call(
        functools.partial(_rw_unlayout_kernel, batch=batch),
        grid=(nchunks,),
        in_specs=[pl.BlockSpec((None, RW_VH, RW_TC * SUBLANES, LANES), lambda c: (c, 0, 0, 0))],
        out_specs=pl.BlockSpec((batch, RW_TC, D_B), lambda c: (0, c, 0)),
        out_shape=jax.ShapeDtypeStruct((batch, t_len, D_B), F32),
        scratch_shapes=[q_scr],
        compiler_params=_cparams(("arbitrary",)),
        name="rw_unlayout",
    )(o3)
    sfin = sfin.reshape(RW_VH, RW_HEAD, SUBLANES, 2, batch, RW_HEADS)
    sfin = jnp.transpose(sfin, (4, 5, 0, 2, 3, 1)).reshape(batch, RW_HEADS, RW_HEAD, RW_HEAD)
    return o, sfin


STEP_PAIRS = 64
STEP_VROWS = RW_HEAD // 2


def _split3(x):
    hi = x.astype(BF16)
    r1 = x - hi.astype(F32)
    mid = r1.astype(BF16)
    return hi, mid, (r1 - mid.astype(F32)).astype(BF16)


def _rw_step_kernel(s_ref, km_ref, v_ref, sel_ref, vmask_ref, vexp_ref, ones_ref, s_o, o_o):
    sel = sel_ref[...]

    def rows_of(x):
        return sum(jnp.dot(sel, p, preferred_element_type=F32) for p in _split3(x))

    kx = rows_of(km_ref[...])
    kk, w, nb, k, r = [kx[:, i * LANES:(i + 1) * LANES] for i in range(N_KVEC)]
    vx = _segsum(rows_of(v_ref[...]) * vmask_ref[...], vexp_ref[...], True)
    s = s_ref[...]
    ones = ones_ref[...]
    sa = _segsum(s * kk, ones, True)
    s = s * w + sa * nb + vx * k
    s_o[...] = s
    o_o[...] = _segsum(s * r, ones, True)


def _rw_recurrence_step(vecs, s0, ones_half):
    batch = vecs.shape[1]
    pairs = batch * RW_HEADS
    rows = pairs * STEP_VROWS
    tr = STEP_PAIRS * STEP_VROWS
    assert pairs % STEP_PAIRS == 0
    kp = vecs[:N_KVEC].reshape(N_KVEC, pairs, 1, RW_HEAD)
    km = jnp.transpose(jnp.broadcast_to(kp, (N_KVEC, pairs, 2, RW_HEAD)), (1, 0, 2, 3)).reshape(pairs, N_KVEC * LANES)
    vp = vecs[N_KVEC].reshape(pairs, RW_HEAD)
    ridx = jnp.arange(tr, dtype=jnp.int32)
    sel = (ridx[:, None] // STEP_VROWS == jnp.arange(STEP_PAIRS, dtype=jnp.int32)[None, :]).astype(BF16)
    vidx = jnp.arange(RW_HEAD, dtype=jnp.int32)
    vmask = (vidx[None, :] // 2 == ridx[:, None] % STEP_VROWS).astype(F32)
    vexp = (vidx[:, None] % 2 == jnp.arange(LANES, dtype=jnp.int32)[None, :] // RW_HEAD).astype(BF16)
    spec = pl.BlockSpec((tr, LANES), lambda i: (i, 0))
    big = jax.ShapeDtypeStruct((rows, LANES), F32)
    consts = [sel, vmask, vexp, ones_half]
    s_new, o = pl.pallas_call(
        _rw_step_kernel,
        grid=(pairs // STEP_PAIRS,),
        in_specs=[spec, pl.BlockSpec((STEP_PAIRS, N_KVEC * LANES), lambda i: (i, 0)),
                  pl.BlockSpec((STEP_PAIRS, RW_HEAD), lambda i: (i, 0))]
                 + [_const_spec(a.shape) for a in consts],
        out_specs=(spec, spec),
        out_shape=(big, big),
        compiler_params=_cparams(("arbitrary",)),
        name="rw_step",
    )(s0.reshape(rows, LANES), km, vp, *consts)
    o = o.reshape(rows, 2, RW_HEAD)[:, :, 0].reshape(batch, D_B)
    return o, s_new.reshape(batch, RW_HEADS, RW_HEAD, RW_HEAD)


def _mix_kernel(*refs, hp, moe):
    (x_ref, ya_ref, o_ref, bon_ref, g_ref, gm_ref, wg_ref, gng_ref, gnb_ref, ones_ref,
     wbo_ref, wout_ref, gf_ref) = refs[:13]
    if moe:
        wr_ref, x1_o, h2_o, route_o = refs[13:]
    else:
        x1_o, h2_o = refs[13:]
    x = x_ref[...]
    h = _rms(x, gm_ref[...])
    gates = _mm(h, wg_ref[...], hp)
    o = o_ref[...]
    ones = ones_ref[...]
    inv_n = 1.0 / RW_HEAD
    mean = _segsum(o, ones, hp) * inv_n
    dlt = o - mean
    var = _segsum(dlt * dlt, ones, hp) * inv_n
    on = dlt * lax.rsqrt(var + GN_EPS) * gng_ref[...] + gnb_ref[...]
    yb = _mm((on + bon_ref[...]) * g_ref[...], wbo_ref[...], hp)
    m = _sigmoid(gates[:, :D_MODEL]) * ya_ref[...] + _sigmoid(gates[:, D_MODEL:]) * yb
    x1 = x + _mm(m, wout_ref[...], hp)
    h2 = _rms(x1, gf_ref[...])
    x1_o[...] = x1
    h2_o[...] = h2.astype(h2_o.dtype)
    if moe:
        logits = _mm(h2, wr_ref[...], hp)
        lane = lax.broadcasted_iota(jnp.int32, logits.shape, 1).astype(F32)
        neg = jnp.float32(-jnp.inf)
        lg = jnp.where(lane < N_EXPERTS, logits, neg)
        m1 = jnp.max(lg, axis=1, keepdims=True)
        i1 = jnp.min(jnp.where(lg == m1, lane, float(LANES)), axis=1, keepdims=True)
        lg2 = jnp.where(lane == i1, neg, lg)
        m2 = jnp.max(lg2, axis=1, keepdims=True)
        i2 = jnp.min(jnp.where(lg2 == m2, lane, float(LANES)), axis=1, keepdims=True)
        e = jnp.exp(m2 - m1)
        g1 = 1.0 / (1.0 + e)
        g2 = e / (1.0 + e)
        route_o[...] = jnp.where(lane == 0.0, i1, jnp.where(lane == 1.0, i2,
                                 jnp.where(lane == 2.0, g1, jnp.where(lane == 3.0, g2, 0.0))))


def _mix(x, ya, o, bon, g, P, hp, moe):
    rows = x.shape[0]
    tm = _pick_tile(rows, 344)
    row_spec = lambda w: pl.BlockSpec((tm, w), lambda i: (i, 0))
    consts = [P["g_mix"], P["w_gate"], P["rw_gn_g"], P["rw_gn_b"], P["ones_head"],
              P["rw_w_bo"], P["w_out"], P["g_ffn"]]
    if moe:
        consts.append(P["w_router"])
    h2_dtype = F32 if (hp or moe) else BF16
    out_specs = [row_spec(D_MODEL), row_spec(D_MODEL)]
    out_shape = [jax.ShapeDtypeStruct((rows, D_MODEL), F32), jax.ShapeDtypeStruct((rows, D_MODEL), h2_dtype)]
    if moe:
        out_specs.append(row_spec(LANES))
        out_shape.append(jax.ShapeDtypeStruct((rows, LANES), F32))
    return pl.pallas_call(
        functools.partial(_mix_kernel, hp=hp, moe=moe),
        grid=(rows // tm,),
        in_specs=[row_spec(D_MODEL), row_spec(D_MODEL), row_spec(D_B), row_spec(D_B), row_spec(D_B)]
                 + [_const_spec(a.shape) for a in consts],
        out_specs=tuple(out_specs),
        out_shape=tuple(out_shape),
        compiler_params=_cparams(("arbitrary",)),
        name=("mix_moe" if moe else "mix") + ("_hp" if hp else ""),
    )(x, ya, o, bon, g, *consts)


def _ffn_kernel(h_ref, x1_ref, w1_ref, w3_ref, w2_ref, o_ref, *, hp):
    h = h_ref[...]
    a = _mm(h, w1_ref[...], hp)
    b = _mm(h, w3_ref[...], hp)
    o_ref[...] = x1_ref[...] + _mm(a * _sigmoid(a) * b, w2_ref[...], hp)


def _ffn(h2, x1, w1, w3, w2, tm, hp):
    rows = h2.shape[0]
    assert rows % tm == 0
    row_spec = pl.BlockSpec((tm, D_MODEL), lambda i: (i, 0))
    return pl.pallas_call(
        functools.partial(_ffn_kernel, hp=hp),
        grid=(rows // tm,),
        in_specs=[row_spec, row_spec, _const_spec(w1.shape), _const_spec(w3.shape), _const_spec(w2.shape)],
        out_specs=row_spec,
        out_shape=jax.ShapeDtypeStruct((rows, D_MODEL), F32),
        compiler_params=_cparams(("arbitrary",)),
        name="ffn_hp" if hp else "ffn",
    )(h2, x1, w1, w3, w2)


MOE_TM = 1024
MOE_TF = 512


def _moe_kernel(te_ref, nv_ref, x_ref, w1_ref, w3_ref, w2_ref, o_ref, acc, xb):
    i = pl.program_id(0)
    f = pl.program_id(1)

    @pl.when(f == 0)
    def _():
        acc[...] = jnp.zeros_like(acc)
        xb[...] = x_ref[...].astype(BF16)

    @pl.when(i < nv_ref[0])
    def _():
        x = xb[...]
        a = jnp.dot(x, w1_ref[...].astype(BF16), preferred_element_type=F32)
        b = jnp.dot(x, w3_ref[...].astype(BF16), preferred_element_type=F32)
        acc[...] += jnp.dot((a * _sigmoid(a) * b).astype(BF16), w2_ref[...].astype(BF16),
                            preferred_element_type=F32)

    @pl.when(f == pl.num_programs(1) - 1)
    def _():
        o_ref[...] = acc[...]


def _moe_experts(xs, tile_expert, n_valid, w1, w3, w2):
    rows = xs.shape[0]
    ntiles = rows // MOE_TM
    nf = D_EXPERT // MOE_TF
    assert ntiles * MOE_TM == rows and nf * MOE_TF == D_EXPERT

    def fblk(i, f, nv):
        return jnp.where(i < nv[0], f, nf - 1)

    grid_spec = pltpu.PrefetchScalarGridSpec(
        num_scalar_prefetch=2,
        grid=(ntiles, nf),
        in_specs=[pl.BlockSpec((MOE_TM, D_MODEL), lambda i, f, te, nv: (i, 0)),
                  pl.BlockSpec((None, D_MODEL, MOE_TF), lambda i, f, te, nv: (te[i], 0, fblk(i, f, nv))),
                  pl.BlockSpec((None, D_MODEL, MOE_TF), lambda i, f, te, nv: (te[i], 0, fblk(i, f, nv))),
                  pl.BlockSpec((None, MOE_TF, D_MODEL), lambda i, f, te, nv: (te[i], fblk(i, f, nv), 0))],
        out_specs=pl.BlockSpec((MOE_TM, D_MODEL), lambda i, f, te, nv: (i, 0)),
        scratch_shapes=[pltpu.VMEM((MOE_TM, D_MODEL), F32), pltpu.VMEM((MOE_TM, D_MODEL), BF16)])
    return pl.pallas_call(
        _moe_kernel,
        grid_spec=grid_spec,
        out_shape=jax.ShapeDtypeStruct((rows, D_MODEL), F32),
        compiler_params=_cparams(("arbitrary", "arbitrary")),
        name="moe_experts",
    )(tile_expert, n_valid, xs, w1, w3, w2)


def _moe_plan(expert_idx, src_row):
    n_pairs = expert_idx.shape[0] * 2
    flat_e = expert_idx.reshape(n_pairs)
    onehot = (flat_e[:, None] == jnp.arange(N_EXPERTS, dtype=jnp.int32)[None, :]).astype(jnp.int32)
    csum = jnp.cumsum(onehot, axis=0)
    rank = jnp.sum(csum * onehot, axis=1) - 1
    counts = csum[-1]
    padded = ((counts + MOE_TM - 1) // MOE_TM) * MOE_TM
    pend = jnp.cumsum(padded)
    pstart = pend - padded
    dest = pstart[flat_e] + rank
    ntiles = (n_pairs + N_EXPERTS * (MOE_TM - 1)) // MOE_TM + 1
    rows = ntiles * MOE_TM
    tile_start = jnp.arange(ntiles, dtype=jnp.int32) * MOE_TM
    n_valid = (pend[-1] // MOE_TM).astype(jnp.int32)
    te = jnp.sum((pend[None, :] <= tile_start[:, None]).astype(jnp.int32), axis=1)
    te = jnp.minimum(te, N_EXPERTS - 1)
    order = jnp.argsort(flat_e, stable=True).astype(jnp.int32)
    gstart = jnp.cumsum(counts) - counts
    slot = jnp.arange(rows, dtype=jnp.int32)
    e_slot = jnp.repeat(te, MOE_TM)
    off = slot - pstart[e_slot]
    filled = (off < counts[e_slot]) & (slot < pend[-1])
    pair = order[jnp.clip(gstart[e_slot] + off, 0, n_pairs - 1)]
    src = jnp.where(filled, src_row[pair // 2], 0).astype(jnp.int32)
    last_e = te[jnp.maximum(n_valid - 1, 0)]
    te = jnp.where(jnp.arange(ntiles) < n_valid, te, last_e).astype(jnp.int32)
    return src, dest.reshape(-1, 2), te, n_valid.reshape(1)


def _combine_kernel(x1_ref, y1_ref, y2_ref, route_ref, gf_ref, o_ref):
    route = route_ref[...]
    x2 = x1_ref[...] + route[:, 2:3] * y1_ref[...] + route[:, 3:4] * y2_ref[...]
    o_ref[...] = _rms(x2, gf_ref[...])


def _combine_norm(x1, x1_spec, flat_blk0, grid, out_shape, out_spec, y1, y2, route, g_final, tm):
    nd = len(grid)

    def flat_map(*idx):
        lin = idx[0]
        for k in range(1, nd):
            lin = lin * grid[k] + idx[k]
        return (flat_blk0 + lin, 0)

    return pl.pallas_call(
        _combine_kernel,
        grid=grid,
        in_specs=[x1_spec, pl.BlockSpec((tm, D_MODEL), flat_map), pl.BlockSpec((tm, D_MODEL), flat_map),
                  pl.BlockSpec((tm, LANES), flat_map), _const_spec(g_final.shape)],
        out_specs=out_spec,
        out_shape=out_shape,
        compiler_params=_cparams(("arbitrary",) * nd),
        name="combine_norm",
    )(x1, y1, y2, route, g_final)


MATMUL_WEIGHTS = ("w_u", "w_rw", "w_gate", "s5_glu1", "s5_glu2", "rw_g_up", "rw_w_bo", "w_out", "rw_wl",
                  "s5_bbar", "s5_cre", "s5_cim", "w_router")


def _layer_params(l, W, hp, batches):
    wdt = F32 if hp else BF16
    row = lambda a: a.reshape(1, -1).astype(F32)
    w_in = W["w_in"][l]
    P = {
        "g_mix": row(W["norm_mix"][l]),
        "g_ffn": row(W["norm_ffn"][l]),
        "w_u": w_in[:, :D_A].astype(wdt),
        "w_rw": w_in[:, D_A:D_A + P_RW].astype(wdt),
        "w_gate": w_in[:, D_A + P_RW:].astype(wdt),
        "s5_d": row(W["s5_d"][l]),
        "s5_glu1": W["s5_glu1"][l].astype(wdt),
        "s5_glu2": W["s5_glu2"][l].astype(wdt),
        "rw_mu": row(W["rw_mu"][l]),
        "rw_w0": row(W["rw_w0"][l]),
        "rw_a0": row(W["rw_a0"][l]),
        "rw_g_up": W["rw_g_up"][l].astype(wdt),
        "rw_k_k": row(W["rw_k_k"][l]),
        "rw_k_a": row(W["rw_k_a"][l]),
        "rw_r_k": row(W["rw_r_k"][l]),
        "rw_gn_g": row(W["rw_gn_g"][l]),
        "rw_gn_b": row(W["rw_gn_b"][l]),
        "rw_w_bo": W["rw_w_bo"][l].astype(wdt),
        "w_out": W["w_out"][l].astype(wdt),
    }
    wl = jnp.zeros((LORA_W + LORA_A, 2 * D_B), F32)
    wl = wl.at[:LORA_W, :D_B].set(W["rw_w_up"][l]).at[LORA_W:, D_B:].set(W["rw_a_up"][l])
    P["rw_wl"] = wl.astype(wdt)
    P["ones_head"] = jnp.kron(jnp.eye(RW_HEADS, dtype=F32), jnp.ones((RW_HEAD, RW_HEAD), F32)).astype(BF16)

    ab_re, ab_im, bb_re, bb_im = _s5_discretize(W["s5_lam_re"][l], W["s5_lam_im"][l], W["s5_log_dt"][l],
                                                W["s5_b_re"][l], W["s5_b_im"][l])
    P["s5_ar"] = {b: jnp.broadcast_to(ab_re.reshape(1, S5_HALF), (b, S5_HALF)) for b in batches}
    P["s5_ai"] = {b: jnp.broadcast_to(ab_im.reshape(1, S5_HALF), (b, S5_HALF)) for b in batches}
    gpt = S5_GROUPS // S5_TILES
    eye = jnp.eye(gpt, dtype=F32)
    bb = jnp.stack([bb_re, bb_im]).reshape(2, S5_GROUP, S5_TILES, gpt, S5_STATE)
    bb = jnp.transpose(bb, (2, 1, 0, 3, 4))
    bbar = bb[:, None] * eye[None, :, None, None, :, None]
    P["s5_bbar"] = bbar.reshape(S5_TILES, LANES, 2 * gpt * S5_STATE).astype(wdt)
    cs = jnp.stack([W["s5_c_re"][l], -W["s5_c_im"][l]]).reshape(2, S5_TILES, gpt, S5_GROUP, S5_STATE)
    cs = jnp.transpose(cs, (1, 0, 2, 4, 3))
    cm = cs[:, :, :, :, None, :] * eye[None, None, :, None, :, None]
    cm = cm.reshape(S5_TILES, 2, gpt * S5_STATE, LANES).astype(wdt)
    P["s5_cre"] = cm[:, 0]
    P["s5_cim"] = cm[:, 1]
    return P


def _mixer_seq(x, t_start, s5r0, s5i0, rw_s0, rw_sh0, P, moe):
    batch, t_len, _ = x.shape
    ya, xr, xi = _s5_branch(x, batch, t_len, t_start, P, s5r0, s5i0, False)
    vecs, g, bon, sh = _rw_prep(x, P, rw_sh0, t_start, False)
    o, s_fin = _rw_recurrence_long(vecs, rw_s0, t_start)
    flat = lambda a: a.reshape(batch * t_len, a.shape[-1])
    outs = _mix(flat(x), flat(ya), flat(o), flat(bon), flat(g), P, False, moe)
    return outs, (xr, xi, s_fin, sh.reshape(batch, P_RW))


def _mixer_step(x, s5r0, s5i0, rw_s0, rw_sh0, P, moe, ones_half):
    batch = x.shape[0]
    ya, xr, xi = _s5_branch(x, batch, 1, 0, P, s5r0, s5i0, True)
    vecs, g, bon, sh = _rw_prep(x, P, rw_sh0, 0, True)
    o, s_fin = _rw_recurrence_step(vecs, rw_s0, ones_half)
    outs = _mix(x, ya, o, bon, g, P, True, moe)
    return outs, (xr, xi, s_fin, sh)


def kernel(x_prompt, x_sample, state_s5_re, state_s5_im, state_rwkv, state_shift, meta_tokens, norm_mix, w_in, s5_lam_re, s5_lam_im, s5_log_dt, s5_b_re, s5_b_im, s5_c_re, s5_c_im, s5_d, s5_glu1, s5_glu2, rw_mu, rw_w0, rw_w_up, rw_a0, rw_a_up, rw_g_up, rw_k_k, rw_k_a, rw_r_k, rw_gn_g, rw_gn_b, rw_w_bo, w_out, norm_ffn, ffn_w1, ffn_w3, ffn_w2, moe_router, moe_w1, moe_w3, moe_w2, norm_final):
    W = dict(norm_mix=norm_mix, w_in=w_in, s5_lam_re=s5_lam_re, s5_lam_im=s5_lam_im,
             s5_log_dt=s5_log_dt, s5_b_re=s5_b_re, s5_b_im=s5_b_im, s5_c_re=s5_c_re,
             s5_c_im=s5_c_im, s5_d=s5_d, s5_glu1=s5_glu1, s5_glu2=s5_glu2, rw_mu=rw_mu,
             rw_w0=rw_w0, rw_w_up=rw_w_up, rw_a0=rw_a0, rw_a_up=rw_a_up, rw_g_up=rw_g_up,
             rw_k_k=rw_k_k, rw_k_a=rw_k_a, rw_r_k=rw_r_k, rw_gn_g=rw_gn_g, rw_gn_b=rw_gn_b,
             rw_w_bo=rw_w_bo, w_out=w_out, norm_ffn=norm_ffn)
    bp, seq, _ = x_prompt.shape
    bs = x_sample.shape[0]
    assert x_sample.shape[1] == 1 and seq % RW_TC == 0 and bp == SUBLANES
    tp = N_META + seq
    t_pad = -(-tp // RW_TC) * RW_TC
    t_start = t_pad - tp
    out0 = t_pad - seq

    meta = jnp.broadcast_to(meta_tokens.astype(F32)[None], (bp, N_META, D_MODEL))
    xp = jnp.concatenate([jnp.zeros((bp, t_start, D_MODEL), F32), meta, x_prompt], axis=1)
    xs = x_sample.reshape(bs, D_MODEL)
    ones_half = jnp.kron(jnp.eye(2, dtype=F32), jnp.ones((RW_HEAD, RW_HEAD), F32)).astype(BF16)
    g_final = norm_final.reshape(1, D_MODEL).astype(F32)

    zero_s5 = jnp.zeros((bp, S5_HALF), F32)
    zero_rw = jnp.zeros((bp, RW_HEADS, RW_HEAD, RW_HEAD), F32)
    zero_sh = jnp.zeros((bp, P_RW), F32)

    p_states, s_states = [], []
    for l in range(DEPTH):
        moe = (l % 2 == 1)
        j = l // 2
        Ph = _layer_params(l, W, True, (bp, bs))
        if moe:
            Ph["w_router"] = jnp.zeros((D_MODEL, LANES), F32).at[:, :N_EXPERTS].set(moe_router[j])
        Pl = {k: (v.astype(BF16) if k in MATMUL_WEIGHTS else v) for k, v in Ph.items()}
        outs_p, st_p = _mixer_seq(xp, t_start, zero_s5, zero_s5, zero_rw, zero_sh, Pl, moe)
        outs_s, st_s = _mixer_step(xs, state_s5_re[l].reshape(bs, S5_HALF), state_s5_im[l].reshape(bs, S5_HALF),
                                   state_rwkv[l], state_shift[l], Ph, moe, ones_half)
        p_states.append(st_p)
        s_states.append(st_s)
        if not moe:
            rows = bp * t_pad
            xp = _ffn(outs_p[1], outs_p[0], ffn_w1[j].astype(BF16), ffn_w3[j].astype(BF16),
                      ffn_w2[j].astype(BF16), _pick_tile(rows, 544), False).reshape(bp, t_pad, D_MODEL)
            xs = _ffn(outs_s[1], outs_s[0], ffn_w1[j], ffn_w3[j], ffn_w2[j], bs, True)
        else:
            assert l == DEPTH - 1
            n_p = bp * seq
            h2 = jnp.concatenate([outs_p[1], outs_s[1]], axis=0)
            route_p = outs_p[2].reshape(bp, t_pad, LANES)[:, out0:].reshape(n_p, LANES)
            route = jnp.concatenate([route_p, outs_s[2]], axis=0)
            tok = jnp.arange(n_p, dtype=jnp.int32)
            src_row = jnp.concatenate([(tok // seq) * t_pad + out0 + tok % seq,
                                       bp * t_pad + jnp.arange(bs, dtype=jnp.int32)])
            src, dest, te, n_valid = _moe_plan(route[:, :2].astype(jnp.int32), src_row)
            take = functools.partial(jnp.take, axis=0, mode="clip")
            y_sorted = _moe_experts(take(h2, src), te, n_valid,
                                    moe_w1[j], moe_w3[j], moe_w2[j])
            y1 = take(y_sorted, dest[:, 0])
            y2 = take(y_sorted, dest[:, 1])
            tm = RW_TC
            nblk = seq // tm
            y_prompt = _combine_norm(
                outs_p[0].reshape(bp, t_pad, D_MODEL),
                pl.BlockSpec((None, tm, D_MODEL), lambda b, s: (b, s + out0 // tm, 0)), 0, (bp, nblk),
                jax.ShapeDtypeStruct((bp, seq, D_MODEL), F32),
                pl.BlockSpec((None, tm, D_MODEL), lambda b, s: (b, s, 0)), y1, y2, route, g_final, tm)
            assert n_p % bs == 0
            y_sample = _combine_norm(
                outs_s[0], pl.BlockSpec((bs, D_MODEL), lambda i: (0, 0)), n_p // bs, (1,),
                jax.ShapeDtypeStruct((bs, D_MODEL), F32),
                pl.BlockSpec((bs, D_MODEL), lambda i: (0, 0)), y1, y2, route, g_final, bs)

    y_sample = y_sample.reshape(bs, 1, D_MODEL)

    def stack(states, b):
        re = jnp.stack([s[0].reshape(b, S5_GROUPS, S5_STATE) for s in states])
        im = jnp.stack([s[1].reshape(b, S5_GROUPS, S5_STATE) for s in states])
        rw = jnp.stack([s[2] for s in states])
        sh = jnp.stack([s[3] for s in states])
        return re, im, rw, sh

    p_re, p_im, p_rw, p_sh = stack(p_states, bp)
    s_re, s_im, s_rw, s_sh = stack(s_states, bs)
    return (y_prompt, y_sample, p_re, p_im, p_rw, p_sh, s_re, s_im, s_rw, s_sh)
```

```python
import functools

import jax
import jax.numpy as jnp
from jax import lax
from jax.experimental import pallas as pl
from jax.experimental.pallas import tpu as pltpu

F32 = jnp.float32
BF16 = jnp.bfloat16
HIGHEST = lax.Precision.HIGHEST

D_MODEL = 1024
DEPTH = 2
N_META = 16
D_A = 512
S5_GROUP = 16
S5_GROUPS = 32
S5_STATE = 64
S5_TILES = 4
S5_HALF = S5_GROUPS * S5_STATE
D_B = 512
RW_HEAD = 64
RW_HEADS = 8
LORA_W = 64
LORA_A = 64
LORA_G = 128
GN_EPS = 64e-5
P_RW = 3 * D_B + LORA_W + LORA_A + LORA_G
D_FF = 2816
N_EXPERTS = 8
D_EXPERT = 3584
RMS_EPS = 1e-6
LANES = 128
SUBLANES = 8
VMEM_LIMIT_MB = 56
RW_TC = LANES
DB_TILES = D_B // LANES


def _pick_tile(n, pref, mult=SUBLANES):
    best = None
    for d in range(mult, min(n, pref) + 1, mult):
        if n % d == 0:
            best = d
    return best if best is not None else n


def _pick_s5_chunk(t_pad, pref=136):
    cands = [d for d in range(SUBLANES, min(t_pad, pref) + 1, SUBLANES) if t_pad % d == 0]
    odd = [d for d in cands if (d // SUBLANES) % 2 == 1]
    return max(odd) if odd else max(cands)


def _cparams(sem):
    return pltpu.CompilerParams(dimension_semantics=sem,
                                vmem_limit_bytes=VMEM_LIMIT_MB * 1024 * 1024)


def _const_spec(shape):
    nd = len(shape)
    return pl.BlockSpec(shape, lambda *_: (0,) * nd, pipeline_mode=pl.Buffered(1))


def _mm(a, b, hp):
    if hp:
        return jnp.dot(a.astype(F32), b, precision=HIGHEST, preferred_element_type=F32)
    return jnp.dot(a.astype(BF16), b, preferred_element_type=F32)


def _segsum(x, ones, hp):
    hi = x.astype(BF16)
    out = jnp.dot(hi, ones, preferred_element_type=F32)
    if hp:
        r1 = x - hi.astype(F32)
        mid = r1.astype(BF16)
        lo = (r1 - mid.astype(F32)).astype(BF16)
        out = out + jnp.dot(mid, ones, preferred_element_type=F32) + jnp.dot(lo, ones, preferred_element_type=F32)
    return out


def _sigmoid(x):
    return 1.0 / (1.0 + jnp.exp(-x))


def _softplus(x):
    return jnp.maximum(x, 0.0) + jnp.log1p(jnp.exp(-jnp.abs(x)))


def _rms(x, g):
    return x * lax.rsqrt(jnp.mean(x * x, axis=-1, keepdims=True) + RMS_EPS) * g


def _s5_disc_kernel(lr_ref, li_ref, ldt_ref, btr_ref, bti_ref, abr_o, abi_o, bbr_o, bbi_o):
    lr = lr_ref[...]
    li = li_ref[...]
    dt = jnp.exp(ldt_ref[...])
    mag = jnp.exp(lr * dt)
    ab_re = mag * jnp.cos(li * dt)
    ab_im = mag * jnp.sin(li * dt)
    den = lr * lr + li * li
    q_re = ((ab_re - 1.0) * lr + ab_im * li) / den
    q_im = (ab_im * lr - (ab_re - 1.0) * li) / den
    abr_o[...] = ab_re
    abi_o[...] = ab_im
    for c in range(S5_GROUP):
        b_re = btr_ref[c]
        b_im = bti_ref[c]
        bbr_o[c] = q_re * b_re - q_im * b_im
        bbi_o[c] = q_re * b_im + q_im * b_re


def _s5_discretize(lam_re, lam_im, log_dt, b_re, b_im):
    gp = jax.ShapeDtypeStruct((S5_GROUPS, S5_STATE), F32)
    cgp = jax.ShapeDtypeStruct((S5_GROUP, S5_GROUPS, S5_STATE), F32)
    return pl.pallas_call(_s5_disc_kernel, out_shape=(gp, gp, cgp, cgp), name="s5_disc")(
        lam_re, lam_im, log_dt[:, None],
        jnp.transpose(b_re, (2, 0, 1)), jnp.transpose(b_im, (2, 0, 1)))


def _s5_kernel(x_ref, gm_ref, wu_ref, bbar_ref, ar_ref, ai_ref, x0r_ref, x0i_ref,
               cre_ref, cim_ref, d_ref, g1_ref, g2_ref,
               ya_ref, xr_out, xi_out, br_scr, bi_scr, sr_scr, si_scr, *rest,
               batch, steps, t_start, hp, reorder):
    c = pl.program_id(0)

    @pl.when(c == 0)
    def _():
        sr_scr[...] = x0r_ref[...]
        si_scr[...] = x0i_ref[...]

    rows = batch * steps
    x = x_ref[...]
    if reorder:
        bm_scr, tm_scr = rest
        x = x.reshape(rows, D_MODEL)
    h = _rms(x, gm_ref[...])
    u = _mm(h, wu_ref[...], hp)

    def to_time_major(t, carry):
        for s in range(S5_TILES):
            tm_scr[s, pl.ds(pl.multiple_of(t * batch, batch), batch), :] = bm_scr[s, pl.ds(t, batch, stride=steps), :]
        return carry

    def to_batch_major(t, carry):
        for s in range(S5_TILES):
            bm_scr[s, pl.ds(t, batch, stride=steps), :] = tm_scr[s, pl.ds(pl.multiple_of(t * batch, batch), batch), :]
        return carry

    if reorder:
        for s in range(S5_TILES):
            bm_scr[s] = u[:, s * LANES:(s + 1) * LANES]
        lax.fori_loop(0, steps, to_time_major, 0)
        u_tm = jnp.concatenate([tm_scr[s] for s in range(S5_TILES)], axis=1)
    else:
        u_tm = u
    half = S5_HALF // S5_TILES
    for j in range(S5_TILES):
        bbj = _mm(u_tm[:, j * LANES:(j + 1) * LANES], bbar_ref[j], hp)
        br_scr[:, j * half:(j + 1) * half] = bbj[:, :half]
        bi_scr[:, j * half:(j + 1) * half] = bbj[:, half:]

    def step(t, carry):
        xr, xi = carry
        rws = pl.ds(pl.multiple_of(t * batch, batch), batch)
        ar = ar_ref[...]
        ai = ai_ref[...]
        nxr = ar * xr - ai * xi + br_scr[rws, :]
        nxi = ar * xi + ai * xr + bi_scr[rws, :]
        br_scr[rws, :] = nxr
        bi_scr[rws, :] = nxi
        return nxr, nxi

    carry = (sr_scr[...], si_scr[...])
    if steps == 1:
        carry = step(0, carry)
    else:
        lo = jnp.clip(t_start - c * steps, 0, steps)
        carry = lax.fori_loop(lo, steps, step, carry)
    sr_scr[...] = carry[0]
    si_scr[...] = carry[1]
    xr_out[...] = carry[0]
    xi_out[...] = carry[1]

    ys = []
    for j in range(S5_TILES):
        ys.append(_mm(br_scr[:, j * half:(j + 1) * half], cre_ref[j], hp)
                  + _mm(bi_scr[:, j * half:(j + 1) * half], cim_ref[j], hp))
    if reorder:
        for s in range(S5_TILES):
            tm_scr[s] = ys[s]
        lax.fori_loop(0, steps, to_batch_major, 0)
        y = jnp.concatenate([bm_scr[s] for s in range(S5_TILES)], axis=1)
    else:
        y = jnp.concatenate(ys, axis=1)
    y = jax.nn.gelu(y + d_ref[...] * u)
    out = _mm(y, g1_ref[...], hp) * _sigmoid(_mm(y, g2_ref[...], hp))
    ya_ref[...] = out.reshape(ya_ref.shape)


def _s5_branch(x, batch, t_len, t_start, P, x0r, x0i, hp):
    seq = x.ndim == 3
    steps = _pick_s5_chunk(t_len) if seq else 1
    nchunks = t_len // steps
    rows = batch * steps
    if seq:
        x_spec = pl.BlockSpec((batch, steps, D_MODEL), lambda c: (0, c, 0))
        scratch_extra = [pltpu.VMEM((S5_TILES, rows, LANES), F32), pltpu.VMEM((S5_TILES, rows, LANES), F32)]
    else:
        x_spec = pl.BlockSpec((batch, D_MODEL), lambda c: (0, 0))
        scratch_extra = []
    state_spec = pl.BlockSpec((batch, S5_HALF), lambda c: (0, 0))
    consts = [P["g_mix"], P["w_u"], P["s5_bbar"], P["s5_ar"][batch], P["s5_ai"][batch], x0r, x0i,
              P["s5_cre"], P["s5_cim"], P["s5_d"], P["s5_glu1"], P["s5_glu2"]]
    return pl.pallas_call(
        functools.partial(_s5_kernel, batch=batch, steps=steps, t_start=t_start, hp=hp, reorder=seq),
        grid=(nchunks,),
        in_specs=[x_spec] + [_const_spec(a.shape) for a in consts],
        out_specs=(x_spec, state_spec, state_spec),
        out_shape=(jax.ShapeDtypeStruct(x.shape, F32),
                   jax.ShapeDtypeStruct((batch, S5_HALF), F32),
                   jax.ShapeDtypeStruct((batch, S5_HALF), F32)),
        scratch_shapes=[pltpu.VMEM((rows, S5_HALF), F32), pltpu.VMEM((rows, S5_HALF), F32),
                        pltpu.VMEM((batch, S5_HALF), F32), pltpu.VMEM((batch, S5_HALF), F32)] + scratch_extra,
        compiler_params=_cparams(("arbitrary",)),
        name="s5_branch_hp" if hp else "s5_branch",
    )(x, *consts)


N_KVEC = 5
N_VEC = N_KVEC + 1


def _rw_prep_kernel(x_ref, gm_ref, wrw_ref, sh0_ref, mu_ref, wl_ref, w0_ref, a0_ref, gup_ref,
                    kk_ref, ka_ref, rk_ref, ones_ref,
                    vec_o, g_o, bon_o, sh_o, carry_scr, *, seq, t_start, hp):
    h = _rms(x_ref[...], gm_ref[...])
    p = _mm(h, wrw_ref[...], hp)
    rows = p.shape[0]
    if seq:
        c = pl.program_id(1)

        @pl.when(c == 0)
        def _():
            carry_scr[...] = jnp.zeros_like(carry_scr)

        row = lax.broadcasted_iota(jnp.int32, p.shape, 0)
        prev = jnp.where(row == 0, carry_scr[0:1, :], pltpu.roll(p, 1, 0))
        prev = jnp.where(row + c * rows == t_start, sh0_ref[...], prev)
        carry_scr[0:1, :] = p[rows - 1:rows, :]

        @pl.when(c == pl.num_programs(1) - 1)
        def _():
            sh_o[...] = p[rows - 1:rows, :]
    else:
        prev = sh0_ref[...]
        sh_o[...] = p
    z = p + (prev - p) * mu_ref[...]
    r = z[:, :D_B]
    k = z[:, D_B:2 * D_B]
    v = z[:, 2 * D_B:3 * D_B]
    zwa = z[:, 3 * D_B:3 * D_B + LORA_W + LORA_A]
    zg = z[:, 3 * D_B + LORA_W + LORA_A:]
    lane = lax.broadcasted_iota(jnp.int32, zwa.shape, 1)
    tw = jnp.where(lane < LORA_W, jnp.tanh(zwa), zwa)
    lw = _mm(tw, wl_ref[...], hp)
    w_log = -_softplus(-(w0_ref[...] + lw[:, :D_B])) - 0.5
    decay = jnp.exp(-jnp.exp(w_log))
    a = _sigmoid(a0_ref[...] + lw[:, D_B:])
    g = _mm(_sigmoid(zg), gup_ref[...], hp)
    kk = k * kk_ref[...]
    n2 = _segsum(kk * kk, ones_ref[...], hp)
    kkn = kk * lax.rsqrt(jnp.maximum(n2, 1e-24))
    k2 = k * (1.0 + (a - 1.0) * ka_ref[...])
    rk = _segsum(r * k2 * rk_ref[...], ones_ref[...], hp)
    vec_o[0] = kkn
    vec_o[1] = decay
    vec_o[2] = -(kkn * a)
    vec_o[3] = k2
    vec_o[4] = r
    vec_o[5] = v
    g_o[...] = g
    bon_o[...] = rk * v


def _rw_prep(x, P, sh0, t_start, hp):
    seq = x.ndim == 3
    consts = [P["g_mix"], P["w_rw"]]
    consts2 = [P["rw_mu"], P["rw_wl"], P["rw_w0"], P["rw_a0"], P["rw_g_up"],
               P["rw_k_k"], P["rw_k_a"], P["rw_r_k"], P["ones_head"]]
    if seq:
        batch, t_len, _ = x.shape
        tc = _pick_tile(t_len, 544)
        grid = (batch, t_len // tc)
        row_spec = lambda w: pl.BlockSpec((None, tc, w), lambda b, c: (b, c, 0))
        vec_spec = pl.BlockSpec((N_VEC, None, tc, D_B), lambda b, c: (0, b, c, 0))
        sh_spec = pl.BlockSpec((None, 1, P_RW), lambda b, c: (b, 0, 0))
        sh0 = sh0.reshape(batch, 1, P_RW)
        vec_shape = (N_VEC, batch, t_len, D_B)
        g_shape = (batch, t_len, D_B)
        sh_shape = (batch, 1, P_RW)
        sem = ("arbitrary", "arbitrary")
    else:
        batch = x.shape[0]
        grid = (1,)
        row_spec = lambda w: pl.BlockSpec((batch, w), lambda i: (0, 0))
        vec_spec = pl.BlockSpec((N_VEC, batch, D_B), lambda i: (0, 0, 0))
        sh_spec = row_spec(P_RW)
        vec_shape = (N_VEC, batch, D_B)
        g_shape = (batch, D_B)
        sh_shape = (batch, P_RW)
        sem = ("arbitrary",)
    return pl.pallas_call(
        functools.partial(_rw_prep_kernel, seq=seq, t_start=t_start, hp=hp),
        grid=grid,
        in_specs=[row_spec(D_MODEL)] + [_const_spec(a.shape) for a in consts] + [sh_spec]
                 + [_const_spec(a.shape) for a in consts2],
        out_specs=(vec_spec, row_spec(D_B), row_spec(D_B), sh_spec),
        out_shape=(jax.ShapeDtypeStruct(vec_shape, F32), jax.ShapeDtypeStruct(g_shape, F32),
                   jax.ShapeDtypeStruct(g_shape, F32), jax.ShapeDtypeStruct(sh_shape, F32)),
        scratch_shapes=[pltpu.VMEM((SUBLANES, P_RW), F32)],
        compiler_params=_cparams(sem),
        name="rw_prep_hp" if hp else "rw_prep",
    )(x, *consts, sh0, *consts2)


RW_VH = RW_HEAD // (2 * SUBLANES)
RW_PAIRS = SUBLANES * RW_HEADS
RW_SC = RW_TC // 2


RW_QP = RW_HEAD + 4


def _pair_rows(i):
    return pl.ds(i, RW_PAIRS, stride=RW_QP)


def _pairs_to_rows(x_ref, q_scr, batch):
    for b in range(batch):
        for j in range(DB_TILES):
            tile = x_ref[b, :, j * LANES:(j + 1) * LANES].T
            pair = b * RW_HEADS + 2 * j
            q_scr[pl.ds(pair * RW_QP, RW_HEAD), :] = tile[:RW_HEAD]
            q_scr[pl.ds((pair + 1) * RW_QP, RW_HEAD), :] = tile[RW_HEAD:]


def _rw_kin_kernel(x_ref, o_ref, q_scr, *, batch):
    _pairs_to_rows(x_ref, q_scr, batch)
    for k in range(RW_HEAD):
        m = q_scr[_pair_rows(k), :]
        o_ref[k] = jnp.concatenate([m, m], axis=0).T


def _rw_vin_kernel(x_ref, o_ref, q_scr, *, batch):
    _pairs_to_rows(x_ref, q_scr, batch)
    for vh in range(RW_VH):
        for vs in range(SUBLANES):
            v0 = (vh * SUBLANES + vs) * 2
            pair_tile = jnp.concatenate([q_scr[_pair_rows(v0), :], q_scr[_pair_rows(v0 + 1), :]], axis=0)
            o_ref[vh, pl.ds(vs, RW_TC, stride=SUBLANES), :] = pair_tile.T


def _rw_unlayout_kernel(o3_ref, o_ref, q_scr, *, batch):
    for vh in range(RW_VH):
        for vs in range(SUBLANES):
            v0 = (vh * SUBLANES + vs) * 2
            zt = o3_ref[vh, pl.ds(vs, RW_TC, stride=SUBLANES), :].T
            q_scr[_pair_rows(v0), :] = zt[:RW_PAIRS]
            q_scr[_pair_rows(v0 + 1), :] = zt[RW_PAIRS:]
    for b in range(batch):
        for j in range(DB_TILES):
            pair = b * RW_HEADS + 2 * j
            tile = jnp.concatenate([q_scr[pl.ds(pair * RW_QP, RW_HEAD), :],
                                    q_scr[pl.ds((pair + 1) * RW_QP, RW_HEAD), :]], axis=0)
            o_ref[b, :, j * LANES:(j + 1) * LANES] = tile.T


def _rw_scan_kernel(kin_ref, vin_ref, s0_ref, o_ref, sfin_ref, s_scr, *, t_start):
    c = pl.program_id(0)

    @pl.when(c == 0)
    def _():
        s_scr[...] = s0_ref[...]

    def krow(t, vec, k):
        return jnp.broadcast_to(kin_ref[vec * RW_HEAD + k, pl.ds(t, 1), :], (SUBLANES, LANES))

    def acc_add(acc, vh, k, x):
        prev = acc[vh][k % 2]
        acc[vh][k % 2] = x if prev is None else prev + x

    def first_sa(t):
        acc = [[None, None] for _ in range(RW_VH)]
        for k in range(RW_HEAD):
            kkb = krow(t, 0, k)
            for vh in range(RW_VH):
                acc_add(acc, vh, k, s_scr[vh, k] * kkb)
        return tuple(a[0] + a[1] for a in acc)

    def step(t, sa):
        tn = jnp.minimum(t + 1, RW_SC - 1)
        tv = pl.multiple_of(t * SUBLANES, SUBLANES)
        vv = [vin_ref[vh, pl.ds(tv, SUBLANES), :] for vh in range(RW_VH)]
        oacc = [[None, None] for _ in range(RW_VH)]
        nacc = [[None, None] for _ in range(RW_VH)]
        for k in range(RW_HEAD):
            wb = krow(t, 1, k)
            nbb = krow(t, 2, k)
            kb = krow(t, 3, k)
            rb = krow(t, 4, k)
            kkn = krow(tn, 0, k)
            for vh in range(RW_VH):
                s = s_scr[vh, k] * wb + sa[vh] * nbb + vv[vh] * kb
                s_scr[vh, k] = s
                acc_add(oacc, vh, k, s * rb)
                acc_add(nacc, vh, k, s * kkn)
        for vh in range(RW_VH):
            o_ref[vh, pl.ds(tv, SUBLANES), :] = oacc[vh][0] + oacc[vh][1]
        return tuple(a[0] + a[1] for a in nacc)

    lo = jnp.clip(t_start - c * RW_SC, 0, RW_SC)

    @pl.when(lo > 0)
    def _():
        o_ref[...] = jnp.zeros_like(o_ref)

    lax.fori_loop(lo, RW_SC, step, first_sa(jnp.minimum(lo, RW_SC - 1)))
    sfin_ref[...] = s_scr[...]


def _rw_recurrence_long(vecs, s0, t_start):
    _, batch, t_len, _ = vecs.shape
    assert batch == SUBLANES and t_len % RW_TC == 0
    nchunks = t_len // RW_TC
    q_scr = pltpu.VMEM((RW_PAIRS * RW_QP, LANES), F32)
    kin = pl.pallas_call(
        functools.partial(_rw_kin_kernel, batch=batch),
        grid=(nchunks, N_KVEC),
        in_specs=[pl.BlockSpec((None, batch, RW_TC, D_B), lambda c, i: (i, 0, c, 0))],
        out_specs=pl.BlockSpec((None, RW_HEAD, RW_TC, LANES), lambda c, i: (c, i, 0, 0)),
        out_shape=jax.ShapeDtypeStruct((nchunks, N_KVEC * RW_HEAD, RW_TC, LANES), F32),
        scratch_shapes=[q_scr],
        compiler_params=_cparams(("arbitrary", "arbitrary")),
        name="rw_kin",
    )(vecs)
    vin = pl.pallas_call(
        functools.partial(_rw_vin_kernel, batch=batch),
        grid=(nchunks,),
        in_specs=[pl.BlockSpec((None, batch, RW_TC, D_B), lambda c: (N_KVEC, 0, c, 0))],
        out_specs=pl.BlockSpec((None, RW_VH, RW_TC * SUBLANES, LANES), lambda c: (c, 0, 0, 0)),
        out_shape=jax.ShapeDtypeStruct((nchunks, RW_VH, RW_TC * SUBLANES, LANES), F32),
        scratch_shapes=[q_scr],
        compiler_params=_cparams(("arbitrary",)),
        name="rw_vin",
    )(vecs)
    sshape = (RW_VH, RW_HEAD, SUBLANES, LANES)
    s = s0.reshape(batch, RW_HEADS, RW_VH, SUBLANES, 2, RW_HEAD)
    s = jnp.transpose(s, (2, 5, 3, 4, 0, 1)).reshape(sshape)
    state_spec = pl.BlockSpec(sshape, lambda c: (0, 0, 0, 0))
    halves = RW_TC // RW_SC
    vblk = (None, RW_VH, RW_SC * SUBLANES, LANES)
    vmap = lambda c: (c // halves, 0, c % halves, 0)
    o3, sfin = pl.pallas_call(
        functools.partial(_rw_scan_kernel, t_start=t_start),
        grid=(nchunks * halves,),
        in_specs=[pl.BlockSpec((None, N_KVEC * RW_HEAD, RW_SC, LANES), vmap),
                  pl.BlockSpec(vblk, vmap), state_spec],
        out_specs=(pl.BlockSpec(vblk, vmap), state_spec),
        out_shape=(jax.ShapeDtypeStruct((nchunks, RW_VH, RW_TC * SUBLANES, LANES), F32),
                   jax.ShapeDtypeStruct(sshape, F32)),
        scratch_shapes=[pltpu.VMEM(sshape, F32)],
        compiler_params=_cparams(("arbitrary",)),
        name="rw_scan",
    )(kin, vin, s)
    o = pl.pallas_call(
        functools.partial(_rw_unlayout_kernel, batch=batch),
        grid=(nchunks,),
        in_specs=[pl.BlockSpec((None, RW_VH, RW_TC * SUBLANES, LANES), lambda c: (c, 0, 0, 0))],
        out_specs=pl.BlockSpec((batch, RW_TC, D_B), lambda c: (0, c, 0)),
        out_shape=jax.ShapeDtypeStruct((batch, t_len, D_B), F32),
        scratch_shapes=[q_scr],
        compiler_params=_cparams(("arbitrary",)),
        name="rw_unlayout",
    )(o3)
    sfin = sfin.reshape(RW_VH, RW_HEAD, SUBLANES, 2, batch, RW_HEADS)
    sfin = jnp.transpose(sfin, (4, 5, 0, 2, 3, 1)).reshape(batch, RW_HEADS, RW_HEAD, RW_HEAD)
    return o, sfin


STEP_PAIRS = 64
STEP_VROWS = RW_HEAD // 2


def _split3(x):
    hi = x.astype(BF16)
    r1 = x - hi.astype(F32)
    mid = r1.astype(BF16)
    return hi, mid, (r1 - mid.astype(F32)).astype(BF16)


def _rw_step_kernel(s_ref, km_ref, v_ref, sel_ref, vmask_ref, vexp_ref, ones_ref, s_o, o_o):
    sel = sel_ref[...]

    def rows_of(x):
        return sum(jnp.dot(sel, p, preferred_element_type=F32) for p in _split3(x))

    kx = rows_of(km_ref[...])
    kk, w, nb, k, r = [kx[:, i * LANES:(i + 1) * LANES] for i in range(N_KVEC)]
    vx = _segsum(rows_of(v_ref[...]) * vmask_ref[...], vexp_ref[...], True)
    s = s_ref[...]
    ones = ones_ref[...]
    sa = _segsum(s * kk, ones, True)
    s = s * w + sa * nb + vx * k
    s_o[...] = s
    o_o[...] = _segsum(s * r, ones, True)


def _rw_recurrence_step(vecs, s0, ones_half):
    batch = vecs.shape[1]
    pairs = batch * RW_HEADS
    rows = pairs * STEP_VROWS
    tr = STEP_PAIRS * STEP_VROWS
    assert pairs % STEP_PAIRS == 0
    kp = vecs[:N_KVEC].reshape(N_KVEC, pairs, 1, RW_HEAD)
    km = jnp.transpose(jnp.broadcast_to(kp, (N_KVEC, pairs, 2, RW_HEAD)), (1, 0, 2, 3)).reshape(pairs, N_KVEC * LANES)
    vp = vecs[N_KVEC].reshape(pairs, RW_HEAD)
    ridx = jnp.arange(tr, dtype=jnp.int32)
    sel = (ridx[:, None] // STEP_VROWS == jnp.arange(STEP_PAIRS, dtype=jnp.int32)[None, :]).astype(BF16)
    vidx = jnp.arange(RW_HEAD, dtype=jnp.int32)
    vmask = (vidx[None, :] // 2 == ridx[:, None] % STEP_VROWS).astype(F32)
    vexp = (vidx[:, None] % 2 == jnp.arange(LANES, dtype=jnp.int32)[None, :] // RW_HEAD).astype(BF16)
    spec = pl.BlockSpec((tr, LANES), lambda i: (i, 0))
    big = jax.ShapeDtypeStruct((rows, LANES), F32)
    consts = [sel, vmask, vexp, ones_half]
    s_new, o = pl.pallas_call(
        _rw_step_kernel,
        grid=(pairs // STEP_PAIRS,),
        in_specs=[spec, pl.BlockSpec((STEP_PAIRS, N_KVEC * LANES), lambda i: (i, 0)),
                  pl.BlockSpec((STEP_PAIRS, RW_HEAD), lambda i: (i, 0))]
                 + [_const_spec(a.shape) for a in consts],
        out_specs=(spec, spec),
        out_shape=(big, big),
        compiler_params=_cparams(("arbitrary",)),
        name="rw_step",
    )(s0.reshape(rows, LANES), km, vp, *consts)
    o = o.reshape(rows, 2, RW_HEAD)[:, :, 0].reshape(batch, D_B)
    return o, s_new.reshape(batch, RW_HEADS, RW_HEAD, RW_HEAD)


def _mix_kernel(*refs, hp, moe):
    (x_ref, ya_ref, o_ref, bon_ref, g_ref, gm_ref, wg_ref, gng_ref, gnb_ref, ones_ref,
     wbo_ref, wout_ref, gf_ref) = refs[:13]
    if moe:
        wr_ref, x1_o, h2_o, route_o = refs[13:]
    else:
        x1_o, h2_o = refs[13:]
    x = x_ref[...]
    h = _rms(x, gm_ref[...])
    gates = _mm(h, wg_ref[...], hp)
    o = o_ref[...]
    ones = ones_ref[...]
    inv_n = 1.0 / RW_HEAD
    mean = _segsum(o, ones, hp) * inv_n
    dlt = o - mean
    var = _segsum(dlt * dlt, ones, hp) * inv_n
    on = dlt * lax.rsqrt(var + GN_EPS) * gng_ref[...] + gnb_ref[...]
    yb = _mm((on + bon_ref[...]) * g_ref[...], wbo_ref[...], hp)
    m = _sigmoid(gates[:, :D_MODEL]) * ya_ref[...] + _sigmoid(gates[:, D_MODEL:]) * yb
    x1 = x + _mm(m, wout_ref[...], hp)
    h2 = _rms(x1, gf_ref[...])
    x1_o[...] = x1
    h2_o[...] = h2.astype(h2_o.dtype)
    if moe:
        logits = _mm(h2, wr_ref[...], hp)
        lane = lax.broadcasted_iota(jnp.int32, logits.shape, 1).astype(F32)
        neg = jnp.float32(-jnp.inf)
        lg = jnp.where(lane < N_EXPERTS, logits, neg)
        m1 = jnp.max(lg, axis=1, keepdims=True)
        i1 = jnp.min(jnp.where(lg == m1, lane, float(LANES)), axis=1, keepdims=True)
        lg2 = jnp.where(lane == i1, neg, lg)
        m2 = jnp.max(lg2, axis=1, keepdims=True)
        i2 = jnp.min(jnp.where(lg2 == m2, lane, float(LANES)), axis=1, keepdims=True)
        e = jnp.exp(m2 - m1)
        g1 = 1.0 / (1.0 + e)
        g2 = e / (1.0 + e)
        route_o[...] = jnp.where(lane == 0.0, i1, jnp.where(lane == 1.0, i2,
                                 jnp.where(lane == 2.0, g1, jnp.where(lane == 3.0, g2, 0.0))))


def _mix(x, ya, o, bon, g, P, hp, moe):
    rows = x.shape[0]
    tm = _pick_tile(rows, 344)
    row_spec = lambda w: pl.BlockSpec((tm, w), lambda i: (i, 0))
    consts = [P["g_mix"], P["w_gate"], P["rw_gn_g"], P["rw_gn_b"], P["ones_head"],
              P["rw_w_bo"], P["w_out"], P["g_ffn"]]
    if moe:
        consts.append(P["w_router"])
    h2_dtype = F32 if (hp or moe) else BF16
    out_specs = [row_spec(D_MODEL), row_spec(D_MODEL)]
    out_shape = [jax.ShapeDtypeStruct((rows, D_MODEL), F32), jax.ShapeDtypeStruct((rows, D_MODEL), h2_dtype)]
    if moe:
        out_specs.append(row_spec(LANES))
        out_shape.append(jax.ShapeDtypeStruct((rows, LANES), F32))
    return pl.pallas_call(
        functools.partial(_mix_kernel, hp=hp, moe=moe),
        grid=(rows // tm,),
        in_specs=[row_spec(D_MODEL), row_spec(D_MODEL), row_spec(D_B), row_spec(D_B), row_spec(D_B)]
                 + [_const_spec(a.shape) for a in consts],
        out_specs=tuple(out_specs),
        out_shape=tuple(out_shape),
        compiler_params=_cparams(("arbitrary",)),
        name=("mix_moe" if moe else "mix") + ("_hp" if hp else ""),
    )(x, ya, o, bon, g, *consts)


def _ffn_kernel(h_ref, x1_ref, w1_ref, w3_ref, w2_ref, o_ref, *, hp):
    h = h_ref[...]
    a = _mm(h, w1_ref[...], hp)
    b = _mm(h, w3_ref[...], hp)
    o_ref[...] = x1_ref[...] + _mm(a * _sigmoid(a) * b, w2_ref[...], hp)


def _ffn(h2, x1, w1, w3, w2, tm, hp):
    rows = h2.shape[0]
    assert rows % tm == 0
    row_spec = pl.BlockSpec((tm, D_MODEL), lambda i: (i, 0))
    return pl.pallas_call(
        functools.partial(_ffn_kernel, hp=hp),
        grid=(rows // tm,),
        in_specs=[row_spec, row_spec, _const_spec(w1.shape), _const_spec(w3.shape), _const_spec(w2.shape)],
        out_specs=row_spec,
        out_shape=jax.ShapeDtypeStruct((rows, D_MODEL), F32),
        compiler_params=_cparams(("arbitrary",)),
        name="ffn_hp" if hp else "ffn",
    )(h2, x1, w1, w3, w2)


MOE_TM = 512
MOE_TF = 1792


def _moe_kernel(te_ref, nv_ref, x_ref, w1_ref, w3_ref, w2_ref, o_ref, acc):
    i = pl.program_id(0)
    f = pl.program_id(1)

    @pl.when(f == 0)
    def _():
        acc[...] = jnp.zeros_like(acc)

    @pl.when(i < nv_ref[0])
    def _():
        x = x_ref[...].astype(BF16)
        a = jnp.dot(x, w1_ref[...], preferred_element_type=F32)
        b = jnp.dot(x, w3_ref[...], preferred_element_type=F32)
        acc[...] += jnp.dot((a * _sigmoid(a) * b).astype(BF16), w2_ref[...], preferred_element_type=F32)

    @pl.when(f == pl.num_programs(1) - 1)
    def _():
        o_ref[...] = acc[...]


def _moe_experts(xs, tile_expert, n_valid, w1, w3, w2):
    rows = xs.shape[0]
    ntiles = rows // MOE_TM
    nf = D_EXPERT // MOE_TF
    assert ntiles * MOE_TM == rows and nf * MOE_TF == D_EXPERT

    def fblk(i, f, nv):
        walk = lambda t, g: jnp.where(t % 2 == 0, g, nf - 1 - g)
        return jnp.where(i < nv[0], walk(i, f), walk(nv[0] - 1, nf - 1))

    grid_spec = pltpu.PrefetchScalarGridSpec(
        num_scalar_prefetch=2,
        grid=(ntiles, nf),
        in_specs=[pl.BlockSpec((MOE_TM, D_MODEL), lambda i, f, te, nv: (i, 0)),
                  pl.BlockSpec((None, D_MODEL, MOE_TF), lambda i, f, te, nv: (te[i], 0, fblk(i, f, nv))),
                  pl.BlockSpec((None, D_MODEL, MOE_TF), lambda i, f, te, nv: (te[i], 0, fblk(i, f, nv))),
                  pl.BlockSpec((None, MOE_TF, D_MODEL), lambda i, f, te, nv: (te[i], fblk(i, f, nv), 0))],
        out_specs=pl.BlockSpec((MOE_TM, D_MODEL), lambda i, f, te, nv: (i, 0)),
        scratch_shapes=[pltpu.VMEM((MOE_TM, D_MODEL), F32)])
    return pl.pallas_call(
        _moe_kernel,
        grid_spec=grid_spec,
        out_shape=jax.ShapeDtypeStruct((rows, D_MODEL), F32),
        compiler_params=_cparams(("arbitrary", "arbitrary")),
        name="moe_experts",
    )(tile_expert, n_valid, xs, w1, w3, w2)


def _moe_plan(expert_idx, src_row):
    n_pairs = expert_idx.shape[0] * 2
    flat_e = expert_idx.reshape(n_pairs)
    onehot = (flat_e[:, None] == jnp.arange(N_EXPERTS, dtype=jnp.int32)[None, :]).astype(jnp.int32)
    csum = jnp.cumsum(onehot, axis=0)
    rank = jnp.sum(csum * onehot, axis=1) - 1
    counts = csum[-1]
    padded = ((counts + MOE_TM - 1) // MOE_TM) * MOE_TM
    pend = jnp.cumsum(padded)
    pstart = pend - padded
    dest = pstart[flat_e] + rank
    ntiles = (n_pairs + N_EXPERTS * (MOE_TM - 1)) // MOE_TM + 1
    rows = ntiles * MOE_TM
    tile_start = jnp.arange(ntiles, dtype=jnp.int32) * MOE_TM
    n_valid = (pend[-1] // MOE_TM).astype(jnp.int32)
    te = jnp.sum((pend[None, :] <= tile_start[:, None]).astype(jnp.int32), axis=1)
    te = jnp.minimum(te, N_EXPERTS - 1)
    order = jnp.argsort(flat_e, stable=True).astype(jnp.int32)
    gstart = jnp.cumsum(counts) - counts
    slot = jnp.arange(rows, dtype=jnp.int32)
    e_slot = jnp.repeat(te, MOE_TM)
    off = slot - pstart[e_slot]
    filled = (off < counts[e_slot]) & (slot < pend[-1])
    pair = order[jnp.clip(gstart[e_slot] + off, 0, n_pairs - 1)]
    src = jnp.where(filled, src_row[pair // 2], 0).astype(jnp.int32)
    last_e = te[jnp.maximum(n_valid - 1, 0)]
    te = jnp.where(jnp.arange(ntiles) < n_valid, te, last_e).astype(jnp.int32)
    return src, dest.reshape(-1, 2), te, n_valid.reshape(1)


def _combine_kernel(x1_ref, y1_ref, y2_ref, route_ref, gf_ref, o_ref):
    route = route_ref[...]
    x2 = x1_ref[...] + route[:, 2:3] * y1_ref[...] + route[:, 3:4] * y2_ref[...]
    o_ref[...] = _rms(x2, gf_ref[...])


def _combine_norm(x1, x1_spec, flat_blk0, grid, out_shape, out_spec, y1, y2, route, g_final, tm):
    nd = len(grid)

    def flat_map(*idx):
        lin = idx[0]
        for k in range(1, nd):
            lin = lin * grid[k] + idx[k]
        return (flat_blk0 + lin, 0)

    return pl.pallas_call(
        _combine_kernel,
        grid=grid,
        in_specs=[x1_spec, pl.BlockSpec((tm, D_MODEL), flat_map), pl.BlockSpec((tm, D_MODEL), flat_map),
                  pl.BlockSpec((tm, LANES), flat_map), _const_spec(g_final.shape)],
        out_specs=out_spec,
        out_shape=out_shape,
        compiler_params=_cparams(("arbitrary",) * nd),
        name="combine_norm",
    )(x1, y1, y2, route, g_final)


MATMUL_WEIGHTS = ("w_u", "w_rw", "w_gate", "s5_glu1", "s5_glu2", "rw_g_up", "rw_w_bo", "w_out", "rw_wl",
                  "s5_bbar", "s5_cre", "s5_cim", "w_router")


def _layer_params(l, W, hp, batches):
    wdt = F32 if hp else BF16
    row = lambda a: a.reshape(1, -1).astype(F32)
    w_in = W["w_in"][l]
    P = {
        "g_mix": row(W["norm_mix"][l]),
        "g_ffn": row(W["norm_ffn"][l]),
        "w_u": w_in[:, :D_A].astype(wdt),
        "w_rw": w_in[:, D_A:D_A + P_RW].astype(wdt),
        "w_gate": w_in[:, D_A + P_RW:].astype(wdt),
        "s5_d": row(W["s5_d"][l]),
        "s5_glu1": W["s5_glu1"][l].astype(wdt),
        "s5_glu2": W["s5_glu2"][l].astype(wdt),
        "rw_mu": row(W["rw_mu"][l]),
        "rw_w0": row(W["rw_w0"][l]),
        "rw_a0": row(W["rw_a0"][l]),
        "rw_g_up": W["rw_g_up"][l].astype(wdt),
        "rw_k_k": row(W["rw_k_k"][l]),
        "rw_k_a": row(W["rw_k_a"][l]),
        "rw_r_k": row(W["rw_r_k"][l]),
        "rw_gn_g": row(W["rw_gn_g"][l]),
        "rw_gn_b": row(W["rw_gn_b"][l]),
        "rw_w_bo": W["rw_w_bo"][l].astype(wdt),
        "w_out": W["w_out"][l].astype(wdt),
    }
    wl = jnp.zeros((LORA_W + LORA_A, 2 * D_B), F32)
    wl = wl.at[:LORA_W, :D_B].set(W["rw_w_up"][l]).at[LORA_W:, D_B:].set(W["rw_a_up"][l])
    P["rw_wl"] = wl.astype(wdt)
    P["ones_head"] = jnp.kron(jnp.eye(RW_HEADS, dtype=F32), jnp.ones((RW_HEAD, RW_HEAD), F32)).astype(BF16)

    ab_re, ab_im, bb_re, bb_im = _s5_discretize(W["s5_lam_re"][l], W["s5_lam_im"][l], W["s5_log_dt"][l],
                                                W["s5_b_re"][l], W["s5_b_im"][l])
    P["s5_ar"] = {b: jnp.broadcast_to(ab_re.reshape(1, S5_HALF), (b, S5_HALF)) for b in batches}
    P["s5_ai"] = {b: jnp.broadcast_to(ab_im.reshape(1, S5_HALF), (b, S5_HALF)) for b in batches}
    gpt = S5_GROUPS // S5_TILES
    eye = jnp.eye(gpt, dtype=F32)
    bb = jnp.stack([bb_re, bb_im]).reshape(2, S5_GROUP, S5_TILES, gpt, S5_STATE)
    bb = jnp.transpose(bb, (2, 1, 0, 3, 4))
    bbar = bb[:, None] * eye[None, :, None, None, :, None]
    P["s5_bbar"] = bbar.reshape(S5_TILES, LANES, 2 * gpt * S5_STATE).astype(wdt)
    cs = jnp.stack([W["s5_c_re"][l], -W["s5_c_im"][l]]).reshape(2, S5_TILES, gpt, S5_GROUP, S5_STATE)
    cs = jnp.transpose(cs, (1, 0, 2, 4, 3))
    cm = cs[:, :, :, :, None, :] * eye[None, None, :, None, :, None]
    cm = cm.reshape(S5_TILES, 2, gpt * S5_STATE, LANES).astype(wdt)
    P["s5_cre"] = cm[:, 0]
    P["s5_cim"] = cm[:, 1]
    return P


def _mixer_seq(x, t_start, s5r0, s5i0, rw_s0, rw_sh0, P, moe):
    batch, t_len, _ = x.shape
    ya, xr, xi = _s5_branch(x, batch, t_len, t_start, P, s5r0, s5i0, False)
    vecs, g, bon, sh = _rw_prep(x, P, rw_sh0, t_start, False)
    o, s_fin = _rw_recurrence_long(vecs, rw_s0, t_start)
    flat = lambda a: a.reshape(batch * t_len, a.shape[-1])
    outs = _mix(flat(x), flat(ya), flat(o), flat(bon), flat(g), P, False, moe)
    return outs, (xr, xi, s_fin, sh.reshape(batch, P_RW))


def _mixer_step(x, s5r0, s5i0, rw_s0, rw_sh0, P, moe, ones_half):
    batch = x.shape[0]
    ya, xr, xi = _s5_branch(x, batch, 1, 0, P, s5r0, s5i0, True)
    vecs, g, bon, sh = _rw_prep(x, P, rw_sh0, 0, True)
    o, s_fin = _rw_recurrence_step(vecs, rw_s0, ones_half)
    outs = _mix(x, ya, o, bon, g, P, True, moe)
    return outs, (xr, xi, s_fin, sh)


def kernel(x_prompt, x_sample, state_s5_re, state_s5_im, state_rwkv, state_shift, meta_tokens, norm_mix, w_in, s5_lam_re, s5_lam_im, s5_log_dt, s5_b_re, s5_b_im, s5_c_re, s5_c_im, s5_d, s5_glu1, s5_glu2, rw_mu, rw_w0, rw_w_up, rw_a0, rw_a_up, rw_g_up, rw_k_k, rw_k_a, rw_r_k, rw_gn_g, rw_gn_b, rw_w_bo, w_out, norm_ffn, ffn_w1, ffn_w3, ffn_w2, moe_router, moe_w1, moe_w3, moe_w2, norm_final):
    W = dict(norm_mix=norm_mix, w_in=w_in, s5_lam_re=s5_lam_re, s5_lam_im=s5_lam_im,
             s5_log_dt=s5_log_dt, s5_b_re=s5_b_re, s5_b_im=s5_b_im, s5_c_re=s5_c_re,
             s5_c_im=s5_c_im, s5_d=s5_d, s5_glu1=s5_glu1, s5_glu2=s5_glu2, rw_mu=rw_mu,
             rw_w0=rw_w0, rw_w_up=rw_w_up, rw_a0=rw_a0, rw_a_up=rw_a_up, rw_g_up=rw_g_up,
             rw_k_k=rw_k_k, rw_k_a=rw_k_a, rw_r_k=rw_r_k, rw_gn_g=rw_gn_g, rw_gn_b=rw_gn_b,
             rw_w_bo=rw_w_bo, w_out=w_out, norm_ffn=norm_ffn)
    bp, seq, _ = x_prompt.shape
    bs = x_sample.shape[0]
    assert x_sample.shape[1] == 1 and seq % RW_TC == 0 and bp == SUBLANES
    tp = N_META + seq
    t_pad = -(-tp // RW_TC) * RW_TC
    t_start = t_pad - tp
    out0 = t_pad - seq

    meta = jnp.broadcast_to(meta_tokens.astype(F32)[None], (bp, N_META, D_MODEL))
    xp = jnp.concatenate([jnp.zeros((bp, t_start, D_MODEL), F32), meta, x_prompt], axis=1)
    xs = x_sample.reshape(bs, D_MODEL)
    ones_half = jnp.kron(jnp.eye(2, dtype=F32), jnp.ones((RW_HEAD, RW_HEAD), F32)).astype(BF16)
    g_final = norm_final.reshape(1, D_MODEL).astype(F32)

    zero_s5 = jnp.zeros((bp, S5_HALF), F32)
    zero_rw = jnp.zeros((bp, RW_HEADS, RW_HEAD, RW_HEAD), F32)
    zero_sh = jnp.zeros((bp, P_RW), F32)

    p_states, s_states = [], []
    for l in range(DEPTH):
        moe = (l % 2 == 1)
        j = l // 2
        Ph = _layer_params(l, W, True, (bp, bs))
        if moe:
            Ph["w_router"] = jnp.zeros((D_MODEL, LANES), F32).at[:, :N_EXPERTS].set(moe_router[j])
        Pl = {k: (v.astype(BF16) if k in MATMUL_WEIGHTS else v) for k, v in Ph.items()}
        outs_p, st_p = _mixer_seq(xp, t_start, zero_s5, zero_s5, zero_rw, zero_sh, Pl, moe)
        outs_s, st_s = _mixer_step(xs, state_s5_re[l].reshape(bs, S5_HALF), state_s5_im[l].reshape(bs, S5_HALF),
                                   state_rwkv[l], state_shift[l], Ph, moe, ones_half)
        p_states.append(st_p)
        s_states.append(st_s)
        if not moe:
            rows = bp * t_pad
            xp = _ffn(outs_p[1], outs_p[0], ffn_w1[j].astype(BF16), ffn_w3[j].astype(BF16),
                      ffn_w2[j].astype(BF16), _pick_tile(rows, 544), False).reshape(bp, t_pad, D_MODEL)
            xs = _ffn(outs_s[1], outs_s[0], ffn_w1[j], ffn_w3[j], ffn_w2[j], bs, True)
        else:
            assert l == DEPTH - 1
            n_p = bp * seq
            h2 = jnp.concatenate([outs_p[1], outs_s[1]], axis=0)
            route_p = outs_p[2].reshape(bp, t_pad, LANES)[:, out0:].reshape(n_p, LANES)
            route = jnp.concatenate([route_p, outs_s[2]], axis=0)
            tok = jnp.arange(n_p, dtype=jnp.int32)
            src_row = jnp.concatenate([(tok // seq) * t_pad + out0 + tok % seq,
                                       bp * t_pad + jnp.arange(bs, dtype=jnp.int32)])
            src, dest, te, n_valid = _moe_plan(route[:, :2].astype(jnp.int32), src_row)
            take = functools.partial(jnp.take, axis=0, mode="clip")
            y_sorted = _moe_experts(take(h2, src), te, n_valid,
                                    moe_w1[j].astype(BF16), moe_w3[j].astype(BF16), moe_w2[j].astype(BF16))
            y1 = take(y_sorted, dest[:, 0])
            y2 = take(y_sorted, dest[:, 1])
            tm = RW_TC
            nblk = seq // tm
            y_prompt = _combine_norm(
                outs_p[0].reshape(bp, t_pad, D_MODEL),
                pl.BlockSpec((None, tm, D_MODEL), lambda b, s: (b, s + out0 // tm, 0)), 0, (bp, nblk),
                jax.ShapeDtypeStruct((bp, seq, D_MODEL), F32),
                pl.BlockSpec((None, tm, D_MODEL), lambda b, s: (b, s, 0)), y1, y2, route, g_final, tm)
            assert n_p % bs == 0
            y_sample = _combine_norm(
                outs_s[0], pl.BlockSpec((bs, D_MODEL), lambda i: (0, 0)), n_p // bs, (1,),
                jax.ShapeDtypeStruct((bs, D_MODEL), F32),
                pl.BlockSpec((bs, D_MODEL), lambda i: (0, 0)), y1, y2, route, g_final, bs)

    y_sample = y_sample.reshape(bs, 1, D_MODEL)

    def stack(states, b):
        re = jnp.stack([s[0].reshape(b, S5_GROUPS, S5_STATE) for s in states])
        im = jnp.stack([s[1].reshape(b, S5_GROUPS, S5_STATE) for s in states])
        rw = jnp.stack([s[2] for s in states])
        sh = jnp.stack([s[3] for s in states])
        return re, im, rw, sh

    p_re, p_im, p_rw, p_sh = stack(p_states, bp)
    s_re, s_im, s_rw, s_sh = stack(s_states, bs)
    return (y_prompt, y_sample, p_re, p_im, p_rw, p_sh, s_re, s_im, s_rw, s_sh)
```

```python
import functools

import jax
import jax.numpy as jnp
from jax import lax
from jax.experimental import pallas as pl
from jax.experimental.pallas import tpu as pltpu

F32 = jnp.float32
BF16 = jnp.bfloat16
HIGHEST = lax.Precision.HIGHEST

D_MODEL = 1024
DEPTH = 2
N_META = 16
D_A = 512
S5_GROUP = 16
S5_GROUPS = 32
S5_STATE = 64
S5_TILES = 4
S5_HALF = S5_GROUPS * S5_STATE
D_B = 512
RW_HEAD = 64
RW_HEADS = 8
LORA_W = 64
LORA_A = 64
LORA_G = 128
GN_EPS = 64e-5
P_RW = 3 * D_B + LORA_W + LORA_A + LORA_G
D_FF = 2816
N_EXPERTS = 8
D_EXPERT = 3584
RMS_EPS = 1e-6
LANES = 128
SUBLANES = 8
VMEM_LIMIT_MB = 56
RW_TC = LANES
DB_TILES = D_B // LANES


def _pick_tile(n, pref, mult=SUBLANES):
    best = None
    for d in range(mult, min(n, pref) + 1, mult):
        if n % d == 0:
            best = d
    return best if best is not None else n


def _pick_s5_chunk(t_pad, pref=136):
    cands = [d for d in range(SUBLANES, min(t_pad, pref) + 1, SUBLANES) if t_pad % d == 0]
    odd = [d for d in cands if (d // SUBLANES) % 2 == 1]
    return max(odd) if odd else max(cands)


def _cparams(sem):
    return pltpu.CompilerParams(dimension_semantics=sem,
                                vmem_limit_bytes=VMEM_LIMIT_MB * 1024 * 1024)


def _const_spec(shape):
    nd = len(shape)
    return pl.BlockSpec(shape, lambda *_: (0,) * nd, pipeline_mode=pl.Buffered(1))


def _mm(a, b, hp):
    if hp:
        return jnp.dot(a.astype(F32), b, precision=HIGHEST, preferred_element_type=F32)
    return jnp.dot(a.astype(BF16), b, preferred_element_type=F32)


def _segsum(x, ones, hp):
    hi = x.astype(BF16)
    out = jnp.dot(hi, ones, preferred_element_type=F32)
    if hp:
        r1 = x - hi.astype(F32)
        mid = r1.astype(BF16)
        lo = (r1 - mid.astype(F32)).astype(BF16)
        out = out + jnp.dot(mid, ones, preferred_element_type=F32) + jnp.dot(lo, ones, preferred_element_type=F32)
    return out


def _sigmoid(x):
    return 1.0 / (1.0 + jnp.exp(-x))


def _softplus(x):
    return jnp.maximum(x, 0.0) + jnp.log1p(jnp.exp(-jnp.abs(x)))


def _rms(x, g):
    return x * lax.rsqrt(jnp.mean(x * x, axis=-1, keepdims=True) + RMS_EPS) * g


def _s5_disc_kernel(lr_ref, li_ref, ldt_ref, btr_ref, bti_ref, abr_o, abi_o, bbr_o, bbi_o):
    lr = lr_ref[...]
    li = li_ref[...]
    dt = jnp.exp(ldt_ref[...])
    mag = jnp.exp(lr * dt)
    ab_re = mag * jnp.cos(li * dt)
    ab_im = mag * jnp.sin(li * dt)
    den = lr * lr + li * li
    q_re = ((ab_re - 1.0) * lr + ab_im * li) / den
    q_im = (ab_im * lr - (ab_re - 1.0) * li) / den
    abr_o[...] = ab_re
    abi_o[...] = ab_im
    for c in range(S5_GROUP):
        b_re = btr_ref[c]
        b_im = bti_ref[c]
        bbr_o[c] = q_re * b_re - q_im * b_im
        bbi_o[c] = q_re * b_im + q_im * b_re


def _s5_discretize(lam_re, lam_im, log_dt, b_re, b_im):
    gp = jax.ShapeDtypeStruct((S5_GROUPS, S5_STATE), F32)
    cgp = jax.ShapeDtypeStruct((S5_GROUP, S5_GROUPS, S5_STATE), F32)
    return pl.pallas_call(_s5_disc_kernel, out_shape=(gp, gp, cgp, cgp), name="s5_disc")(
        lam_re, lam_im, log_dt[:, None],
        jnp.transpose(b_re, (2, 0, 1)), jnp.transpose(b_im, (2, 0, 1)))


def _s5_kernel(x_ref, gm_ref, wu_ref, bbar_ref, ar_ref, ai_ref, x0r_ref, x0i_ref,
               cre_ref, cim_ref, d_ref, g1_ref, g2_ref,
               ya_ref, xr_out, xi_out, br_scr, bi_scr, sr_scr, si_scr, *rest,
               batch, steps, t_start, hp, reorder):
    c = pl.program_id(0)

    @pl.when(c == 0)
    def _():
        sr_scr[...] = x0r_ref[...]
        si_scr[...] = x0i_ref[...]

    rows = batch * steps
    x = x_ref[...]
    if reorder:
        bm_scr, tm_scr = rest
        x = x.reshape(rows, D_MODEL)
    h = _rms(x, gm_ref[...])
    u = _mm(h, wu_ref[...], hp)

    def to_time_major(t, carry):
        for s in range(S5_TILES):
            tm_scr[s, pl.ds(pl.multiple_of(t * batch, batch), batch), :] = bm_scr[s, pl.ds(t, batch, stride=steps), :]
        return carry

    def to_batch_major(t, carry):
        for s in range(S5_TILES):
            bm_scr[s, pl.ds(t, batch, stride=steps), :] = tm_scr[s, pl.ds(pl.multiple_of(t * batch, batch), batch), :]
        return carry

    if reorder:
        for s in range(S5_TILES):
            bm_scr[s] = u[:, s * LANES:(s + 1) * LANES]
        lax.fori_loop(0, steps, to_time_major, 0)
        u_tm = jnp.concatenate([tm_scr[s] for s in range(S5_TILES)], axis=1)
    else:
        u_tm = u
    half = S5_HALF // S5_TILES
    for j in range(S5_TILES):
        bbj = _mm(u_tm[:, j * LANES:(j + 1) * LANES], bbar_ref[j], hp)
        br_scr[:, j * half:(j + 1) * half] = bbj[:, :half]
        bi_scr[:, j * half:(j + 1) * half] = bbj[:, half:]

    def step(t, carry):
        xr, xi = carry
        rws = pl.ds(pl.multiple_of(t * batch, batch), batch)
        ar = ar_ref[...]
        ai = ai_ref[...]
        nxr = ar * xr - ai * xi + br_scr[rws, :]
        nxi = ar * xi + ai * xr + bi_scr[rws, :]
        br_scr[rws, :] = nxr
        bi_scr[rws, :] = nxi
        return nxr, nxi

    carry = (sr_scr[...], si_scr[...])
    if steps == 1:
        carry = step(0, carry)
    else:
        lo = jnp.clip(t_start - c * steps, 0, steps)
        carry = lax.fori_loop(lo, steps, step, carry)
    sr_scr[...] = carry[0]
    si_scr[...] = carry[1]
    xr_out[...] = carry[0]
    xi_out[...] = carry[1]

    ys = []
    for j in range(S5_TILES):
        ys.append(_mm(br_scr[:, j * half:(j + 1) * half], cre_ref[j], hp)
                  + _mm(bi_scr[:, j * half:(j + 1) * half], cim_ref[j], hp))
    if reorder:
        for s in range(S5_TILES):
            tm_scr[s] = ys[s]
        lax.fori_loop(0, steps, to_batch_major, 0)
        y = jnp.concatenate([bm_scr[s] for s in range(S5_TILES)], axis=1)
    else:
        y = jnp.concatenate(ys, axis=1)
    y = jax.nn.gelu(y + d_ref[...] * u)
    out = _mm(y, g1_ref[...], hp) * _sigmoid(_mm(y, g2_ref[...], hp))
    ya_ref[...] = out.reshape(ya_ref.shape)


def _s5_branch(x, batch, t_len, t_start, P, x0r, x0i, hp):
    seq = x.ndim == 3
    steps = _pick_s5_chunk(t_len) if seq else 1
    nchunks = t_len // steps
    rows = batch * steps
    if seq:
        x_spec = pl.BlockSpec((batch, steps, D_MODEL), lambda c: (0, c, 0))
        scratch_extra = [pltpu.VMEM((S5_TILES, rows, LANES), F32), pltpu.VMEM((S5_TILES, rows, LANES), F32)]
    else:
        x_spec = pl.BlockSpec((batch, D_MODEL), lambda c: (0, 0))
        scratch_extra = []
    state_spec = pl.BlockSpec((batch, S5_HALF), lambda c: (0, 0))
    consts = [P["g_mix"], P["w_u"], P["s5_bbar"], P["s5_ar"][batch], P["s5_ai"][batch], x0r, x0i,
              P["s5_cre"], P["s5_cim"], P["s5_d"], P["s5_glu1"], P["s5_glu2"]]
    return pl.pallas_call(
        functools.partial(_s5_kernel, batch=batch, steps=steps, t_start=t_start, hp=hp, reorder=seq),
        grid=(nchunks,),
        in_specs=[x_spec] + [_const_spec(a.shape) for a in consts],
        out_specs=(x_spec, state_spec, state_spec),
        out_shape=(jax.ShapeDtypeStruct(x.shape, F32),
                   jax.ShapeDtypeStruct((batch, S5_HALF), F32),
                   jax.ShapeDtypeStruct((batch, S5_HALF), F32)),
        scratch_shapes=[pltpu.VMEM((rows, S5_HALF), F32), pltpu.VMEM((rows, S5_HALF), F32),
                        pltpu.VMEM((batch, S5_HALF), F32), pltpu.VMEM((batch, S5_HALF), F32)] + scratch_extra,
        compiler_params=_cparams(("arbitrary",)),
        name="s5_branch_hp" if hp else "s5_branch",
    )(x, *consts)


N_KVEC = 5
N_VEC = N_KVEC + 1


def _rw_prep_kernel(x_ref, gm_ref, wrw_ref, sh0_ref, mu_ref, wl_ref, w0_ref, a0_ref, gup_ref,
                    kk_ref, ka_ref, rk_ref, ones_ref,
                    vec_o, g_o, bon_o, sh_o, carry_scr, *, seq, t_start, hp):
    h = _rms(x_ref[...], gm_ref[...])
    p = _mm(h, wrw_ref[...], hp)
    rows = p.shape[0]
    if seq:
        c = pl.program_id(1)

        @pl.when(c == 0)
        def _():
            carry_scr[...] = jnp.zeros_like(carry_scr)

        row = lax.broadcasted_iota(jnp.int32, p.shape, 0)
        prev = jnp.where(row == 0, carry_scr[0:1, :], pltpu.roll(p, 1, 0))
        prev = jnp.where(row + c * rows == t_start, sh0_ref[...], prev)
        carry_scr[0:1, :] = p[rows - 1:rows, :]

        @pl.when(c == pl.num_programs(1) - 1)
        def _():
            sh_o[...] = p[rows - 1:rows, :]
    else:
        prev = sh0_ref[...]
        sh_o[...] = p
    z = p + (prev - p) * mu_ref[...]
    r = z[:, :D_B]
    k = z[:, D_B:2 * D_B]
    v = z[:, 2 * D_B:3 * D_B]
    zwa = z[:, 3 * D_B:3 * D_B + LORA_W + LORA_A]
    zg = z[:, 3 * D_B + LORA_W + LORA_A:]
    lane = lax.broadcasted_iota(jnp.int32, zwa.shape, 1)
    tw = jnp.where(lane < LORA_W, jnp.tanh(zwa), zwa)
    lw = _mm(tw, wl_ref[...], hp)
    w_log = -_softplus(-(w0_ref[...] + lw[:, :D_B])) - 0.5
    decay = jnp.exp(-jnp.exp(w_log))
    a = _sigmoid(a0_ref[...] + lw[:, D_B:])
    g = _mm(_sigmoid(zg), gup_ref[...], hp)
    kk = k * kk_ref[...]
    n2 = _segsum(kk * kk, ones_ref[...], hp)
    kkn = kk * lax.rsqrt(jnp.maximum(n2, 1e-24))
    k2 = k * (1.0 + (a - 1.0) * ka_ref[...])
    rk = _segsum(r * k2 * rk_ref[...], ones_ref[...], hp)
    vec_o[0] = kkn
    vec_o[1] = decay
    vec_o[2] = -(kkn * a)
    vec_o[3] = k2
    vec_o[4] = r
    vec_o[5] = v
    g_o[...] = g
    bon_o[...] = rk * v


def _rw_prep(x, P, sh0, t_start, hp):
    seq = x.ndim == 3
    consts = [P["g_mix"], P["w_rw"]]
    consts2 = [P["rw_mu"], P["rw_wl"], P["rw_w0"], P["rw_a0"], P["rw_g_up"],
               P["rw_k_k"], P["rw_k_a"], P["rw_r_k"], P["ones_head"]]
    if seq:
        batch, t_len, _ = x.shape
        tc = _pick_tile(t_len, 544)
        grid = (batch, t_len // tc)
        row_spec = lambda w: pl.BlockSpec((None, tc, w), lambda b, c: (b, c, 0))
        vec_spec = pl.BlockSpec((N_VEC, None, tc, D_B), lambda b, c: (0, b, c, 0))
        sh_spec = pl.BlockSpec((None, 1, P_RW), lambda b, c: (b, 0, 0))
        sh0 = sh0.reshape(batch, 1, P_RW)
        vec_shape = (N_VEC, batch, t_len, D_B)
        g_shape = (batch, t_len, D_B)
        sh_shape = (batch, 1, P_RW)
        sem = ("arbitrary", "arbitrary")
    else:
        batch = x.shape[0]
        grid = (1,)
        row_spec = lambda w: pl.BlockSpec((batch, w), lambda i: (0, 0))
        vec_spec = pl.BlockSpec((N_VEC, batch, D_B), lambda i: (0, 0, 0))
        sh_spec = row_spec(P_RW)
        vec_shape = (N_VEC, batch, D_B)
        g_shape = (batch, D_B)
        sh_shape = (batch, P_RW)
        sem = ("arbitrary",)
    return pl.pallas_call(
        functools.partial(_rw_prep_kernel, seq=seq, t_start=t_start, hp=hp),
        grid=grid,
        in_specs=[row_spec(D_MODEL)] + [_const_spec(a.shape) for a in consts] + [sh_spec]
                 + [_const_spec(a.shape) for a in consts2],
        out_specs=(vec_spec, row_spec(D_B), row_spec(D_B), sh_spec),
        out_shape=(jax.ShapeDtypeStruct(vec_shape, F32), jax.ShapeDtypeStruct(g_shape, F32),
                   jax.ShapeDtypeStruct(g_shape, F32), jax.ShapeDtypeStruct(sh_shape, F32)),
        scratch_shapes=[pltpu.VMEM((SUBLANES, P_RW), F32)],
        compiler_params=_cparams(sem),
        name="rw_prep_hp" if hp else "rw_prep",
    )(x, *consts, sh0, *consts2)


RW_VH = RW_HEAD // (2 * SUBLANES)
RW_PAIRS = SUBLANES * RW_HEADS
RW_SC = RW_TC // 2


RW_QP = RW_HEAD + 4


def _pair_rows(i):
    return pl.ds(i, RW_PAIRS, stride=RW_QP)


def _pairs_to_rows(x_ref, q_scr, batch):
    for b in range(batch):
        for j in range(DB_TILES):
            tile = x_ref[b, :, j * LANES:(j + 1) * LANES].T
            pair = b * RW_HEADS + 2 * j
            q_scr[pl.ds(pair * RW_QP, RW_HEAD), :] = tile[:RW_HEAD]
            q_scr[pl.ds((pair + 1) * RW_QP, RW_HEAD), :] = tile[RW_HEAD:]


def _rw_kin_kernel(x_ref, o_ref, q_scr, *, batch):
    _pairs_to_rows(x_ref, q_scr, batch)
    for k in range(RW_HEAD):
        m = q_scr[_pair_rows(k), :]
        o_ref[k] = jnp.concatenate([m, m], axis=0).T


def _rw_vin_kernel(x_ref, o_ref, q_scr, *, batch):
    _pairs_to_rows(x_ref, q_scr, batch)
    for vh in range(RW_VH):
        for vs in range(SUBLANES):
            v0 = (vh * SUBLANES + vs) * 2
            pair_tile = jnp.concatenate([q_scr[_pair_rows(v0), :], q_scr[_pair_rows(v0 + 1), :]], axis=0)
            o_ref[vh, pl.ds(vs, RW_TC, stride=SUBLANES), :] = pair_tile.T


def _rw_unlayout_kernel(o3_ref, o_ref, q_scr, *, batch):
    for vh in range(RW_VH):
        for vs in range(SUBLANES):
            v0 = (vh * SUBLANES + vs) * 2
            zt = o3_ref[vh, pl.ds(vs, RW_TC, stride=SUBLANES), :].T
            q_scr[_pair_rows(v0), :] = zt[:RW_PAIRS]
            q_scr[_pair_rows(v0 + 1), :] = zt[RW_PAIRS:]
    for b in range(batch):
        for j in range(DB_TILES):
            pair = b * RW_HEADS + 2 * j
            tile = jnp.concatenate([q_scr[pl.ds(pair * RW_QP, RW_HEAD), :],
                                    q_scr[pl.ds((pair + 1) * RW_QP, RW_HEAD), :]], axis=0)
            o_ref[b, :, j * LANES:(j + 1) * LANES] = tile.T


def _rw_scan_kernel(kin_ref, vin_ref, s0_ref, o_ref, sfin_ref, s_scr, *, t_start):
    c = pl.program_id(0)

    @pl.when(c == 0)
    def _():
        s_scr[...] = s0_ref[...]

    def krow(t, vec, k):
        return jnp.broadcast_to(kin_ref[vec * RW_HEAD + k, pl.ds(t, 1), :], (SUBLANES, LANES))

    def acc_add(acc, vh, k, x):
        prev = acc[vh][k % 2]
        acc[vh][k % 2] = x if prev is None else prev + x

    def first_sa(t):
        acc = [[None, None] for _ in range(RW_VH)]
        for k in range(RW_HEAD):
            kkb = krow(t, 0, k)
            for vh in range(RW_VH):
                acc_add(acc, vh, k, s_scr[vh, k] * kkb)
        return tuple(a[0] + a[1] for a in acc)

    def step(t, sa):
        tn = jnp.minimum(t + 1, RW_SC - 1)
        tv = pl.multiple_of(t * SUBLANES, SUBLANES)
        vv = [vin_ref[vh, pl.ds(tv, SUBLANES), :] for vh in range(RW_VH)]
        oacc = [[None, None] for _ in range(RW_VH)]
        nacc = [[None, None] for _ in range(RW_VH)]
        for k in range(RW_HEAD):
            wb = krow(t, 1, k)
            nbb = krow(t, 2, k)
            kb = krow(t, 3, k)
            rb = krow(t, 4, k)
            kkn = krow(tn, 0, k)
            for vh in range(RW_VH):
                s = s_scr[vh, k] * wb + sa[vh] * nbb + vv[vh] * kb
                s_scr[vh, k] = s
                acc_add(oacc, vh, k, s * rb)
                acc_add(nacc, vh, k, s * kkn)
        for vh in range(RW_VH):
            o_ref[vh, pl.ds(tv, SUBLANES), :] = oacc[vh][0] + oacc[vh][1]
        return tuple(a[0] + a[1] for a in nacc)

    lo = jnp.clip(t_start - c * RW_SC, 0, RW_SC)

    @pl.when(lo > 0)
    def _():
        o_ref[...] = jnp.zeros_like(o_ref)

    lax.fori_loop(lo, RW_SC, step, first_sa(jnp.minimum(lo, RW_SC - 1)))
    sfin_ref[...] = s_scr[...]


def _rw_recurrence_long(vecs, s0, t_start):
    _, batch, t_len, _ = vecs.shape
    assert batch == SUBLANES and t_len % RW_TC == 0
    nchunks = t_len // RW_TC
    q_scr = pltpu.VMEM((RW_PAIRS * RW_QP, LANES), F32)
    kin = pl.pallas_call(
        functools.partial(_rw_kin_kernel, batch=batch),
        grid=(nchunks, N_KVEC),
        in_specs=[pl.BlockSpec((None, batch, RW_TC, D_B), lambda c, i: (i, 0, c, 0))],
        out_specs=pl.BlockSpec((None, RW_HEAD, RW_TC, LANES), lambda c, i: (c, i, 0, 0)),
        out_shape=jax.ShapeDtypeStruct((nchunks, N_KVEC * RW_HEAD, RW_TC, LANES), F32),
        scratch_shapes=[q_scr],
        compiler_params=_cparams(("arbitrary", "arbitrary")),
        name="rw_kin",
    )(vecs)
    vin = pl.pallas_call(
        functools.partial(_rw_vin_kernel, batch=batch),
        grid=(nchunks,),
        in_specs=[pl.BlockSpec((None, batch, RW_TC, D_B), lambda c: (N_KVEC, 0, c, 0))],
        out_specs=pl.BlockSpec((None, RW_VH, RW_TC * SUBLANES, LANES), lambda c: (c, 0, 0, 0)),
        out_shape=jax.ShapeDtypeStruct((nchunks, RW_VH, RW_TC * SUBLANES, LANES), F32),
        scratch_shapes=[q_scr],
        compiler_params=_cparams(("arbitrary",)),
        name="rw_vin",
    )(vecs)
    sshape = (RW_VH, RW_HEAD, SUBLANES, LANES)
    s = s0.reshape(batch, RW_HEADS, RW_VH, SUBLANES, 2, RW_HEAD)
    s = jnp.transpose(s, (2, 5, 3, 4, 0, 1)).reshape(sshape)
    state_spec = pl.BlockSpec(sshape, lambda c: (0, 0, 0, 0))
    halves = RW_TC // RW_SC
    vblk = (None, RW_VH, RW_SC * SUBLANES, LANES)
    vmap = lambda c: (c // halves, 0, c % halves, 0)
    o3, sfin = pl.pallas_call(
        functools.partial(_rw_scan_kernel, t_start=t_start),
        grid=(nchunks * halves,),
        in_specs=[pl.BlockSpec((None, N_KVEC * RW_HEAD, RW_SC, LANES), vmap),
                  pl.BlockSpec(vblk, vmap), state_spec],
        out_specs=(pl.BlockSpec(vblk, vmap), state_spec),
        out_shape=(jax.ShapeDtypeStruct((nchunks, RW_VH, RW_TC * SUBLANES, LANES), F32),
                   jax.ShapeDtypeStruct(sshape, F32)),
        scratch_shapes=[pltpu.VMEM(sshape, F32)],
        compiler_params=_cparams(("arbitrary",)),
        name="rw_scan",
    )(kin, vin, s)
    o = pl.pallas_call(
        functools.partial(_rw_unlayout_kernel, batch=batch),
        grid=(nchunks,),
        in_specs=[pl.BlockSpec((None, RW_VH, RW_TC * SUBLANES, LANES), lambda c: (c, 0, 0, 0))],
        out_specs=pl.BlockSpec((batch, RW_TC, D_B), lambda c: (0, c, 0)),
        out_shape=jax.ShapeDtypeStruct((batch, t_len, D_B), F32),
        scratch_shapes=[q_scr],
        compiler_params=_cparams(("arbitrary",)),
        name="rw_unlayout",
    )(o3)
    sfin = sfin.reshape(RW_VH, RW_HEAD, SUBLANES, 2, batch, RW_HEADS)
    sfin = jnp.transpose(sfin, (4, 5, 0, 2, 3, 1)).reshape(batch, RW_HEADS, RW_HEAD, RW_HEAD)
    return o, sfin


STEP_PAIRS = 64
STEP_VROWS = RW_HEAD // 2


def _split3(x):
    hi = x.astype(BF16)
    r1 = x - hi.astype(F32)
    mid = r1.astype(BF16)
    return hi, mid, (r1 - mid.astype(F32)).astype(BF16)


def _rw_step_kernel(s_ref, km_ref, v_ref, sel_ref, vmask_ref, vexp_ref, ones_ref, s_o, o_o):
    sel = sel_ref[...]

    def rows_of(x):
        return sum(jnp.dot(sel, p, preferred_element_type=F32) for p in _split3(x))

    kx = rows_of(km_ref[...])
    kk, w, nb, k, r = [kx[:, i * LANES:(i + 1) * LANES] for i in range(N_KVEC)]
    vx = _segsum(rows_of(v_ref[...]) * vmask_ref[...], vexp_ref[...], True)
    s = s_ref[...]
    ones = ones_ref[...]
    sa = _segsum(s * kk, ones, True)
    s = s * w + sa * nb + vx * k
    s_o[...] = s
    o_o[...] = _segsum(s * r, ones, True)


def _rw_recurrence_step(vecs, s0, ones_half):
    batch = vecs.shape[1]
    pairs = batch * RW_HEADS
    rows = pairs * STEP_VROWS
    tr = STEP_PAIRS * STEP_VROWS
    assert pairs % STEP_PAIRS == 0
    kp = vecs[:N_KVEC].reshape(N_KVEC, pairs, 1, RW_HEAD)
    km = jnp.transpose(jnp.broadcast_to(kp, (N_KVEC, pairs, 2, RW_HEAD)), (1, 0, 2, 3)).reshape(pairs, N_KVEC * LANES)
    vp = vecs[N_KVEC].reshape(pairs, RW_HEAD)
    ridx = jnp.arange(tr, dtype=jnp.int32)
    sel = (ridx[:, None] // STEP_VROWS == jnp.arange(STEP_PAIRS, dtype=jnp.int32)[None, :]).astype(BF16)
    vidx = jnp.arange(RW_HEAD, dtype=jnp.int32)
    vmask = (vidx[None, :] // 2 == ridx[:, None] % STEP_VROWS).astype(F32)
    vexp = (vidx[:, None] % 2 == jnp.arange(LANES, dtype=jnp.int32)[None, :] // RW_HEAD).astype(BF16)
    spec = pl.BlockSpec((tr, LANES), lambda i: (i, 0))
    big = jax.ShapeDtypeStruct((rows, LANES), F32)
    consts = [sel, vmask, vexp, ones_half]
    s_new, o = pl.pallas_call(
        _rw_step_kernel,
        grid=(pairs // STEP_PAIRS,),
        in_specs=[spec, pl.BlockSpec((STEP_PAIRS, N_KVEC * LANES), lambda i: (i, 0)),
                  pl.BlockSpec((STEP_PAIRS, RW_HEAD), lambda i: (i, 0))]
                 + [_const_spec(a.shape) for a in consts],
        out_specs=(spec, spec),
        out_shape=(big, big),
        compiler_params=_cparams(("arbitrary",)),
        name="rw_step",
    )(s0.reshape(rows, LANES), km, vp, *consts)
    o = o.reshape(rows, 2, RW_HEAD)[:, :, 0].reshape(batch, D_B)
    return o, s_new.reshape(batch, RW_HEADS, RW_HEAD, RW_HEAD)


def _mix_kernel(*refs, hp, moe):
    (x_ref, ya_ref, o_ref, bon_ref, g_ref, gm_ref, wg_ref, gng_ref, gnb_ref, ones_ref,
     wbo_ref, wout_ref, gf_ref) = refs[:13]
    if moe:
        wr_ref, x1_o, h2_o, route_o = refs[13:]
    else:
        x1_o, h2_o = refs[13:]
    x = x_ref[...]
    h = _rms(x, gm_ref[...])
    gates = _mm(h, wg_ref[...], hp)
    o = o_ref[...]
    ones = ones_ref[...]
    inv_n = 1.0 / RW_HEAD
    mean = _segsum(o, ones, hp) * inv_n
    dlt = o - mean
    var = _segsum(dlt * dlt, ones, hp) * inv_n
    on = dlt * lax.rsqrt(var + GN_EPS) * gng_ref[...] + gnb_ref[...]
    yb = _mm((on + bon_ref[...]) * g_ref[...], wbo_ref[...], hp)
    m = _sigmoid(gates[:, :D_MODEL]) * ya_ref[...] + _sigmoid(gates[:, D_MODEL:]) * yb
    x1 = x + _mm(m, wout_ref[...], hp)
    h2 = _rms(x1, gf_ref[...])
    x1_o[...] = x1
    h2_o[...] = h2.astype(h2_o.dtype)
    if moe:
        logits = _mm(h2, wr_ref[...], hp)
        lane = lax.broadcasted_iota(jnp.int32, logits.shape, 1).astype(F32)
        neg = jnp.float32(-jnp.inf)
        lg = jnp.where(lane < N_EXPERTS, logits, neg)
        m1 = jnp.max(lg, axis=1, keepdims=True)
        i1 = jnp.min(jnp.where(lg == m1, lane, float(LANES)), axis=1, keepdims=True)
        lg2 = jnp.where(lane == i1, neg, lg)
        m2 = jnp.max(lg2, axis=1, keepdims=True)
        i2 = jnp.min(jnp.where(lg2 == m2, lane, float(LANES)), axis=1, keepdims=True)
        e = jnp.exp(m2 - m1)
        g1 = 1.0 / (1.0 + e)
        g2 = e / (1.0 + e)
        route_o[...] = jnp.where(lane == 0.0, i1, jnp.where(lane == 1.0, i2,
                                 jnp.where(lane == 2.0, g1, jnp.where(lane == 3.0, g2, 0.0))))


def _mix(x, ya, o, bon, g, P, hp, moe):
    rows = x.shape[0]
    tm = _pick_tile(rows, 344)
    row_spec = lambda w: pl.BlockSpec((tm, w), lambda i: (i, 0))
    consts = [P["g_mix"], P["w_gate"], P["rw_gn_g"], P["rw_gn_b"], P["ones_head"],
              P["rw_w_bo"], P["w_out"], P["g_ffn"]]
    if moe:
        consts.append(P["w_router"])
    h2_dtype = F32 if (hp or moe) else BF16
    out_specs = [row_spec(D_MODEL), row_spec(D_MODEL)]
    out_shape = [jax.ShapeDtypeStruct((rows, D_MODEL), F32), jax.ShapeDtypeStruct((rows, D_MODEL), h2_dtype)]
    if moe:
        out_specs.append(row_spec(LANES))
        out_shape.append(jax.ShapeDtypeStruct((rows, LANES), F32))
    return pl.pallas_call(
        functools.partial(_mix_kernel, hp=hp, moe=moe),
        grid=(rows // tm,),
        in_specs=[row_spec(D_MODEL), row_spec(D_MODEL), row_spec(D_B), row_spec(D_B), row_spec(D_B)]
                 + [_const_spec(a.shape) for a in consts],
        out_specs=tuple(out_specs),
        out_shape=tuple(out_shape),
        compiler_params=_cparams(("arbitrary",)),
        name=("mix_moe" if moe else "mix") + ("_hp" if hp else ""),
    )(x, ya, o, bon, g, *consts)


def _ffn_kernel(h_ref, x1_ref, w1_ref, w3_ref, w2_ref, o_ref, *, hp):
    h = h_ref[...]
    a = _mm(h, w1_ref[...], hp)
    b = _mm(h, w3_ref[...], hp)
    o_ref[...] = x1_ref[...] + _mm(a * _sigmoid(a) * b, w2_ref[...], hp)


def _ffn(h2, x1, w1, w3, w2, tm, hp):
    rows = h2.shape[0]
    assert rows % tm == 0
    row_spec = pl.BlockSpec((tm, D_MODEL), lambda i: (i, 0))
    return pl.pallas_call(
        functools.partial(_ffn_kernel, hp=hp),
        grid=(rows // tm,),
        in_specs=[row_spec, row_spec, _const_spec(w1.shape), _const_spec(w3.shape), _const_spec(w2.shape)],
        out_specs=row_spec,
        out_shape=jax.ShapeDtypeStruct((rows, D_MODEL), F32),
        compiler_params=_cparams(("arbitrary",)),
        name="ffn_hp" if hp else "ffn",
    )(h2, x1, w1, w3, w2)


MOE_TM = 512
MOE_TF = 1792


def _moe_kernel(te_ref, nv_ref, x_ref, w1_ref, w3_ref, w2_ref, o_ref, acc):
    i = pl.program_id(0)
    f = pl.program_id(1)

    @pl.when(f == 0)
    def _():
        acc[...] = jnp.zeros_like(acc)

    @pl.when(i < nv_ref[0])
    def _():
        x = x_ref[...].astype(BF16)
        a = jnp.dot(x, w1_ref[...], preferred_element_type=F32)
        b = jnp.dot(x, w3_ref[...], preferred_element_type=F32)
        acc[...] += jnp.dot((a * _sigmoid(a) * b).astype(BF16), w2_ref[...], preferred_element_type=F32)

    @pl.when(f == pl.num_programs(1) - 1)
    def _():
        o_ref[...] = acc[...]


def _moe_experts(xs, tile_expert, n_valid, w1, w3, w2):
    rows = xs.shape[0]
    ntiles = rows // MOE_TM
    nf = D_EXPERT // MOE_TF
    assert ntiles * MOE_TM == rows and nf * MOE_TF == D_EXPERT

    def fblk(i, f, nv):
        walk = lambda t, g: jnp.where(t % 2 == 0, g, nf - 1 - g)
        return jnp.where(i < nv[0], walk(i, f), walk(nv[0] - 1, nf - 1))

    grid_spec = pltpu.PrefetchScalarGridSpec(
        num_scalar_prefetch=2,
        grid=(ntiles, nf),
        in_specs=[pl.BlockSpec((MOE_TM, D_MODEL), lambda i, f, te, nv: (i, 0)),
                  pl.BlockSpec((None, D_MODEL, MOE_TF), lambda i, f, te, nv: (te[i], 0, fblk(i, f, nv))),
                  pl.BlockSpec((None, D_MODEL, MOE_TF), lambda i, f, te, nv: (te[i], 0, fblk(i, f, nv))),
                  pl.BlockSpec((None, MOE_TF, D_MODEL), lambda i, f, te, nv: (te[i], fblk(i, f, nv), 0))],
        out_specs=pl.BlockSpec((MOE_TM, D_MODEL), lambda i, f, te, nv: (i, 0)),
        scratch_shapes=[pltpu.VMEM((MOE_TM, D_MODEL), F32)])
    return pl.pallas_call(
        _moe_kernel,
        grid_spec=grid_spec,
        out_shape=jax.ShapeDtypeStruct((rows, D_MODEL), F32),
        compiler_params=_cparams(("arbitrary", "arbitrary")),
        name="moe_experts",
    )(tile_expert, n_valid, xs, w1, w3, w2)


def _moe_plan(expert_idx, src_row):
    n_pairs = expert_idx.shape[0] * 2
    flat_e = expert_idx.reshape(n_pairs)
    onehot = (flat_e[:, None] == jnp.arange(N_EXPERTS, dtype=jnp.int32)[None, :]).astype(jnp.int32)
    csum = jnp.cumsum(onehot, axis=0)
    rank = jnp.sum(csum * onehot, axis=1) - 1
    counts = csum[-1]
    padded = ((counts + MOE_TM - 1) // MOE_TM) * MOE_TM
    pend = jnp.cumsum(padded)
    pstart = pend - padded
    dest = pstart[flat_e] + rank
    ntiles = (n_pairs + N_EXPERTS * (MOE_TM - 1)) // MOE_TM + 1
    rows = ntiles * MOE_TM
    tile_start = jnp.arange(ntiles, dtype=jnp.int32) * MOE_TM
    n_valid = (pend[-1] // MOE_TM).astype(jnp.int32)
    te = jnp.sum((pend[None, :] <= tile_start[:, None]).astype(jnp.int32), axis=1)
    te = jnp.minimum(te, N_EXPERTS - 1)
    order = jnp.argsort(flat_e, stable=True).astype(jnp.int32)
    gstart = jnp.cumsum(counts) - counts
    slot = jnp.arange(rows, dtype=jnp.int32)
    e_slot = jnp.repeat(te, MOE_TM)
    off = slot - pstart[e_slot]
    filled = (off < counts[e_slot]) & (slot < pend[-1])
    pair = order[jnp.clip(gstart[e_slot] + off, 0, n_pairs - 1)]
    src = jnp.where(filled, src_row[pair // 2], 0).astype(jnp.int32)
    last_e = te[jnp.maximum(n_valid - 1, 0)]
    te = jnp.where(jnp.arange(ntiles) < n_valid, te, last_e).astype(jnp.int32)
    return src, dest.reshape(-1, 2), te, n_valid.reshape(1)


def _combine_kernel(x1_ref, y1_ref, y2_ref, route_ref, gf_ref, o_ref):
    route = route_ref[...]
    x2 = x1_ref[...] + route[:, 2:3] * y1_ref[...] + route[:, 3:4] * y2_ref[...]
    o_ref[...] = _rms(x2, gf_ref[...])


def _combine_norm(x1, x1_spec, flat_blk0, grid, out_shape, out_spec, y1, y2, route, g_final, tm):
    nd = len(grid)

    def flat_map(*idx):
        lin = idx[0]
        for k in range(1, nd):
            lin = lin * grid[k] + idx[k]
        return (flat_blk0 + lin, 0)

    return pl.pallas_call(
        _combine_kernel,
        grid=grid,
        in_specs=[x1_spec, pl.BlockSpec((tm, D_MODEL), flat_map), pl.BlockSpec((tm, D_MODEL), flat_map),
                  pl.BlockSpec((tm, LANES), flat_map), _const_spec(g_final.shape)],
        out_specs=out_spec,
        out_shape=out_shape,
        compiler_params=_cparams(("arbitrary",) * nd),
        name="combine_norm",
    )(x1, y1, y2, route, g_final)


MATMUL_WEIGHTS = ("w_u", "w_rw", "w_gate", "s5_glu1", "s5_glu2", "rw_g_up", "rw_w_bo", "w_out", "rw_wl",
                  "s5_bbar", "s5_cre", "s5_cim", "w_router")


def _layer_params(l, W, hp, batches):
    wdt = F32 if hp else BF16
    row = lambda a: a.reshape(1, -1).astype(F32)
    w_in = W["w_in"][l]
    P = {
        "g_mix": row(W["norm_mix"][l]),
        "g_ffn": row(W["norm_ffn"][l]),
        "w_u": w_in[:, :D_A].astype(wdt),
        "w_rw": w_in[:, D_A:D_A + P_RW].astype(wdt),
        "w_gate": w_in[:, D_A + P_RW:].astype(wdt),
        "s5_d": row(W["s5_d"][l]),
        "s5_glu1": W["s5_glu1"][l].astype(wdt),
        "s5_glu2": W["s5_glu2"][l].astype(wdt),
        "rw_mu": row(W["rw_mu"][l]),
        "rw_w0": row(W["rw_w0"][l]),
        "rw_a0": row(W["rw_a0"][l]),
        "rw_g_up": W["rw_g_up"][l].astype(wdt),
        "rw_k_k": row(W["rw_k_k"][l]),
        "rw_k_a": row(W["rw_k_a"][l]),
        "rw_r_k": row(W["rw_r_k"][l]),
        "rw_gn_g": row(W["rw_gn_g"][l]),
        "rw_gn_b": row(W["rw_gn_b"][l]),
        "rw_w_bo": W["rw_w_bo"][l].astype(wdt),
        "w_out": W["w_out"][l].astype(wdt),
    }
    wl = jnp.zeros((LORA_W + LORA_A, 2 * D_B), F32)
    wl = wl.at[:LORA_W, :D_B].set(W["rw_w_up"][l]).at[LORA_W:, D_B:].set(W["rw_a_up"][l])
    P["rw_wl"] = wl.astype(wdt)
    P["ones_head"] = jnp.kron(jnp.eye(RW_HEADS, dtype=F32), jnp.ones((RW_HEAD, RW_HEAD), F32)).astype(BF16)

    ab_re, ab_im, bb_re, bb_im = _s5_discretize(W["s5_lam_re"][l], W["s5_lam_im"][l], W["s5_log_dt"][l],
                                                W["s5_b_re"][l], W["s5_b_im"][l])
    P["s5_ar"] = {b: jnp.broadcast_to(ab_re.reshape(1, S5_HALF), (b, S5_HALF)) for b in batches}
    P["s5_ai"] = {b: jnp.broadcast_to(ab_im.reshape(1, S5_HALF), (b, S5_HALF)) for b in batches}
    gpt = S5_GROUPS // S5_TILES
    eye = jnp.eye(gpt, dtype=F32)
    bb = jnp.stack([bb_re, bb_im]).reshape(2, S5_GROUP, S5_TILES, gpt, S5_STATE)
    bb = jnp.transpose(bb, (2, 1, 0, 3, 4))
    bbar = bb[:, None] * eye[None, :, None, None, :, None]
    P["s5_bbar"] = bbar.reshape(S5_TILES, LANES, 2 * gpt * S5_STATE).astype(wdt)
    cs = jnp.stack([W["s5_c_re"][l], -W["s5_c_im"][l]]).reshape(2, S5_TILES, gpt, S5_GROUP, S5_STATE)
    cs = jnp.transpose(cs, (1, 0, 2, 4, 3))
    cm = cs[:, :, :, :, None, :] * eye[None, None, :, None, :, None]
    cm = cm.reshape(S5_TILES, 2, gpt * S5_STATE, LANES).astype(wdt)
    P["s5_cre"] = cm[:, 0]
    P["s5_cim"] = cm[:, 1]
    return P


def _mixer_seq(x, t_start, s5r0, s5i0, rw_s0, rw_sh0, P, moe):
    batch, t_len, _ = x.shape
    ya, xr, xi = _s5_branch(x, batch, t_len, t_start, P, s5r0, s5i0, False)
    vecs, g, bon, sh = _rw_prep(x, P, rw_sh0, t_start, False)
    o, s_fin = _rw_recurrence_long(vecs, rw_s0, t_start)
    flat = lambda a: a.reshape(batch * t_len, a.shape[-1])
    outs = _mix(flat(x), flat(ya), flat(o), flat(bon), flat(g), P, False, moe)
    return outs, (xr, xi, s_fin, sh.reshape(batch, P_RW))


def _mixer_step(x, s5r0, s5i0, rw_s0, rw_sh0, P, moe, ones_half):
    batch = x.shape[0]
    ya, xr, xi = _s5_branch(x, batch, 1, 0, P, s5r0, s5i0, True)
    vecs, g, bon, sh = _rw_prep(x, P, rw_sh0, 0, True)
    o, s_fin = _rw_recurrence_step(vecs, rw_s0, ones_half)
    outs = _mix(x, ya, o, bon, g, P, True, moe)
    return outs, (xr, xi, s_fin, sh)


def kernel(x_prompt, x_sample, state_s5_re, state_s5_im, state_rwkv, state_shift, meta_tokens, norm_mix, w_in, s5_lam_re, s5_lam_im, s5_log_dt, s5_b_re, s5_b_im, s5_c_re, s5_c_im, s5_d, s5_glu1, s5_glu2, rw_mu, rw_w0, rw_w_up, rw_a0, rw_a_up, rw_g_up, rw_k_k, rw_k_a, rw_r_k, rw_gn_g, rw_gn_b, rw_w_bo, w_out, norm_ffn, ffn_w1, ffn_w3, ffn_w2, moe_router, moe_w1, moe_w3, moe_w2, norm_final):
    W = dict(norm_mix=norm_mix, w_in=w_in, s5_lam_re=s5_lam_re, s5_lam_im=s5_lam_im,
             s5_log_dt=s5_log_dt, s5_b_re=s5_b_re, s5_b_im=s5_b_im, s5_c_re=s5_c_re,
             s5_c_im=s5_c_im, s5_d=s5_d, s5_glu1=s5_glu1, s5_glu2=s5_glu2, rw_mu=rw_mu,
             rw_w0=rw_w0, rw_w_up=rw_w_up, rw_a0=rw_a0, rw_a_up=rw_a_up, rw_g_up=rw_g_up,
             rw_k_k=rw_k_k, rw_k_a=rw_k_a, rw_r_k=rw_r_k, rw_gn_g=rw_gn_g, rw_gn_b=rw_gn_b,
             rw_w_bo=rw_w_bo, w_out=w_out, norm_ffn=norm_ffn)
    bp, seq, _ = x_prompt.shape
    bs = x_sample.shape[0]
    assert x_sample.shape[1] == 1 and seq % RW_TC == 0 and bp == SUBLANES
    tp = N_META + seq
    t_pad = -(-tp // RW_TC) * RW_TC
    t_start = t_pad - tp
    out0 = t_pad - seq

    meta = jnp.broadcast_to(meta_tokens.astype(F32)[None], (bp, N_META, D_MODEL))
    xp = jnp.concatenate([jnp.zeros((bp, t_start, D_MODEL), F32), meta, x_prompt], axis=1)
    xs = x_sample.reshape(bs, D_MODEL)
    ones_half = jnp.kron(jnp.eye(2, dtype=F32), jnp.ones((RW_HEAD, RW_HEAD), F32)).astype(BF16)
    g_final = norm_final.reshape(1, D_MODEL).astype(F32)

    zero_s5 = jnp.zeros((bp, S5_HALF), F32)
    zero_rw = jnp.zeros((bp, RW_HEADS, RW_HEAD, RW_HEAD), F32)
    zero_sh = jnp.zeros((bp, P_RW), F32)

    p_states, s_states = [], []
    for l in range(DEPTH):
        moe = (l % 2 == 1)
        j = l // 2
        Ph = _layer_params(l, W, True, (bp, bs))
        if moe:
            Ph["w_router"] = jnp.zeros((D_MODEL, LANES), F32).at[:, :N_EXPERTS].set(moe_router[j])
        Pl = {k: (v.astype(BF16) if k in MATMUL_WEIGHTS else v) for k, v in Ph.items()}
        outs_p, st_p = _mixer_seq(xp, t_start, zero_s5, zero_s5, zero_rw, zero_sh, Pl, moe)
        outs_s, st_s = _mixer_step(xs, state_s5_re[l].reshape(bs, S5_HALF), state_s5_im[l].reshape(bs, S5_HALF),
                                   state_rwkv[l], state_shift[l], Ph, moe, ones_half)
        p_states.append(st_p)
        s_states.append(st_s)
        if not moe:
            rows = bp * t_pad
            xp = _ffn(outs_p[1], outs_p[0], ffn_w1[j].astype(BF16), ffn_w3[j].astype(BF16),
                      ffn_w2[j].astype(BF16), _pick_tile(rows, 544), False).reshape(bp, t_pad, D_MODEL)
            xs = _ffn(outs_s[1], outs_s[0], ffn_w1[j], ffn_w3[j], ffn_w2[j], bs, True)
        else:
            assert l == DEPTH - 1
            n_p = bp * seq
            route_p = outs_p[2].reshape(bp, t_pad, LANES)[:, out0:].reshape(n_p, LANES)
            route = jnp.concatenate([route_p, outs_s[2]], axis=0)
            tok = jnp.arange(n_p, dtype=jnp.int32)
            if bp * t_start >= bs:
                free = [b * t_pad + t for b in range(bp) for t in range(t_start)][:bs]
                sample_rows = jnp.asarray(free, dtype=jnp.int32)
                h2 = outs_p[1].at[sample_rows].set(outs_s[1])
            else:
                sample_rows = bp * t_pad + jnp.arange(bs, dtype=jnp.int32)
                h2 = jnp.concatenate([outs_p[1], outs_s[1]], axis=0)
            src_row = jnp.concatenate([(tok // seq) * t_pad + out0 + tok % seq, sample_rows])
            src, dest, te, n_valid = _moe_plan(route[:, :2].astype(jnp.int32), src_row)
            take = functools.partial(jnp.take, axis=0, mode="clip")
            y_sorted = _moe_experts(take(h2, src), te, n_valid,
                                    moe_w1[j].astype(BF16), moe_w3[j].astype(BF16), moe_w2[j].astype(BF16))
            y1 = take(y_sorted, dest[:, 0])
            y2 = take(y_sorted, dest[:, 1])
            tm = RW_TC
            nblk = seq // tm
            y_prompt = _combine_norm(
                outs_p[0].reshape(bp, t_pad, D_MODEL),
                pl.BlockSpec((None, tm, D_MODEL), lambda b, s: (b, s + out0 // tm, 0)), 0, (bp, nblk),
                jax.ShapeDtypeStruct((bp, seq, D_MODEL), F32),
                pl.BlockSpec((None, tm, D_MODEL), lambda b, s: (b, s, 0)), y1, y2, route, g_final, tm)
            assert n_p % bs == 0
            y_sample = _combine_norm(
                outs_s[0], pl.BlockSpec((bs, D_MODEL), lambda i: (0, 0)), n_p // bs, (1,),
                jax.ShapeDtypeStruct((bs, D_MODEL), F32),
                pl.BlockSpec((bs, D_MODEL), lambda i: (0, 0)), y1, y2, route, g_final, bs)

    y_sample = y_sample.reshape(bs, 1, D_MODEL)

    def stack(states, b):
        re = jnp.stack([s[0].reshape(b, S5_GROUPS, S5_STATE) for s in states])
        im = jnp.stack([s[1].reshape(b, S5_GROUPS, S5_STATE) for s in states])
        rw = jnp.stack([s[2] for s in states])
        sh = jnp.stack([s[3] for s in states])
        return re, im, rw, sh

    p_re, p_im, p_rw, p_sh = stack(p_states, bp)
    s_re, s_im, s_rw, s_sh = stack(s_states, bs)
    return (y_prompt, y_sample, p_re, p_im, p_rw, p_sh, s_re, s_im, s_rw, s_sh)
```

```python
import functools

import jax
import jax.numpy as jnp
from jax import lax
from jax.experimental import pallas as pl
from jax.experimental.pallas import tpu as pltpu

F32 = jnp.float32
BF16 = jnp.bfloat16
HIGHEST = lax.Precision.HIGHEST

D_MODEL = 1024
DEPTH = 2
N_META = 16
D_A = 512
S5_GROUP = 16
S5_GROUPS = 32
S5_STATE = 64
S5_TILES = 4
S5_HALF = S5_GROUPS * S5_STATE
D_B = 512
RW_HEAD = 64
RW_HEADS = 8
LORA_W = 64
LORA_A = 64
LORA_G = 128
GN_EPS = 64e-5
P_RW = 3 * D_B + LORA_W + LORA_A + LORA_G
D_FF = 2816
N_EXPERTS = 8
D_EXPERT = 3584
RMS_EPS = 1e-6
LANES = 128
SUBLANES = 8
VMEM_LIMIT_MB = 56
RW_TC = LANES
DB_TILES = D_B // LANES


def _pick_tile(n, pref, mult=SUBLANES):
    best = None
    for d in range(mult, min(n, pref) + 1, mult):
        if n % d == 0:
            best = d
    return best if best is not None else n


def _pick_s5_chunk(t_pad, pref=136):
    cands = [d for d in range(SUBLANES, min(t_pad, pref) + 1, SUBLANES) if t_pad % d == 0]
    odd = [d for d in cands if (d // SUBLANES) % 2 == 1]
    return max(odd) if odd else max(cands)


def _cparams(sem):
    return pltpu.CompilerParams(dimension_semantics=sem,
                                vmem_limit_bytes=VMEM_LIMIT_MB * 1024 * 1024)


def _const_spec(shape):
    nd = len(shape)
    return pl.BlockSpec(shape, lambda *_: (0,) * nd, pipeline_mode=pl.Buffered(1))


def _mm(a, b, hp):
    if hp:
        return jnp.dot(a.astype(F32), b, precision=HIGHEST, preferred_element_type=F32)
    return jnp.dot(a.astype(BF16), b, preferred_element_type=F32)


def _segsum(x, ones, hp):
    hi = x.astype(BF16)
    out = jnp.dot(hi, ones, preferred_element_type=F32)
    if hp:
        r1 = x - hi.astype(F32)
        mid = r1.astype(BF16)
        lo = (r1 - mid.astype(F32)).astype(BF16)
        out = out + jnp.dot(mid, ones, preferred_element_type=F32) + jnp.dot(lo, ones, preferred_element_type=F32)
    return out


def _sigmoid(x):
    return 1.0 / (1.0 + jnp.exp(-x))


def _softplus(x):
    return jnp.maximum(x, 0.0) + jnp.log1p(jnp.exp(-jnp.abs(x)))


def _rms(x, g):
    return x * lax.rsqrt(jnp.mean(x * x, axis=-1, keepdims=True) + RMS_EPS) * g


def _s5_disc_kernel(lr_ref, li_ref, ldt_ref, btr_ref, bti_ref, abr_o, abi_o, bbr_o, bbi_o):
    lr = lr_ref[...]
    li = li_ref[...]
    dt = jnp.exp(ldt_ref[...])
    mag = jnp.exp(lr * dt)
    ab_re = mag * jnp.cos(li * dt)
    ab_im = mag * jnp.sin(li * dt)
    den = lr * lr + li * li
    q_re = ((ab_re - 1.0) * lr + ab_im * li) / den
    q_im = (ab_im * lr - (ab_re - 1.0) * li) / den
    abr_o[...] = ab_re
    abi_o[...] = ab_im
    for c in range(S5_GROUP):
        b_re = btr_ref[c]
        b_im = bti_ref[c]
        bbr_o[c] = q_re * b_re - q_im * b_im
        bbi_o[c] = q_re * b_im + q_im * b_re


def _s5_discretize(lam_re, lam_im, log_dt, b_re, b_im):
    gp = jax.ShapeDtypeStruct((S5_GROUPS, S5_STATE), F32)
    cgp = jax.ShapeDtypeStruct((S5_GROUP, S5_GROUPS, S5_STATE), F32)
    return pl.pallas_call(_s5_disc_kernel, out_shape=(gp, gp, cgp, cgp), name="s5_disc")(
        lam_re, lam_im, log_dt[:, None],
        jnp.transpose(b_re, (2, 0, 1)), jnp.transpose(b_im, (2, 0, 1)))


def _s5_kernel(x_ref, gm_ref, wu_ref, bbar_ref, ar_ref, ai_ref, x0r_ref, x0i_ref,
               cre_ref, cim_ref, d_ref, g1_ref, g2_ref,
               ya_ref, xr_out, xi_out, br_scr, bi_scr, sr_scr, si_scr, *rest,
               batch, steps, t_start, hp, reorder):
    c = pl.program_id(0)

    @pl.when(c == 0)
    def _():
        sr_scr[...] = x0r_ref[...]
        si_scr[...] = x0i_ref[...]

    rows = batch * steps
    x = x_ref[...]
    if reorder:
        bm_scr, tm_scr = rest
        x = x.reshape(rows, D_MODEL)
    h = _rms(x, gm_ref[...])
    u = _mm(h, wu_ref[...], hp)

    def to_time_major(t, carry):
        for s in range(S5_TILES):
            tm_scr[s, pl.ds(pl.multiple_of(t * batch, batch), batch), :] = bm_scr[s, pl.ds(t, batch, stride=steps), :]
        return carry

    def to_batch_major(t, carry):
        for s in range(S5_TILES):
            bm_scr[s, pl.ds(t, batch, stride=steps), :] = tm_scr[s, pl.ds(pl.multiple_of(t * batch, batch), batch), :]
        return carry

    if reorder:
        for s in range(S5_TILES):
            bm_scr[s] = u[:, s * LANES:(s + 1) * LANES]
        lax.fori_loop(0, steps, to_time_major, 0)
        u_tm = jnp.concatenate([tm_scr[s] for s in range(S5_TILES)], axis=1)
    else:
        u_tm = u
    half = S5_HALF // S5_TILES
    for j in range(S5_TILES):
        bbj = _mm(u_tm[:, j * LANES:(j + 1) * LANES], bbar_ref[j], hp)
        br_scr[:, j * half:(j + 1) * half] = bbj[:, :half]
        bi_scr[:, j * half:(j + 1) * half] = bbj[:, half:]

    def step(t, carry):
        xr, xi = carry
        rws = pl.ds(pl.multiple_of(t * batch, batch), batch)
        ar = ar_ref[...]
        ai = ai_ref[...]
        nxr = ar * xr - ai * xi + br_scr[rws, :]
        nxi = ar * xi + ai * xr + bi_scr[rws, :]
        br_scr[rws, :] = nxr
        bi_scr[rws, :] = nxi
        return nxr, nxi

    carry = (sr_scr[...], si_scr[...])
    if steps == 1:
        carry = step(0, carry)
    else:
        lo = jnp.clip(t_start - c * steps, 0, steps)
        carry = lax.fori_loop(lo, steps, step, carry)
    sr_scr[...] = carry[0]
    si_scr[...] = carry[1]
    xr_out[...] = carry[0]
    xi_out[...] = carry[1]

    ys = []
    for j in range(S5_TILES):
        ys.append(_mm(br_scr[:, j * half:(j + 1) * half], cre_ref[j], hp)
                  + _mm(bi_scr[:, j * half:(j + 1) * half], cim_ref[j], hp))
    if reorder:
        for s in range(S5_TILES):
            tm_scr[s] = ys[s]
        lax.fori_loop(0, steps, to_batch_major, 0)
        y = jnp.concatenate([bm_scr[s] for s in range(S5_TILES)], axis=1)
    else:
        y = jnp.concatenate(ys, axis=1)
    y = jax.nn.gelu(y + d_ref[...] * u)
    out = _mm(y, g1_ref[...], hp) * _sigmoid(_mm(y, g2_ref[...], hp))
    ya_ref[...] = out.reshape(ya_ref.shape)


def _s5_branch(x, batch, t_len, t_start, P, x0r, x0i, hp):
    seq = x.ndim == 3
    steps = _pick_s5_chunk(t_len) if seq else 1
    nchunks = t_len // steps
    rows = batch * steps
    if seq:
        x_spec = pl.BlockSpec((batch, steps, D_MODEL), lambda c: (0, c, 0))
        scratch_extra = [pltpu.VMEM((S5_TILES, rows, LANES), F32), pltpu.VMEM((S5_TILES, rows, LANES), F32)]
    else:
        x_spec = pl.BlockSpec((batch, D_MODEL), lambda c: (0, 0))
        scratch_extra = []
    state_spec = pl.BlockSpec((batch, S5_HALF), lambda c: (0, 0))
    consts = [P["g_mix"], P["w_u"], P["s5_bbar"], P["s5_ar"][batch], P["s5_ai"][batch], x0r, x0i,
              P["s5_cre"], P["s5_cim"], P["s5_d"], P["s5_glu1"], P["s5_glu2"]]
    return pl.pallas_call(
        functools.partial(_s5_kernel, batch=batch, steps=steps, t_start=t_start, hp=hp, reorder=seq),
        grid=(nchunks,),
        in_specs=[x_spec] + [_const_spec(a.shape) for a in consts],
        out_specs=(x_spec, state_spec, state_spec),
        out_shape=(jax.ShapeDtypeStruct(x.shape, F32),
                   jax.ShapeDtypeStruct((batch, S5_HALF), F32),
                   jax.ShapeDtypeStruct((batch, S5_HALF), F32)),
        scratch_shapes=[pltpu.VMEM((rows, S5_HALF), F32), pltpu.VMEM((rows, S5_HALF), F32),
                        pltpu.VMEM((batch, S5_HALF), F32), pltpu.VMEM((batch, S5_HALF), F32)] + scratch_extra,
        compiler_params=_cparams(("arbitrary",)),
        name="s5_branch_hp" if hp else "s5_branch",
    )(x, *consts)


N_KVEC = 5
N_VEC = N_KVEC + 1


def _rw_prep_kernel(x_ref, gm_ref, wrw_ref, sh0_ref, mu_ref, wl_ref, w0_ref, a0_ref, gup_ref,
                    kk_ref, ka_ref, rk_ref, ones_ref,
                    vec_o, g_o, bon_o, sh_o, carry_scr, *, seq, t_start, hp):
    h = _rms(x_ref[...], gm_ref[...])
    p = _mm(h, wrw_ref[...], hp)
    rows = p.shape[0]
    if seq:
        c = pl.program_id(1)

        @pl.when(c == 0)
        def _():
            carry_scr[...] = jnp.zeros_like(carry_scr)

        row = lax.broadcasted_iota(jnp.int32, p.shape, 0)
        prev = jnp.where(row == 0, carry_scr[0:1, :], pltpu.roll(p, 1, 0))
        prev = jnp.where(row + c * rows == t_start, sh0_ref[...], prev)
        carry_scr[0:1, :] = p[rows - 1:rows, :]

        @pl.when(c == pl.num_programs(1) - 1)
        def _():
            sh_o[...] = p[rows - 1:rows, :]
    else:
        prev = sh0_ref[...]
        sh_o[...] = p
    z = p + (prev - p) * mu_ref[...]
    r = z[:, :D_B]
    k = z[:, D_B:2 * D_B]
    v = z[:, 2 * D_B:3 * D_B]
    zwa = z[:, 3 * D_B:3 * D_B + LORA_W + LORA_A]
    zg = z[:, 3 * D_B + LORA_W + LORA_A:]
    lane = lax.broadcasted_iota(jnp.int32, zwa.shape, 1)
    tw = jnp.where(lane < LORA_W, jnp.tanh(zwa), zwa)
    lw = _mm(tw, wl_ref[...], hp)
    w_log = -_softplus(-(w0_ref[...] + lw[:, :D_B])) - 0.5
    decay = jnp.exp(-jnp.exp(w_log))
    a = _sigmoid(a0_ref[...] + lw[:, D_B:])
    g = _mm(_sigmoid(zg), gup_ref[...], hp)
    kk = k * kk_ref[...]
    n2 = _segsum(kk * kk, ones_ref[...], hp)
    kkn = kk * lax.rsqrt(jnp.maximum(n2, 1e-24))
    k2 = k * (1.0 + (a - 1.0) * ka_ref[...])
    rk = _segsum(r * k2 * rk_ref[...], ones_ref[...], hp)
    vec_o[0] = kkn
    vec_o[1] = decay
    vec_o[2] = -(kkn * a)
    vec_o[3] = k2
    vec_o[4] = r
    vec_o[5] = v
    g_o[...] = g
    bon_o[...] = rk * v


def _rw_prep(x, P, sh0, t_start, hp):
    seq = x.ndim == 3
    consts = [P["g_mix"], P["w_rw"]]
    consts2 = [P["rw_mu"], P["rw_wl"], P["rw_w0"], P["rw_a0"], P["rw_g_up"],
               P["rw_k_k"], P["rw_k_a"], P["rw_r_k"], P["ones_head"]]
    if seq:
        batch, t_len, _ = x.shape
        tc = _pick_tile(t_len, 544)
        grid = (batch, t_len // tc)
        row_spec = lambda w: pl.BlockSpec((None, tc, w), lambda b, c: (b, c, 0))
        vec_spec = pl.BlockSpec((N_VEC, None, tc, D_B), lambda b, c: (0, b, c, 0))
        sh_spec = pl.BlockSpec((None, 1, P_RW), lambda b, c: (b, 0, 0))
        sh0 = sh0.reshape(batch, 1, P_RW)
        vec_shape = (N_VEC, batch, t_len, D_B)
        g_shape = (batch, t_len, D_B)
        sh_shape = (batch, 1, P_RW)
        sem = ("arbitrary", "arbitrary")
    else:
        batch = x.shape[0]
        grid = (1,)
        row_spec = lambda w: pl.BlockSpec((batch, w), lambda i: (0, 0))
        vec_spec = pl.BlockSpec((N_VEC, batch, D_B), lambda i: (0, 0, 0))
        sh_spec = row_spec(P_RW)
        vec_shape = (N_VEC, batch, D_B)
        g_shape = (batch, D_B)
        sh_shape = (batch, P_RW)
        sem = ("arbitrary",)
    return pl.pallas_call(
        functools.partial(_rw_prep_kernel, seq=seq, t_start=t_start, hp=hp),
        grid=grid,
        in_specs=[row_spec(D_MODEL)] + [_const_spec(a.shape) for a in consts] + [sh_spec]
                 + [_const_spec(a.shape) for a in consts2],
        out_specs=(vec_spec, row_spec(D_B), row_spec(D_B), sh_spec),
        out_shape=(jax.ShapeDtypeStruct(vec_shape, F32), jax.ShapeDtypeStruct(g_shape, F32),
                   jax.ShapeDtypeStruct(g_shape, F32), jax.ShapeDtypeStruct(sh_shape, F32)),
        scratch_shapes=[pltpu.VMEM((SUBLANES, P_RW), F32)],
        compiler_params=_cparams(sem),
        name="rw_prep_hp" if hp else "rw_prep",
    )(x, *consts, sh0, *consts2)


RW_VH = RW_HEAD // (2 * SUBLANES)
RW_PAIRS = SUBLANES * RW_HEADS
RW_SC = RW_TC // 2


RW_QP = RW_HEAD + 4


def _pair_rows(i):
    return pl.ds(i, RW_PAIRS, stride=RW_QP)


def _pairs_to_rows(x_ref, q_scr, batch):
    for b in range(batch):
        for j in range(DB_TILES):
            tile = x_ref[b, :, j * LANES:(j + 1) * LANES].T
            pair = b * RW_HEADS + 2 * j
            q_scr[pl.ds(pair * RW_QP, RW_HEAD), :] = tile[:RW_HEAD]
            q_scr[pl.ds((pair + 1) * RW_QP, RW_HEAD), :] = tile[RW_HEAD:]


def _rw_kin_kernel(x_ref, o_ref, q_scr, *, batch):
    _pairs_to_rows(x_ref, q_scr, batch)
    for k in range(RW_HEAD):
        m = q_scr[_pair_rows(k), :]
        o_ref[k] = jnp.concatenate([m, m], axis=0).T


def _rw_vin_kernel(x_ref, o_ref, q_scr, *, batch):
    _pairs_to_rows(x_ref, q_scr, batch)
    for vh in range(RW_VH):
        for vs in range(SUBLANES):
            v0 = (vh * SUBLANES + vs) * 2
            pair_tile = jnp.concatenate([q_scr[_pair_rows(v0), :], q_scr[_pair_rows(v0 + 1), :]], axis=0)
            o_ref[vh, pl.ds(vs, RW_TC, stride=SUBLANES), :] = pair_tile.T


def _rw_unlayout_kernel(o3_ref, o_ref, q_scr, *, batch):
    for vh in range(RW_VH):
        for vs in range(SUBLANES):
            v0 = (vh * SUBLANES + vs) * 2
            zt = o3_ref[vh, pl.ds(vs, RW_TC, stride=SUBLANES), :].T
            q_scr[_pair_rows(v0), :] = zt[:RW_PAIRS]
            q_scr[_pair_rows(v0 + 1), :] = zt[RW_PAIRS:]
    def head_normed(pair):
        x = q_scr[pl.ds(pair * RW_QP, RW_HEAD), :]
        mean = jnp.mean(x, axis=0, keepdims=True)
        d = x - mean
        var = jnp.mean(d * d, axis=0, keepdims=True)
        return d * lax.rsqrt(var + GN_EPS)

    for b in range(batch):
        for j in range(DB_TILES):
            pair = b * RW_HEADS + 2 * j
            tile = jnp.concatenate([head_normed(pair), head_normed(pair + 1)], axis=0)
            o_ref[b, :, j * LANES:(j + 1) * LANES] = tile.T


def _rw_scan_kernel(kin_ref, vin_ref, s0_ref, o_ref, sfin_ref, s_scr, *, t_start):
    c = pl.program_id(0)

    @pl.when(c == 0)
    def _():
        s_scr[...] = s0_ref[...]

    def krow(t, vec, k):
        return jnp.broadcast_to(kin_ref[vec * RW_HEAD + k, pl.ds(t, 1), :], (SUBLANES, LANES))

    def acc_add(acc, vh, k, x):
        prev = acc[vh][k % 2]
        acc[vh][k % 2] = x if prev is None else prev + x

    def first_sa(t):
        acc = [[None, None] for _ in range(RW_VH)]
        for k in range(RW_HEAD):
            kkb = krow(t, 0, k)
            for vh in range(RW_VH):
                acc_add(acc, vh, k, s_scr[vh, k] * kkb)
        return tuple(a[0] + a[1] for a in acc)

    def step(t, sa):
        tn = jnp.minimum(t + 1, RW_SC - 1)
        tv = pl.multiple_of(t * SUBLANES, SUBLANES)
        vv = [vin_ref[vh, pl.ds(tv, SUBLANES), :] for vh in range(RW_VH)]
        oacc = [[None, None] for _ in range(RW_VH)]
        nacc = [[None, None] for _ in range(RW_VH)]
        for k in range(RW_HEAD):
            wb = krow(t, 1, k)
            nbb = krow(t, 2, k)
            kb = krow(t, 3, k)
            rb = krow(t, 4, k)
            kkn = krow(tn, 0, k)
            for vh in range(RW_VH):
                s = s_scr[vh, k] * wb + sa[vh] * nbb + vv[vh] * kb
                s_scr[vh, k] = s
                acc_add(oacc, vh, k, s * rb)
                acc_add(nacc, vh, k, s * kkn)
        for vh in range(RW_VH):
            o_ref[vh, pl.ds(tv, SUBLANES), :] = oacc[vh][0] + oacc[vh][1]
        return tuple(a[0] + a[1] for a in nacc)

    lo = jnp.clip(t_start - c * RW_SC, 0, RW_SC)

    @pl.when(lo > 0)
    def _():
        o_ref[...] = jnp.zeros_like(o_ref)

    lax.fori_loop(lo, RW_SC, step, first_sa(jnp.minimum(lo, RW_SC - 1)))
    sfin_ref[...] = s_scr[...]


def _rw_recurrence_long(vecs, s0, t_start):
    _, batch, t_len, _ = vecs.shape
    assert batch == SUBLANES and t_len % RW_TC == 0
    nchunks = t_len // RW_TC
    q_scr = pltpu.VMEM((RW_PAIRS * RW_QP, LANES), F32)
    kin = pl.pallas_call(
        functools.partial(_rw_kin_kernel, batch=batch),
        grid=(nchunks, N_KVEC),
        in_specs=[pl.BlockSpec((None, batch, RW_TC, D_B), lambda c, i: (i, 0, c, 0))],
        out_specs=pl.BlockSpec((None, RW_HEAD, RW_TC, LANES), lambda c, i: (c, i, 0, 0)),
        out_shape=jax.ShapeDtypeStruct((nchunks, N_KVEC * RW_HEAD, RW_TC, LANES), F32),
        scratch_shapes=[q_scr],
        compiler_params=_cparams(("arbitrary", "arbitrary")),
        name="rw_kin",
    )(vecs)
    vin = pl.pallas_call(
        functools.partial(_rw_vin_kernel, batch=batch),
        grid=(nchunks,),
        in_specs=[pl.BlockSpec((None, batch, RW_TC, D_B), lambda c: (N_KVEC, 0, c, 0))],
        out_specs=pl.BlockSpec((None, RW_VH, RW_TC * SUBLANES, LANES), lambda c: (c, 0, 0, 0)),
        out_shape=jax.ShapeDtypeStruct((nchunks, RW_VH, RW_TC * SUBLANES, LANES), F32),
        scratch_shapes=[q_scr],
        compiler_params=_cparams(("arbitrary",)),
        name="rw_vin",
    )(vecs)
    sshape = (RW_VH, RW_HEAD, SUBLANES, LANES)
    s = s0.reshape(batch, RW_HEADS, RW_VH, SUBLANES, 2, RW_HEAD)
    s = jnp.transpose(s, (2, 5, 3, 4, 0, 1)).reshape(sshape)
    state_spec = pl.BlockSpec(sshape, lambda c: (0, 0, 0, 0))
    halves = RW_TC // RW_SC
    vblk = (None, RW_VH, RW_SC * SUBLANES, LANES)
    vmap = lambda c: (c // halves, 0, c % halves, 0)
    o3, sfin = pl.pallas_call(
        functools.partial(_rw_scan_kernel, t_start=t_start),
        grid=(nchunks * halves,),
        in_specs=[pl.BlockSpec((None, N_KVEC * RW_HEAD, RW_SC, LANES), vmap),
                  pl.BlockSpec(vblk, vmap), state_spec],
        out_specs=(pl.BlockSpec(vblk, vmap), state_spec),
        out_shape=(jax.ShapeDtypeStruct((nchunks, RW_VH, RW_TC * SUBLANES, LANES), F32),
                   jax.ShapeDtypeStruct(sshape, F32)),
        scratch_shapes=[pltpu.VMEM(sshape, F32)],
        compiler_params=_cparams(("arbitrary",)),
        name="rw_scan",
    )(kin, vin, s)
    o = pl.pallas_call(
        functools.partial(_rw_unlayout_kernel, batch=batch),
        grid=(nchunks,),
        in_specs=[pl.BlockSpec((None, RW_VH, RW_TC * SUBLANES, LANES), lambda c: (c, 0, 0, 0))],
        out_specs=pl.BlockSpec((batch, RW_TC, D_B), lambda c: (0, c, 0)),
        out_shape=jax.ShapeDtypeStruct((batch, t_len, D_B), F32),
        scratch_shapes=[q_scr],
        compiler_params=_cparams(("arbitrary",)),
        name="rw_unlayout",
    )(o3)
    sfin = sfin.reshape(RW_VH, RW_HEAD, SUBLANES, 2, batch, RW_HEADS)
    sfin = jnp.transpose(sfin, (4, 5, 0, 2, 3, 1)).reshape(batch, RW_HEADS, RW_HEAD, RW_HEAD)
    return o, sfin


STEP_PAIRS = 64
STEP_VROWS = RW_HEAD // 2


def _split3(x):
    hi = x.astype(BF16)
    r1 = x - hi.astype(F32)
    mid = r1.astype(BF16)
    return hi, mid, (r1 - mid.astype(F32)).astype(BF16)


def _rw_step_kernel(s_ref, km_ref, v_ref, sel_ref, vmask_ref, vexp_ref, ones_ref, s_o, o_o):
    sel = sel_ref[...]

    def rows_of(x):
        return sum(jnp.dot(sel, p, preferred_element_type=F32) for p in _split3(x))

    kx = rows_of(km_ref[...])
    kk, w, nb, k, r = [kx[:, i * LANES:(i + 1) * LANES] for i in range(N_KVEC)]
    vx = _segsum(rows_of(v_ref[...]) * vmask_ref[...], vexp_ref[...], True)
    s = s_ref[...]
    ones = ones_ref[...]
    sa = _segsum(s * kk, ones, True)
    s = s * w + sa * nb + vx * k
    s_o[...] = s
    o_o[...] = _segsum(s * r, ones, True)


def _rw_recurrence_step(vecs, s0, ones_half):
    batch = vecs.shape[1]
    pairs = batch * RW_HEADS
    rows = pairs * STEP_VROWS
    tr = STEP_PAIRS * STEP_VROWS
    assert pairs % STEP_PAIRS == 0
    kp = vecs[:N_KVEC].reshape(N_KVEC, pairs, 1, RW_HEAD)
    km = jnp.transpose(jnp.broadcast_to(kp, (N_KVEC, pairs, 2, RW_HEAD)), (1, 0, 2, 3)).reshape(pairs, N_KVEC * LANES)
    vp = vecs[N_KVEC].reshape(pairs, RW_HEAD)
    ridx = jnp.arange(tr, dtype=jnp.int32)
    sel = (ridx[:, None] // STEP_VROWS == jnp.arange(STEP_PAIRS, dtype=jnp.int32)[None, :]).astype(BF16)
    vidx = jnp.arange(RW_HEAD, dtype=jnp.int32)
    vmask = (vidx[None, :] // 2 == ridx[:, None] % STEP_VROWS).astype(F32)
    vexp = (vidx[:, None] % 2 == jnp.arange(LANES, dtype=jnp.int32)[None, :] // RW_HEAD).astype(BF16)
    spec = pl.BlockSpec((tr, LANES), lambda i: (i, 0))
    big = jax.ShapeDtypeStruct((rows, LANES), F32)
    consts = [sel, vmask, vexp, ones_half]
    s_new, o = pl.pallas_call(
        _rw_step_kernel,
        grid=(pairs // STEP_PAIRS,),
        in_specs=[spec, pl.BlockSpec((STEP_PAIRS, N_KVEC * LANES), lambda i: (i, 0)),
                  pl.BlockSpec((STEP_PAIRS, RW_HEAD), lambda i: (i, 0))]
                 + [_const_spec(a.shape) for a in consts],
        out_specs=(spec, spec),
        out_shape=(big, big),
        compiler_params=_cparams(("arbitrary",)),
        name="rw_step",
    )(s0.reshape(rows, LANES), km, vp, *consts)
    o = o.reshape(rows, 2, RW_HEAD)[:, :, 0].reshape(batch, D_B)
    return o, s_new.reshape(batch, RW_HEADS, RW_HEAD, RW_HEAD)


def _mix_kernel(*refs, hp, moe, o_normed):
    (x_ref, ya_ref, o_ref, bon_ref, g_ref, gm_ref, wg_ref, gng_ref, gnb_ref, ones_ref,
     wbo_ref, wout_ref, gf_ref) = refs[:13]
    if moe:
        wr_ref, x1_o, h2_o, route_o = refs[13:]
    else:
        x1_o, h2_o = refs[13:]
    x = x_ref[...]
    h = _rms(x, gm_ref[...])
    gates = _mm(h, wg_ref[...], hp)
    o = o_ref[...]
    ones = ones_ref[...]
    if not o_normed:
        inv_n = 1.0 / RW_HEAD
        mean = _segsum(o, ones, hp) * inv_n
        dlt = o - mean
        var = _segsum(dlt * dlt, ones, hp) * inv_n
        o = dlt * lax.rsqrt(var + GN_EPS)
    on = o * gng_ref[...] + gnb_ref[...]
    yb = _mm((on + bon_ref[...]) * g_ref[...], wbo_ref[...], hp)
    m = _sigmoid(gates[:, :D_MODEL]) * ya_ref[...] + _sigmoid(gates[:, D_MODEL:]) * yb
    x1 = x + _mm(m, wout_ref[...], hp)
    h2 = _rms(x1, gf_ref[...])
    x1_o[...] = x1
    h2_o[...] = h2.astype(h2_o.dtype)
    if moe:
        logits = _mm(h2, wr_ref[...], hp)
        lane = lax.broadcasted_iota(jnp.int32, logits.shape, 1).astype(F32)
        neg = jnp.float32(-jnp.inf)
        lg = jnp.where(lane < N_EXPERTS, logits, neg)
        m1 = jnp.max(lg, axis=1, keepdims=True)
        i1 = jnp.min(jnp.where(lg == m1, lane, float(LANES)), axis=1, keepdims=True)
        lg2 = jnp.where(lane == i1, neg, lg)
        m2 = jnp.max(lg2, axis=1, keepdims=True)
        i2 = jnp.min(jnp.where(lg2 == m2, lane, float(LANES)), axis=1, keepdims=True)
        e = jnp.exp(m2 - m1)
        g1 = 1.0 / (1.0 + e)
        g2 = e / (1.0 + e)
        route_o[...] = jnp.where(lane == 0.0, i1, jnp.where(lane == 1.0, i2,
                                 jnp.where(lane == 2.0, g1, jnp.where(lane == 3.0, g2, 0.0))))


def _mix(x, ya, o, bon, g, P, hp, moe, o_normed):
    rows = x.shape[0]
    tm = _pick_tile(rows, 344)
    row_spec = lambda w: pl.BlockSpec((tm, w), lambda i: (i, 0))
    consts = [P["g_mix"], P["w_gate"], P["rw_gn_g"], P["rw_gn_b"], P["ones_head"],
              P["rw_w_bo"], P["w_out"], P["g_ffn"]]
    if moe:
        consts.append(P["w_router"])
    h2_dtype = F32 if (hp or moe) else BF16
    out_specs = [row_spec(D_MODEL), row_spec(D_MODEL)]
    out_shape = [jax.ShapeDtypeStruct((rows, D_MODEL), F32), jax.ShapeDtypeStruct((rows, D_MODEL), h2_dtype)]
    if moe:
        out_specs.append(row_spec(LANES))
        out_shape.append(jax.ShapeDtypeStruct((rows, LANES), F32))
    return pl.pallas_call(
        functools.partial(_mix_kernel, hp=hp, moe=moe, o_normed=o_normed),
        grid=(rows // tm,),
        in_specs=[row_spec(D_MODEL), row_spec(D_MODEL), row_spec(D_B), row_spec(D_B), row_spec(D_B)]
                 + [_const_spec(a.shape) for a in consts],
        out_specs=tuple(out_specs),
        out_shape=tuple(out_shape),
        compiler_params=_cparams(("arbitrary",)),
        name=("mix_moe" if moe else "mix") + ("_hp" if hp else ""),
    )(x, ya, o, bon, g, *consts)


def _ffn_kernel(h_ref, x1_ref, w1_ref, w3_ref, w2_ref, o_ref, *, hp):
    h = h_ref[...]
    a = _mm(h, w1_ref[...], hp)
    b = _mm(h, w3_ref[...], hp)
    o_ref[...] = x1_ref[...] + _mm(a * _sigmoid(a) * b, w2_ref[...], hp)


def _ffn(h2, x1, w1, w3, w2, tm, hp):
    rows = h2.shape[0]
    assert rows % tm == 0
    row_spec = pl.BlockSpec((tm, D_MODEL), lambda i: (i, 0))
    return pl.pallas_call(
        functools.partial(_ffn_kernel, hp=hp),
        grid=(rows // tm,),
        in_specs=[row_spec, row_spec, _const_spec(w1.shape), _const_spec(w3.shape), _const_spec(w2.shape)],
        out_specs=row_spec,
        out_shape=jax.ShapeDtypeStruct((rows, D_MODEL), F32),
        compiler_params=_cparams(("arbitrary",)),
        name="ffn_hp" if hp else "ffn",
    )(h2, x1, w1, w3, w2)


MOE_TM = 512
MOE_TF = 1792


def _moe_kernel(te_ref, nv_ref, x_ref, w1_ref, w3_ref, w2_ref, o_ref, acc):
    i = pl.program_id(0)
    f = pl.program_id(1)

    @pl.when(f == 0)
    def _():
        acc[...] = jnp.zeros_like(acc)

    @pl.when(i < nv_ref[0])
    def _():
        x = x_ref[...].astype(BF16)
        a = jnp.dot(x, w1_ref[...], preferred_element_type=F32)
        b = jnp.dot(x, w3_ref[...], preferred_element_type=F32)
        acc[...] += jnp.dot((a * _sigmoid(a) * b).astype(BF16), w2_ref[...], preferred_element_type=F32)

    @pl.when(f == pl.num_programs(1) - 1)
    def _():
        o_ref[...] = acc[...]


def _moe_experts(xs, tile_expert, n_valid, w1, w3, w2):
    rows = xs.shape[0]
    ntiles = rows // MOE_TM
    nf = D_EXPERT // MOE_TF
    assert ntiles * MOE_TM == rows and nf * MOE_TF == D_EXPERT

    def fblk(i, f, nv):
        walk = lambda t, g: jnp.where(t % 2 == 0, g, nf - 1 - g)
        return jnp.where(i < nv[0], walk(i, f), walk(nv[0] - 1, nf - 1))

    grid_spec = pltpu.PrefetchScalarGridSpec(
        num_scalar_prefetch=2,
        grid=(ntiles, nf),
        in_specs=[pl.BlockSpec((MOE_TM, D_MODEL), lambda i, f, te, nv: (i, 0)),
                  pl.BlockSpec((None, D_MODEL, MOE_TF), lambda i, f, te, nv: (te[i], 0, fblk(i, f, nv))),
                  pl.BlockSpec((None, D_MODEL, MOE_TF), lambda i, f, te, nv: (te[i], 0, fblk(i, f, nv))),
                  pl.BlockSpec((None, MOE_TF, D_MODEL), lambda i, f, te, nv: (te[i], fblk(i, f, nv), 0))],
        out_specs=pl.BlockSpec((MOE_TM, D_MODEL), lambda i, f, te, nv: (i, 0)),
        scratch_shapes=[pltpu.VMEM((MOE_TM, D_MODEL), F32)])
    return pl.pallas_call(
        _moe_kernel,
        grid_spec=grid_spec,
        out_shape=jax.ShapeDtypeStruct((rows, D_MODEL), F32),
        compiler_params=_cparams(("arbitrary", "arbitrary")),
        name="moe_experts",
    )(tile_expert, n_valid, xs, w1, w3, w2)


def _moe_plan(expert_idx, src_row):
    n_pairs = expert_idx.shape[0] * 2
    flat_e = expert_idx.reshape(n_pairs)
    onehot = (flat_e[:, None] == jnp.arange(N_EXPERTS, dtype=jnp.int32)[None, :]).astype(jnp.int32)
    csum = jnp.cumsum(onehot, axis=0)
    rank = jnp.sum(csum * onehot, axis=1) - 1
    counts = csum[-1]
    padded = ((counts + MOE_TM - 1) // MOE_TM) * MOE_TM
    pend = jnp.cumsum(padded)
    pstart = pend - padded
    dest = pstart[flat_e] + rank
    ntiles = (n_pairs + N_EXPERTS * (MOE_TM - 1)) // MOE_TM + 1
    rows = ntiles * MOE_TM
    tile_start = jnp.arange(ntiles, dtype=jnp.int32) * MOE_TM
    n_valid = (pend[-1] // MOE_TM).astype(jnp.int32)
    te = jnp.sum((pend[None, :] <= tile_start[:, None]).astype(jnp.int32), axis=1)
    te = jnp.minimum(te, N_EXPERTS - 1)
    order = jnp.argsort(flat_e, stable=True).astype(jnp.int32)
    gstart = jnp.cumsum(counts) - counts
    slot = jnp.arange(rows, dtype=jnp.int32)
    e_slot = jnp.repeat(te, MOE_TM)
    off = slot - pstart[e_slot]
    filled = (off < counts[e_slot]) & (slot < pend[-1])
    pair = order[jnp.clip(gstart[e_slot] + off, 0, n_pairs - 1)]
    src = jnp.where(filled, src_row[pair // 2], 0).astype(jnp.int32)
    last_e = te[jnp.maximum(n_valid - 1, 0)]
    te = jnp.where(jnp.arange(ntiles) < n_valid, te, last_e).astype(jnp.int32)
    return src, dest.reshape(-1, 2), te, n_valid.reshape(1)


def _combine_kernel(x1_ref, y1_ref, y2_ref, route_ref, gf_ref, o_ref):
    route = route_ref[...]
    x2 = x1_ref[...] + route[:, 2:3] * y1_ref[...] + route[:, 3:4] * y2_ref[...]
    o_ref[...] = _rms(x2, gf_ref[...])


def _combine_norm(x1, x1_spec, flat_blk0, grid, out_shape, out_spec, y1, y2, route, g_final, tm):
    nd = len(grid)

    def flat_map(*idx):
        lin = idx[0]
        for k in range(1, nd):
            lin = lin * grid[k] + idx[k]
        return (flat_blk0 + lin, 0)

    return pl.pallas_call(
        _combine_kernel,
        grid=grid,
        in_specs=[x1_spec, pl.BlockSpec((tm, D_MODEL), flat_map), pl.BlockSpec((tm, D_MODEL), flat_map),
                  pl.BlockSpec((tm, LANES), flat_map), _const_spec(g_final.shape)],
        out_specs=out_spec,
        out_shape=out_shape,
        compiler_params=_cparams(("arbitrary",) * nd),
        name="combine_norm",
    )(x1, y1, y2, route, g_final)


MATMUL_WEIGHTS = ("w_u", "w_rw", "w_gate", "s5_glu1", "s5_glu2", "rw_g_up", "rw_w_bo", "w_out", "rw_wl",
                  "s5_bbar", "s5_cre", "s5_cim", "w_router")


def _layer_params(l, W, hp, batches):
    wdt = F32 if hp else BF16
    row = lambda a: a.reshape(1, -1).astype(F32)
    w_in = W["w_in"][l]
    P = {
        "g_mix": row(W["norm_mix"][l]),
        "g_ffn": row(W["norm_ffn"][l]),
        "w_u": w_in[:, :D_A].astype(wdt),
        "w_rw": w_in[:, D_A:D_A + P_RW].astype(wdt),
        "w_gate": w_in[:, D_A + P_RW:].astype(wdt),
        "s5_d": row(W["s5_d"][l]),
        "s5_glu1": W["s5_glu1"][l].astype(wdt),
        "s5_glu2": W["s5_glu2"][l].astype(wdt),
        "rw_mu": row(W["rw_mu"][l]),
        "rw_w0": row(W["rw_w0"][l]),
        "rw_a0": row(W["rw_a0"][l]),
        "rw_g_up": W["rw_g_up"][l].astype(wdt),
        "rw_k_k": row(W["rw_k_k"][l]),
        "rw_k_a": row(W["rw_k_a"][l]),
        "rw_r_k": row(W["rw_r_k"][l]),
        "rw_gn_g": row(W["rw_gn_g"][l]),
        "rw_gn_b": row(W["rw_gn_b"][l]),
        "rw_w_bo": W["rw_w_bo"][l].astype(wdt),
        "w_out": W["w_out"][l].astype(wdt),
    }
    wl = jnp.zeros((LORA_W + LORA_A, 2 * D_B), F32)
    wl = wl.at[:LORA_W, :D_B].set(W["rw_w_up"][l]).at[LORA_W:, D_B:].set(W["rw_a_up"][l])
    P["rw_wl"] = wl.astype(wdt)
    P["ones_head"] = jnp.kron(jnp.eye(RW_HEADS, dtype=F32), jnp.ones((RW_HEAD, RW_HEAD), F32)).astype(BF16)

    ab_re, ab_im, bb_re, bb_im = _s5_discretize(W["s5_lam_re"][l], W["s5_lam_im"][l], W["s5_log_dt"][l],
                                                W["s5_b_re"][l], W["s5_b_im"][l])
    P["s5_ar"] = {b: jnp.broadcast_to(ab_re.reshape(1, S5_HALF), (b, S5_HALF)) for b in batches}
    P["s5_ai"] = {b: jnp.broadcast_to(ab_im.reshape(1, S5_HALF), (b, S5_HALF)) for b in batches}
    gpt = S5_GROUPS // S5_TILES
    eye = jnp.eye(gpt, dtype=F32)
    bb = jnp.stack([bb_re, bb_im]).reshape(2, S5_GROUP, S5_TILES, gpt, S5_STATE)
    bb = jnp.transpose(bb, (2, 1, 0, 3, 4))
    bbar = bb[:, None] * eye[None, :, None, None, :, None]
    P["s5_bbar"] = bbar.reshape(S5_TILES, LANES, 2 * gpt * S5_STATE).astype(wdt)
    cs = jnp.stack([W["s5_c_re"][l], -W["s5_c_im"][l]]).reshape(2, S5_TILES, gpt, S5_GROUP, S5_STATE)
    cs = jnp.transpose(cs, (1, 0, 2, 4, 3))
    cm = cs[:, :, :, :, None, :] * eye[None, None, :, None, :, None]
    cm = cm.reshape(S5_TILES, 2, gpt * S5_STATE, LANES).astype(wdt)
    P["s5_cre"] = cm[:, 0]
    P["s5_cim"] = cm[:, 1]
    return P


def _mixer_seq(x, t_start, s5r0, s5i0, rw_s0, rw_sh0, P, moe):
    batch, t_len, _ = x.shape
    ya, xr, xi = _s5_branch(x, batch, t_len, t_start, P, s5r0, s5i0, False)
    vecs, g, bon, sh = _rw_prep(x, P, rw_sh0, t_start, False)
    o, s_fin = _rw_recurrence_long(vecs, rw_s0, t_start)
    flat = lambda a: a.reshape(batch * t_len, a.shape[-1])
    outs = _mix(flat(x), flat(ya), flat(o), flat(bon), flat(g), P, False, moe, True)
    return outs, (xr, xi, s_fin, sh.reshape(batch, P_RW))


def _mixer_step(x, s5r0, s5i0, rw_s0, rw_sh0, P, moe, ones_half):
    batch = x.shape[0]
    ya, xr, xi = _s5_branch(x, batch, 1, 0, P, s5r0, s5i0, True)
    vecs, g, bon, sh = _rw_prep(x, P, rw_sh0, 0, True)
    o, s_fin = _rw_recurrence_step(vecs, rw_s0, ones_half)
    outs = _mix(x, ya, o, bon, g, P, True, moe, False)
    return outs, (xr, xi, s_fin, sh)


def kernel(x_prompt, x_sample, state_s5_re, state_s5_im, state_rwkv, state_shift, meta_tokens, norm_mix, w_in, s5_lam_re, s5_lam_im, s5_log_dt, s5_b_re, s5_b_im, s5_c_re, s5_c_im, s5_d, s5_glu1, s5_glu2, rw_mu, rw_w0, rw_w_up, rw_a0, rw_a_up, rw_g_up, rw_k_k, rw_k_a, rw_r_k, rw_gn_g, rw_gn_b, rw_w_bo, w_out, norm_ffn, ffn_w1, ffn_w3, ffn_w2, moe_router, moe_w1, moe_w3, moe_w2, norm_final):
    W = dict(norm_mix=norm_mix, w_in=w_in, s5_lam_re=s5_lam_re, s5_lam_im=s5_lam_im,
             s5_log_dt=s5_log_dt, s5_b_re=s5_b_re, s5_b_im=s5_b_im, s5_c_re=s5_c_re,
             s5_c_im=s5_c_im, s5_d=s5_d, s5_glu1=s5_glu1, s5_glu2=s5_glu2, rw_mu=rw_mu,
             rw_w0=rw_w0, rw_w_up=rw_w_up, rw_a0=rw_a0, rw_a_up=rw_a_up, rw_g_up=rw_g_up,
             rw_k_k=rw_k_k, rw_k_a=rw_k_a, rw_r_k=rw_r_k, rw_gn_g=rw_gn_g, rw_gn_b=rw_gn_b,
             rw_w_bo=rw_w_bo, w_out=w_out, norm_ffn=norm_ffn)
    bp, seq, _ = x_prompt.shape
    bs = x_sample.shape[0]
    assert x_sample.shape[1] == 1 and seq % RW_TC == 0 and bp == SUBLANES
    tp = N_META + seq
    t_pad = -(-tp // RW_TC) * RW_TC
    t_start = t_pad - tp
    out0 = t_pad - seq

    meta = jnp.broadcast_to(meta_tokens.astype(F32)[None], (bp, N_META, D_MODEL))
    xp = jnp.concatenate([jnp.zeros((bp, t_start, D_MODEL), F32), meta, x_prompt], axis=1)
    xs = x_sample.reshape(bs, D_MODEL)
    ones_half = jnp.kron(jnp.eye(2, dtype=F32), jnp.ones((RW_HEAD, RW_HEAD), F32)).astype(BF16)
    g_final = norm_final.reshape(1, D_MODEL).astype(F32)

    zero_s5 = jnp.zeros((bp, S5_HALF), F32)
    zero_rw = jnp.zeros((bp, RW_HEADS, RW_HEAD, RW_HEAD), F32)
    zero_sh = jnp.zeros((bp, P_RW), F32)

    p_states, s_states = [], []
    for l in range(DEPTH):
        moe = (l % 2 == 1)
        j = l // 2
        Ph = _layer_params(l, W, True, (bp, bs))
        if moe:
            Ph["w_router"] = jnp.zeros((D_MODEL, LANES), F32).at[:, :N_EXPERTS].set(moe_router[j])
        Pl = {k: (v.astype(BF16) if k in MATMUL_WEIGHTS else v) for k, v in Ph.items()}
        outs_p, st_p = _mixer_seq(xp, t_start, zero_s5, zero_s5, zero_rw, zero_sh, Pl, moe)
        outs_s, st_s = _mixer_step(xs, state_s5_re[l].reshape(bs, S5_HALF), state_s5_im[l].reshape(bs, S5_HALF),
                                   state_rwkv[l], state_shift[l], Ph, moe, ones_half)
        p_states.append(st_p)
        s_states.append(st_s)
        if not moe:
            rows = bp * t_pad
            xp = _ffn(outs_p[1], outs_p[0], ffn_w1[j].astype(BF16), ffn_w3[j].astype(BF16),
                      ffn_w2[j].astype(BF16), _pick_tile(rows, 544), False).reshape(bp, t_pad, D_MODEL)
            xs = _ffn(outs_s[1], outs_s[0], ffn_w1[j], ffn_w3[j], ffn_w2[j], bs, True)
        else:
            assert l == DEPTH - 1
            n_p = bp * seq
            route_p = outs_p[2].reshape(bp, t_pad, LANES)[:, out0:].reshape(n_p, LANES)
            route = jnp.concatenate([route_p, outs_s[2]], axis=0)
            tok = jnp.arange(n_p, dtype=jnp.int32)
            if bp * t_start >= bs:
                free = [b * t_pad + t for b in range(bp) for t in range(t_start)][:bs]
                sample_rows = jnp.asarray(free, dtype=jnp.int32)
                h2 = outs_p[1].at[sample_rows].set(outs_s[1])
            else:
                sample_rows = bp * t_pad + jnp.arange(bs, dtype=jnp.int32)
                h2 = jnp.concatenate([outs_p[1], outs_s[1]], axis=0)
            src_row = jnp.concatenate([(tok // seq) * t_pad + out0 + tok % seq, sample_rows])
            src, dest, te, n_valid = _moe_plan(route[:, :2].astype(jnp.int32), src_row)
            take = functools.partial(jnp.take, axis=0, mode="clip")
            y_sorted = _moe_experts(take(h2, src), te, n_valid,
                                    moe_w1[j].astype(BF16), moe_w3[j].astype(BF16), moe_w2[j].astype(BF16))
            y1 = take(y_sorted, dest[:, 0])
            y2 = take(y_sorted, dest[:, 1])
            tm = RW_TC
            nblk = seq // tm
            y_prompt = _combine_norm(
                outs_p[0].reshape(bp, t_pad, D_MODEL),
                pl.BlockSpec((None, tm, D_MODEL), lambda b, s: (b, s + out0 // tm, 0)), 0, (bp, nblk),
                jax.ShapeDtypeStruct((bp, seq, D_MODEL), F32),
                pl.BlockSpec((None, tm, D_MODEL), lambda b, s: (b, s, 0)), y1, y2, route, g_final, tm)
            assert n_p % bs == 0
            y_sample = _combine_norm(
                outs_s[0], pl.BlockSpec((bs, D_MODEL), lambda i: (0, 0)), n_p // bs, (1,),
                jax.ShapeDtypeStruct((bs, D_MODEL), F32),
                pl.BlockSpec((bs, D_MODEL), lambda i: (0, 0)), y1, y2, route, g_final, bs)

    y_sample = y_sample.reshape(bs, 1, D_MODEL)

    def stack(states, b):
        re = jnp.stack([s[0].reshape(b, S5_GROUPS, S5_STATE) for s in states])
        im = jnp.stack([s[1].reshape(b, S5_GROUPS, S5_STATE) for s in states])
        rw = jnp.stack([s[2] for s in states])
        sh = jnp.stack([s[3] for s in states])
        return re, im, rw, sh

    p_re, p_im, p_rw, p_sh = stack(p_states, bp)
    s_re, s_im, s_rw, s_sh = stack(s_states, bs)
    return (y_prompt, y_sample, p_re, p_im, p_rw, p_sh, s_re, s_im, s_rw, s_sh)
```
